```python
import jax
import jax.numpy as jnp
from jax import lax
import numpy as np

D_MODEL = 1024
BATCH = 2
SEQ = 16384
DEPTH = 1
DEC_BATCH = 128
DEC_SEQ = 4
PAST_LEN = 8192
PAGE_SIZE = 128

HEAD_DIM = 64
N_FOX_HEADS = 8
N_RWKV_HEADS = 8
FOX_WIDTH = N_FOX_HEADS * HEAD_DIM
RWKV_WIDTH = N_RWKV_HEADS * HEAD_DIM
MIX_WIDTH = FOX_WIDTH + RWKV_WIDTH
DECAY_LORA = 64
ICLR_LORA = 64
GATE_LORA = 128
FOX_COLS = 3 * FOX_WIDTH + N_FOX_HEADS
RWKV_COLS = 3 * RWKV_WIDTH + DECAY_LORA + ICLR_LORA + GATE_LORA
IN_COLS = FOX_COLS + RWKV_COLS
Q_BLOCK = 128
ATTN_SCALE = HEAD_DIM ** -0.5
N_EXPERTS = 32
TOP_K = 4
D_EXPERT = D_MODEL
SWIGLU_ALPHA = 1.702
SWIGLU_LIMIT = 7.0
RMS_EPS = 1e-6
GN_EPS = 64e-5
N_MOD = 6

kernel_name = 'fox_rwkv7_parallel_moe_adaln_step'


def rms_norm(x, g):
    x32 = x.astype(jnp.float32)
    y = x32 * lax.rsqrt(jnp.mean(x32 * x32, axis=-1, keepdims=True) + RMS_EPS)
    return (y * g.astype(jnp.float32)).astype(x.dtype)


def fox_heads(p_fox, p):
    B, T, _ = p_fox.shape
    q, k, v, fz = jnp.split(p_fox, [FOX_WIDTH, 2 * FOX_WIDTH, 3 * FOX_WIDTH], axis=-1)
    sh = (B, T, N_FOX_HEADS, HEAD_DIM)
    q = rms_norm(q.reshape(sh), p['q_norm'])
    k = rms_norm(k.reshape(sh), p['k_norm'])
    v = v.reshape(sh)
    logf = jax.nn.log_sigmoid(fz.astype(jnp.float32) + p['b_forget'].astype(jnp.float32))
    return q, k, v, logf


def fox_prompt(q, k, v, logf):
    B, T, H, Dh = q.shape
    nb = T // Q_BLOCK
    FT = jnp.swapaxes(jnp.cumsum(logf, axis=1), 1, 2)
    k_pos = jnp.arange(T)
    qb = jnp.swapaxes(q.reshape(B, nb, Q_BLOCK, H, Dh), 0, 1)
    fqb = FT.reshape(B, H, nb, Q_BLOCK).transpose(2, 0, 1, 3)
    qpb = k_pos.reshape(nb, Q_BLOCK)

    def block(args):
        q_i, fq_i, qp_i = args
        s = jnp.einsum('bqhd,bkhd->bhqk', q_i, k, preferred_element_type=jnp.float32) * ATTN_SCALE
        s = s + (fq_i[..., :, None] - FT[..., None, :])
        s = jnp.where(k_pos[None, :] <= qp_i[:, None], s, -jnp.inf)
        pr = jax.nn.softmax(s, axis=-1)
        return jnp.einsum('bhqk,bkhd->bqhd', pr.astype(v.dtype), v)

    o = lax.map(block, (qb, fqb, qpb))
    return jnp.swapaxes(o, 0, 1).reshape(B, T, H * Dh)


def fox_decode(q, k, v, logf, k_past, v_past, logf_past):
    B, T, H, Dh = q.shape
    P = k_past.shape[1]
    F_all = jnp.cumsum(jnp.concatenate([logf_past.astype(jnp.float32), logf], axis=1), axis=1)
    FT = jnp.swapaxes(F_all, 1, 2)
    s_past = jnp.einsum('bqhd,bkhd->bhqk', q, k_past, preferred_element_type=jnp.float32)
    s_new = jnp.einsum('bqhd,bkhd->bhqk', q, k, preferred_element_type=jnp.float32)
    s = jnp.concatenate([s_past, s_new], axis=-1) * ATTN_SCALE
    s = s + (FT[..., P:, None] - FT[..., None, :])
    q_pos = P + jnp.arange(T)
    k_pos = jnp.arange(P + T)
    s = jnp.where(k_pos[None, :] <= q_pos[:, None], s, -jnp.inf)
    pr = jax.nn.softmax(s, axis=-1).astype(v.dtype)
    o = (jnp.einsum('bhqk,bkhd->bqhd', pr[..., :P], v_past)
         + jnp.einsum('bhqk,bkhd->bqhd', pr[..., P:], v))
    return o.reshape(B, T, H * Dh)


def wkv7_scan(state, r, k, v, decay, kk, b):
    def step(S, inp):
        r_t, k_t, v_t, w_t, kk_t, b_t = inp
        sa = jnp.einsum('bhij,bhj->bhi', S, -kk_t)
        S = (S * w_t[:, :, None, :] + sa[..., None] * b_t[:, :, None, :]
             + v_t[..., None] * k_t[:, :, None, :])
        return S, jnp.einsum('bhij,bhj->bhi', S, r_t)

    xs = tuple(jnp.swapaxes(t, 0, 1) for t in (r, k, v, decay, kk, b))
    S, ys = lax.scan(step, state, xs)
    return S, jnp.swapaxes(ys, 0, 1)


def rwkv_mix(p_rw, shift_prev, wkv_prev, p):
    B, T, _ = p_rw.shape
    z = p_rw.astype(jnp.float32)
    prev = jnp.concatenate([shift_prev.astype(jnp.float32)[:, None, :], z[:, :-1]], axis=1)
    z = z + (prev - z) * p['rw_mu']
    r, k, v, wd, ad, gd = jnp.split(z, [RWKV_WIDTH, 2 * RWKV_WIDTH, 3 * RWKV_WIDTH,
                                        3 * RWKV_WIDTH + DECAY_LORA,
                                        3 * RWKV_WIDTH + DECAY_LORA + ICLR_LORA], axis=-1)
    w = -jax.nn.softplus(-(p['rw_w0'] + jnp.tanh(wd) @ p['rw_w2'])) - 0.5
    decay = jnp.exp(-jnp.exp(w))
    a = jax.nn.sigmoid(p['rw_a0'] + ad @ p['rw_a2'])
    g = jax.nn.sigmoid(gd) @ p['rw_g2']
    hs = lambda t: t.reshape(B, T, N_RWKV_HEADS, HEAD_DIM)
    kk = hs(k * p['rw_kk'])
    kk = kk / jnp.maximum(jnp.sqrt(jnp.sum(kk * kk, axis=-1, keepdims=True)), 1e-12)
    k = k * (1.0 + (a - 1.0) * p['rw_ka'])
    r, k, v, decay, a = hs(r), hs(k), hs(v), hs(decay), hs(a)
    wkv_new, y = wkv7_scan(wkv_prev.astype(jnp.float32), r, k, v, decay, kk, kk * a)
    mu = jnp.mean(y, axis=-1, keepdims=True)
    var = jnp.mean(jnp.square(y - mu), axis=-1, keepdims=True)
    y = ((y - mu) * lax.rsqrt(var + GN_EPS)).reshape(B, T, RWKV_WIDTH) * p['lnx_w'] + p['lnx_b']
    bonus = jnp.sum(r * k * p['rw_rk'], axis=-1, keepdims=True) * v
    out = (y + bonus.reshape(B, T, RWKV_WIDTH)) * g
    return out, wkv_new


def moe_ffn(h, p):
    B, T, D = h.shape
    hf = h.reshape(B * T, D)
    logits = (hf @ p['w_router'] + p['b_router']).astype(jnp.float32)
    top_v, top_i = lax.top_k(logits, TOP_K)
    top_w = jax.nn.softmax(top_v, axis=-1)
    comb = jnp.einsum('nk,nke->ne', top_w, jax.nn.one_hot(top_i, N_EXPERTS, dtype=jnp.float32))
    out = jnp.zeros((B * T, D), jnp.float32)
    for e in range(N_EXPERTS):
        glu = jnp.minimum(hf @ p['w_gate'][e] + p['b_gate'][e], SWIGLU_LIMIT)
        lin = jnp.clip(hf @ p['w_up'][e] + p['b_up'][e], -SWIGLU_LIMIT, SWIGLU_LIMIT)
        act = glu * jax.nn.sigmoid(SWIGLU_ALPHA * glu) * (lin + 1.0)
        out = out + comb[:, e:e + 1] * (act @ p['w_down'][e] + p['b_down'][e])
    return out.astype(h.dtype).reshape(B, T, D)


def hybrid_layer(x, c, p, shift_prev, wkv_prev, past):
    mod = jax.nn.silu(c) @ p['w_mod'] + p['b_mod']
    sh_m, sc_m, gt_m, sh_f, sc_f, gt_f = jnp.split(mod[:, None, :], N_MOD, axis=-1)
    h = rms_norm(x, p['g_mix']) * (1.0 + sc_m) + sh_m
    proj = h @ p['w_in']
    p_fox, p_rw = jnp.split(proj, [FOX_COLS], axis=-1)
    q, k, v, logf = fox_heads(p_fox, p)
    if past is None:
        o_fox = fox_prompt(q, k, v, logf)
    else:
        o_fox = fox_decode(q, k, v, logf, *past)
    o_rw, wkv_new = rwkv_mix(p_rw, shift_prev, wkv_prev, p)
    mixed = jnp.concatenate([o_fox, o_rw.astype(o_fox.dtype)], axis=-1) @ p['w_out']
    x = x + gt_m * mixed
    h = rms_norm(x, p['g_ffn']) * (1.0 + sc_f) + sh_f
    x = x + gt_f * moe_ffn(h, p)
    return x, (k, v, logf, wkv_new, p_rw[:, -1])


def setup_inputs(seed: int = 0) -> dict:
    key = jax.random.key(seed)
    keys = iter(jax.random.split(key, 64))
    f32 = jnp.float32
    D = D_MODEL

    def nrm(shape, scale):
        return jax.random.normal(next(keys), shape, f32) * scale

    def unif(shape, lo, hi):
        return jax.random.uniform(next(keys), shape, f32, lo, hi)

    n_pages = PAST_LEN // PAGE_SIZE
    n_used = DEC_BATCH * n_pages
    n_pool = n_used + max(n_used // 4, 1)
    perm = jax.random.permutation(next(keys), n_pool)
    page_table = perm[:n_used].reshape(DEC_BATCH, n_pages).astype(jnp.int32)
    return {
        'x_prompt': nrm((BATCH, SEQ, D), 1.0),
        'x_sample': nrm((DEC_BATCH, DEC_SEQ, D), 1.0),
        'cache_k': nrm((DEPTH, n_pool, PAGE_SIZE, N_FOX_HEADS, HEAD_DIM), 1.0),
        'cache_v': nrm((DEPTH, n_pool, PAGE_SIZE, N_FOX_HEADS, HEAD_DIM), 1.0),
        'cache_logf': jax.nn.log_sigmoid(unif((DEPTH, n_pool, PAGE_SIZE, N_FOX_HEADS), 1.0, 6.0)),
        'state_wkv': nrm((DEPTH, DEC_BATCH, N_RWKV_HEADS, HEAD_DIM, HEAD_DIM), 1.0),
        'state_shift': nrm((DEPTH, DEC_BATCH, RWKV_COLS), 1.0),
        'page_table': page_table,
        'c_prompt': nrm((BATCH, D), 1.0),
        'c_sample': nrm((DEC_BATCH, D), 1.0),
        'w_mod': nrm((DEPTH, D, N_MOD * D), 0.5 * D ** -0.5),
        'b_mod': nrm((DEPTH, N_MOD * D), 0.02),
        'g_mix': 1.0 + nrm((DEPTH, D), 0.05),
        'g_ffn': 1.0 + nrm((DEPTH, D), 0.05),
        'w_in': nrm((DEPTH, D, IN_COLS), D ** -0.5),
        'q_norm': 1.0 + nrm((DEPTH, HEAD_DIM), 0.05),
        'k_norm': 1.0 + nrm((DEPTH, HEAD_DIM), 0.05),
        'b_forget': unif((DEPTH, N_FOX_HEADS), 1.0, 6.0),
        'rw_mu': unif((DEPTH, RWKV_COLS), 0.0, 1.0),
        'rw_w0': unif((DEPTH, RWKV_WIDTH), -6.0, -1.0),
        'rw_w2': nrm((DEPTH, DECAY_LORA, RWKV_WIDTH), 0.1 * DECAY_LORA ** -0.5),
        'rw_a0': nrm((DEPTH, RWKV_WIDTH), 0.1),
        'rw_a2': nrm((DEPTH, ICLR_LORA, RWKV_WIDTH), 0.1 * ICLR_LORA ** -0.5),
        'rw_g2': nrm((DEPTH, GATE_LORA, RWKV_WIDTH), GATE_LORA ** -0.5),
        'rw_kk': 0.85 + nrm((DEPTH, RWKV_WIDTH), 0.05),
        'rw_ka': 1.0 + nrm((DEPTH, RWKV_WIDTH), 0.05),
        'rw_rk': nrm((DEPTH, N_RWKV_HEADS, HEAD_DIM), 0.3),
        'lnx_w': 1.0 + nrm((DEPTH, RWKV_WIDTH), 0.05),
        'lnx_b': nrm((DEPTH, RWKV_WIDTH), 0.02),
        'w_out': nrm((DEPTH, MIX_WIDTH, D), MIX_WIDTH ** -0.5),
        'w_router': nrm((DEPTH, D, N_EXPERTS), D ** -0.5),
        'b_router': nrm((DEPTH, N_EXPERTS), 0.01),
        'w_gate': nrm((DEPTH, N_EXPERTS, D, D_EXPERT), D ** -0.5),
        'b_gate': nrm((DEPTH, N_EXPERTS, D_EXPERT), 0.02),
        'w_up': nrm((DEPTH, N_EXPERTS, D, D_EXPERT), D ** -0.5),
        'b_up': nrm((DEPTH, N_EXPERTS, D_EXPERT), 0.02),
        'w_down': nrm((DEPTH, N_EXPERTS, D_EXPERT, D), D_EXPERT ** -0.5),
        'b_down': nrm((DEPTH, N_EXPERTS, D), 0.02),
    }


def reference(x_prompt, x_sample, cache_k, cache_v, cache_logf, state_wkv, state_shift,
              page_table, c_prompt, c_sample, w_mod, b_mod, g_mix, g_ffn, w_in, q_norm,
              k_norm, b_forget, rw_mu, rw_w0, rw_w2, rw_a0, rw_a2, rw_g2, rw_kk, rw_ka,
              rw_rk, lnx_w, lnx_b, w_out, w_router, b_router, w_gate, b_gate, w_up, b_up,
              w_down, b_down):
    n_prompt = x_prompt.shape[0]
    n_dec, n_pages = page_table.shape
    past_len = n_pages * cache_k.shape[2]
    y_prompt, y_sample = x_prompt, x_sample
    kp_l, vp_l, fp_l, wp_l, sp_l = [], [], [], [], []
    ks_l, vs_l, fs_l, ws_l, ss_l = [], [], [], [], []
    for l in range(DEPTH):
        p = dict(w_mod=w_mod[l], b_mod=b_mod[l], g_mix=g_mix[l], g_ffn=g_ffn[l], w_in=w_in[l],
                 q_norm=q_norm[l], k_norm=k_norm[l], b_forget=b_forget[l], rw_mu=rw_mu[l],
                 rw_w0=rw_w0[l], rw_w2=rw_w2[l], rw_a0=rw_a0[l], rw_a2=rw_a2[l],
                 rw_g2=rw_g2[l], rw_kk=rw_kk[l], rw_ka=rw_ka[l], rw_rk=rw_rk[l],
                 lnx_w=lnx_w[l], lnx_b=lnx_b[l], w_out=w_out[l], w_router=w_router[l],
                 b_router=b_router[l], w_gate=w_gate[l], b_gate=b_gate[l], w_up=w_up[l],
                 b_up=b_up[l], w_down=w_down[l], b_down=b_down[l])
        shift0 = jnp.zeros((n_prompt, RWKV_COLS), x_prompt.dtype)
        wkv0 = jnp.zeros((n_prompt, N_RWKV_HEADS, HEAD_DIM, HEAD_DIM), jnp.float32)
        y_prompt, (kp, vp, fp, wp, sp) = hybrid_layer(y_prompt, c_prompt, p, shift0, wkv0, None)
        k_past = cache_k[l, page_table].reshape(n_dec, past_len, N_FOX_HEADS, HEAD_DIM)
        v_past = cache_v[l, page_table].reshape(n_dec, past_len, N_FOX_HEADS, HEAD_DIM)
        f_past = cache_logf[l, page_table].reshape(n_dec, past_len, N_FOX_HEADS)
        y_sample, (ks, vs, fs, ws, ss) = hybrid_layer(
            y_sample, c_sample, p, state_shift[l], state_wkv[l], (k_past, v_past, f_past))
        kp_l.append(kp); vp_l.append(vp); fp_l.append(fp); wp_l.append(wp); sp_l.append(sp)
        ks_l.append(ks); vs_l.append(vs); fs_l.append(fs); ws_l.append(ws); ss_l.append(ss)
    k_prompt = jnp.stack(kp_l)
    v_prompt = jnp.stack(vp_l)
    logf_prompt = jnp.stack(fp_l)
    wkv_prompt = jnp.stack(wp_l)
    shift_prompt = jnp.stack(sp_l)
    k_sample = jnp.stack(ks_l)
    v_sample = jnp.stack(vs_l)
    logf_sample = jnp.stack(fs_l)
    wkv_sample = jnp.stack(ws_l)
    shift_sample = jnp.stack(ss_l)
    return (y_prompt, y_sample, k_prompt, v_prompt, logf_prompt, wkv_prompt, shift_prompt,
            k_sample, v_sample, logf_sample, wkv_sample, shift_sample)
```

```python
import functools

import jax
import jax.numpy as jnp
from jax import lax
from jax.experimental import pallas as pl
from jax.experimental.pallas import tpu as pltpu

F32 = jnp.float32
BF16 = jnp.bfloat16

D_MODEL = 1024
HEAD_DIM = 64
N_HEADS = 8
WIDTH = N_HEADS * HEAD_DIM
DECAY_LORA = 64
ICLR_LORA = 64
GATE_LORA = 128
FOX_COLS = 3 * WIDTH + N_HEADS
RWKV_COLS = 3 * WIDTH + DECAY_LORA + ICLR_LORA + GATE_LORA
IN_COLS_PADDED = 3 * WIDTH + RWKV_COLS + 128
FZ_OFF = 3 * WIDTH + RWKV_COLS
ATTN_SCALE = HEAD_DIM ** -0.5
N_EXPERTS = 32
TOP_K = 4
SWIGLU_ALPHA = 1.702
SWIGLU_LIMIT = 7.0
RMS_EPS = 1e-6
GN_EPS = 64e-5
N_MOD = 6
PAGE = 128
NEG = -1e30

VMEM_LIMIT = 56 * 1024 * 1024


def _cparams(*sem):
    return pltpu.CompilerParams(dimension_semantics=sem, vmem_limit_bytes=VMEM_LIMIT)


def _dot(a, b):
    return jnp.dot(a, b, preferred_element_type=F32)


def _dot_nt(a, b):
    return lax.dot_general(a, b, (((1,), (1,)), ((), ())), preferred_element_type=F32)


def _split2(x):
    hi = x.astype(BF16)
    lo = (x - hi.astype(F32)).astype(BF16)
    return hi, lo


def _split3(x):
    hi = x.astype(BF16)
    r = x - hi.astype(F32)
    mid = r.astype(BF16)
    lo = (r - mid.astype(F32)).astype(BF16)
    return hi, mid, lo


def _dot_x3(a, b):
    ah, al = _split2(a)
    bh, bl = _split2(b)
    return _dot(ah, bh) + (_dot(ah, bl) + _dot(al, bh))


def _dot_sel2(a, sel):
    hi, lo = _split2(a)
    return _dot(hi, sel) + _dot(lo, sel)


def _dot_sel3(a, sel):
    hi, mid, lo = _split3(a)
    return _dot(hi, sel) + (_dot(mid, sel) + _dot(lo, sel))


def _sigmoid(x):
    return 1.0 / (1.0 + jnp.exp(-x))


def _softplus(x):
    return jnp.maximum(x, 0.0) + jnp.log(1.0 + jnp.exp(-jnp.abs(x)))


def _head_ones(n_heads):
    w = n_heads * HEAD_DIM
    r = lax.broadcasted_iota(jnp.int32, (w, w), 0) // HEAD_DIM
    c = lax.broadcasted_iota(jnp.int32, (w, w), 1) // HEAD_DIM
    return (r == c).astype(BF16)


def _mod_kernel(c_ref, w_ref, b_ref, o_ref):
    c = c_ref[...]
    s = c * _sigmoid(c)
    o_ref[...] = _dot_x3(s, w_ref[...]) + b_ref[...]


def _mod(c, w_mod, b_mod):
    rows, d = c.shape
    n = w_mod.shape[1]
    tn = 768
    return pl.pallas_call(
        _mod_kernel,
        grid=(n // tn,),
        in_specs=[pl.BlockSpec((rows, d), lambda j: (0, 0)),
                  pl.BlockSpec((d, tn), lambda j: (0, j)),
                  pl.BlockSpec((1, tn), lambda j: (0, j))],
        out_specs=pl.BlockSpec((rows, tn), lambda j: (0, j)),
        out_shape=jax.ShapeDtypeStruct((rows, n), F32),
        compiler_params=_cparams("arbitrary"),
    )(c, w_mod, b_mod.reshape(1, n))


def _inproj_kernel(x_ref, sc_ref, sh_ref, g_ref, w_ref, wfz_ref, gq_ref, gk_ref, bf_ref, bfc_ref, hs_ref,
                   qh_ref, kh_ref, vh_ref, k_ref, v_ref, lf_ref, lft_ref, rw_ref):
    x = x_ref[0]
    ms = jnp.mean(x * x, axis=-1, keepdims=True)
    xn = x * lax.rsqrt(ms + RMS_EPS) * g_ref[...]
    h = (xn * (1.0 + sc_ref[0]) + sh_ref[0]).astype(BF16)
    proj = _dot(h, w_ref[...])
    hs = hs_ref[...]

    def head_norm(t, gain):
        m = _dot_sel2(t * t, hs) * (1.0 / HEAD_DIM)
        return t * lax.rsqrt(m + RMS_EPS) * gain

    q = head_norm(proj[:, 0:WIDTH], gq_ref[...])
    k = head_norm(proj[:, WIDTH:2 * WIDTH], gk_ref[...])
    v = proj[:, 2 * WIDTH:3 * WIDTH]
    k_ref[0] = k
    v_ref[0] = v
    qb = q.astype(BF16)
    kb = k.astype(BF16)
    vb = v.astype(BF16)
    for hd in range(N_HEADS):
        sl = slice(hd * HEAD_DIM, (hd + 1) * HEAD_DIM)
        qh_ref[0, hd] = qb[:, sl]
        kh_ref[0, hd] = kb[:, sl]
        vh_ref[0, hd] = vb[:, sl]
    rw_ref[0] = proj[:, 3 * WIDTH:3 * WIDTH + RWKV_COLS]
    fz = proj[:, FZ_OFF:FZ_OFF + N_HEADS] + bf_ref[...]
    lf_ref[0] = -_softplus(-fz)
    fzt = _dot_nt(wfz_ref[...], h) + bfc_ref[...]
    lft_ref[0] = -_softplus(-fzt)


def _inproj(x3, sc, sh, g_mix, w_r, wfz_t, gq, gk, b_forget, tm):
    bx, tx, d = x3.shape
    per_row = sc.shape[1] != 1
    mod_spec = (pl.BlockSpec((1, tm, d), lambda b, t: (b, t, 0)) if per_row
                else pl.BlockSpec((1, 1, d), lambda b, t: (b, 0, 0)))
    const = lambda shape: pl.BlockSpec(shape, lambda b, t: tuple(0 for _ in shape))
    head_spec = pl.BlockSpec((1, N_HEADS, tm, HEAD_DIM), lambda b, t: (b, 0, t, 0))
    row_spec = lambda w: pl.BlockSpec((1, tm, w), lambda b, t: (b, t, 0))
    head_shape = jax.ShapeDtypeStruct((bx, N_HEADS, tx, HEAD_DIM), BF16)
    return pl.pallas_call(
        _inproj_kernel,
        grid=(bx, tx // tm),
        in_specs=[row_spec(d), mod_spec, mod_spec, const((1, d)), const((d, IN_COLS_PADDED)),
                  const((N_HEADS, d)), const((1, WIDTH)), const((1, WIDTH)), const((1, N_HEADS)),
                  const((N_HEADS, 1)), const((WIDTH, WIDTH))],
        out_specs=[head_spec, head_spec, head_spec, row_spec(WIDTH), row_spec(WIDTH), row_spec(N_HEADS),
                   pl.BlockSpec((1, N_HEADS, tm), lambda b, t: (b, 0, t)), row_spec(RWKV_COLS)],
        out_shape=[head_shape, head_shape, head_shape,
                   jax.ShapeDtypeStruct((bx, tx, WIDTH), F32), jax.ShapeDtypeStruct((bx, tx, WIDTH), F32),
                   jax.ShapeDtypeStruct((bx, tx, N_HEADS), F32), jax.ShapeDtypeStruct((bx, N_HEADS, tx), F32),
                   jax.ShapeDtypeStruct((bx, tx, RWKV_COLS), F32)],
        compiler_params=_cparams("parallel", "arbitrary"),
    )(x3, sc, sh, g_mix, w_r, wfz_t, gq, gk, b_forget.reshape(1, N_HEADS), b_forget.reshape(N_HEADS, 1),
      _head_ones(N_HEADS))


def _cumsum_kernel(l_ref, u_ref, o_ref, c_ref):
    tc = l_ref.shape[2]

    @pl.when(pl.program_id(1) == 0)
    def _():
        c_ref[...] = jnp.zeros_like(c_ref)

    f = _dot_sel3(l_ref[0], u_ref[...]) + c_ref[:, 0:1]
    o_ref[0] = f
    c_ref[...] = jnp.broadcast_to(f[:, tc - 1:tc], c_ref.shape)


def _cumsum_t(lft, tc):
    b, h, t = lft.shape
    r = lax.broadcasted_iota(jnp.int32, (tc, tc), 0)
    c = lax.broadcasted_iota(jnp.int32, (tc, tc), 1)
    upper = (r <= c).astype(BF16)
    return pl.pallas_call(
        _cumsum_kernel,
        grid=(b, t // tc),
        in_specs=[pl.BlockSpec((1, h, tc), lambda i, j: (i, 0, j)), pl.BlockSpec((tc, tc), lambda i, j: (0, 0))],
        out_specs=pl.BlockSpec((1, h, tc), lambda i, j: (i, 0, j)),
        out_shape=jax.ShapeDtypeStruct((b, h, t), F32),
        scratch_shapes=[pltpu.VMEM((h, 128), F32)],
        compiler_params=_cparams("parallel", "arbitrary"),
    )(lft, upper)


def _attn_kernel(q_ref, k_ref, v_ref, f_ref, o_ref, m_ref, l_ref, acc_ref, *, tq):
    i = pl.program_id(2)
    q = q_ref[0, 0]
    q0 = pl.multiple_of(i * tq, tq)
    f_base = f_ref[0, 0, :, pl.ds(q0, tq)][:, 0:1]
    m_ref[...] = jnp.full_like(m_ref, NEG)
    l_ref[...] = jnp.zeros_like(l_ref)
    acc_ref[...] = jnp.zeros_like(acc_ref)

    def tile(j, masked):
        k0 = pl.multiple_of(j * tq, tq)
        k = k_ref[0, 0, pl.ds(k0, tq), :]
        v = v_ref[0, 0, pl.ds(k0, tq), :]
        s = _dot_nt(q, k) + (f_base - f_ref[0, 0, :, pl.ds(k0, tq)])
        if masked:
            r = lax.broadcasted_iota(jnp.int32, s.shape, 0)
            c = lax.broadcasted_iota(jnp.int32, s.shape, 1)
            s = jnp.where(c <= r, s, NEG)
        m_old = m_ref[...]
        m_new = jnp.maximum(m_old, jnp.max(s, axis=1, keepdims=True))
        p = jnp.exp(s - m_new)
        alpha = jnp.exp(m_old - m_new)
        l_ref[...] = alpha * l_ref[...] + jnp.sum(p, axis=1, keepdims=True)
        acc_ref[...] = alpha * acc_ref[...] + _dot(p.astype(BF16), v)
        m_ref[...] = m_new

    def body(j, carry):
        tile(j, False)
        return carry

    lax.fori_loop(0, i, body, 0)
    tile(i, True)
    o_ref[0, 0] = (acc_ref[...] / l_ref[...]).astype(o_ref.dtype)


def _fox_prompt(qh, kh, vh, ft, tq):
    b, h, t, dh = qh.shape
    ft4 = ft.reshape(b, h, 1, t)
    seq_spec = pl.BlockSpec((1, 1, t, dh), lambda bi, hi, i: (bi, hi, 0, 0))
    tile_spec = pl.BlockSpec((1, 1, tq, dh), lambda bi, hi, i: (bi, hi, i, 0))
    return pl.pallas_call(
        functools.partial(_attn_kernel, tq=tq),
        grid=(b, h, t // tq),
        in_specs=[tile_spec, seq_spec, seq_spec, pl.BlockSpec((1, 1, 1, t), lambda bi, hi, i: (bi, hi, 0, 0))],
        out_specs=tile_spec,
        out_shape=jax.ShapeDtypeStruct((b, h, t, dh), BF16),
        scratch_shapes=[pltpu.VMEM((tq, 1), F32), pltpu.VMEM((tq, 1), F32), pltpu.VMEM((tq, dh), F32)],
        compiler_params=_cparams("parallel", "parallel", "arbitrary"),
    )(qh, kh, vh, ft4)


PAGES_PER_STEP = 8
QROWS = 4 * N_HEADS


def _decode_kernel(pt_ref, q_ref, kn_ref, vn_ref, lfn_ref, cnew_ref, urep_ref, *refs, n_steps):
    del pt_ref
    npg = PAGES_PER_STEP
    k_refs = refs[0:npg]
    v_refs = refs[npg:2 * npg]
    lf_refs = refs[2 * npg:3 * npg]
    o_ref, m_ref, l_ref, acc_ref, carry_ref = refs[3 * npg:]
    j = pl.program_id(1)
    q = q_ref[0]
    cols = PAGE * N_HEADS

    def update(s, v):
        m_old = m_ref[...]
        m_new = jnp.maximum(m_old, jnp.max(s, axis=1, keepdims=True))
        p = jnp.exp(s - m_new)
        alpha = jnp.exp(m_old - m_new)
        l_ref[...] = alpha * l_ref[...] + jnp.sum(p, axis=1, keepdims=True)
        acc_ref[...] = alpha * acc_ref[...] + _dot(p.astype(BF16), v)
        m_ref[...] = m_new

    @pl.when(j == 0)
    def _():
        m_ref[...] = jnp.full_like(m_ref, NEG)
        l_ref[...] = jnp.zeros_like(l_ref)
        acc_ref[...] = jnp.zeros_like(acc_ref)
        carry_ref[...] = jnp.zeros_like(carry_ref)
        s = _dot_nt(q, kn_ref[0])
        bias = _dot_sel3(lfn_ref[0], cnew_ref[...])[0:1, :]
        r = lax.broadcasted_iota(jnp.int32, s.shape, 0)
        c = lax.broadcasted_iota(jnp.int32, s.shape, 1)
        ok = ((r % N_HEADS) == (c % N_HEADS)) & ((c // N_HEADS) <= (r // N_HEADS))
        update(jnp.where(ok, s + bias, NEG), vn_ref[0])

    r = lax.broadcasted_iota(jnp.int32, (QROWS, cols), 0)
    c = lax.broadcasted_iota(jnp.int32, (QROWS, cols), 1)
    same_head = (r % N_HEADS) == (c % N_HEADS)
    urep = urep_ref[...]
    for u in range(npg):
        kp = k_refs[u][...].reshape(cols, HEAD_DIM).astype(BF16)
        vp = v_refs[u][...].reshape(cols, HEAD_DIM).astype(BF16)
        lft = lf_refs[u][...]
        a = jnp.broadcast_to(lft[None], (4, N_HEADS, PAGE)).reshape(QROWS, PAGE)
        hi, mid, lo = _split3(a)
        bx = _dot(jnp.concatenate([hi, mid, lo], axis=0), urep)
        bias = bx[0:QROWS] + (bx[QROWS:2 * QROWS] + bx[2 * QROWS:3 * QROWS]) + carry_ref[...]
        carry_ref[...] = carry_ref[...] + jnp.sum(a, axis=1, keepdims=True)
        s = _dot_nt(q, kp)
        update(jnp.where(same_head, s + bias, NEG), vp)

    @pl.when(j == n_steps - 1)
    def _():
        o_ref[0] = acc_ref[...] / l_ref[...]


def _fox_decode(page_table, q, k_new, v_new, lf_new, cache_k, cache_v, cache_lft):
    nb, n_pages = page_table.shape
    npg = PAGES_PER_STEP
    n_steps = n_pages // npg
    cols = PAGE * N_HEADS
    urep = (lax.broadcasted_iota(jnp.int32, (PAGE, cols), 0)
            > lax.broadcasted_iota(jnp.int32, (PAGE, cols), 1) // N_HEADS).astype(BF16)
    ri = lax.broadcasted_iota(jnp.int32, (QROWS, QROWS), 0)
    ci = lax.broadcasted_iota(jnp.int32, (QROWS, QROWS), 1)
    cnew = -(((ri % N_HEADS) == (ci % N_HEADS)) & ((ri // N_HEADS) <= (ci // N_HEADS))).astype(BF16)

    def page_idx(u):
        return lambda b, j, pt: (pt[b, n_pages - 1 - (j * npg + u)], 0, 0, 0)

    def page_idx3(u):
        return lambda b, j, pt: (pt[b, n_pages - 1 - (j * npg + u)], 0, 0)

    per_b = lambda shape: pl.BlockSpec((1,) + shape, lambda b, j, pt: (b, 0, 0))
    const = lambda shape: pl.BlockSpec(shape, lambda b, j, pt: (0, 0))
    in_specs = ([per_b((QROWS, HEAD_DIM)), per_b((QROWS, HEAD_DIM)), per_b((QROWS, HEAD_DIM)),
                 per_b((8, QROWS)), const((QROWS, QROWS)), const((PAGE, cols))]
                + [pl.BlockSpec((None, PAGE, N_HEADS, HEAD_DIM), page_idx(u)) for u in range(npg)]
                + [pl.BlockSpec((None, PAGE, N_HEADS, HEAD_DIM), page_idx(u)) for u in range(npg)]
                + [pl.BlockSpec((None, N_HEADS, PAGE), page_idx3(u)) for u in range(npg)])
    grid_spec = pltpu.PrefetchScalarGridSpec(
        num_scalar_prefetch=1,
        grid=(nb, n_steps),
        in_specs=in_specs,
        out_specs=pl.BlockSpec((1, QROWS, HEAD_DIM), lambda b, j, pt: (b, 0, 0)),
        scratch_shapes=[pltpu.VMEM((QROWS, 1), F32), pltpu.VMEM((QROWS, 1), F32),
                        pltpu.VMEM((QROWS, HEAD_DIM), F32), pltpu.VMEM((QROWS, 1), F32)],
    )
    lfn = jnp.broadcast_to(lf_new.reshape(nb, 1, QROWS), (nb, 8, QROWS))
    return pl.pallas_call(
        functools.partial(_decode_kernel, n_steps=n_steps),
        grid_spec=grid_spec,
        out_shape=jax.ShapeDtypeStruct((nb, QROWS, HEAD_DIM), F32),
        compiler_params=_cparams("parallel", "arbitrary"),
    )(page_table, q, k_new, v_new, lfn, cnew, urep,
      *([cache_k] * npg), *([cache_v] * npg), *([cache_lft] * npg))


def _rwkv_prep_kernel(p_ref, prev_ref, mu_ref, w0_ref, w2_ref, a0_ref, a2_ref, g2_ref, kk_ref, ka_ref, rk_ref,
                      hs_ref, r_ref, k_ref, v_ref, w_ref, a_ref, b_ref, g_ref, bonus_ref):
    p = p_ref[0]
    z = p + (prev_ref[0] - p) * mu_ref[...]
    r = z[:, 0:WIDTH]
    k = z[:, WIDTH:2 * WIDTH]
    v = z[:, 2 * WIDTH:3 * WIDTH]
    lora = z[:, 3 * WIDTH:3 * WIDTH + DECAY_LORA + ICLR_LORA]
    gd = z[:, 3 * WIDTH + DECAY_LORA + ICLR_LORA:RWKV_COLS]
    hs = hs_ref[...]
    w = -_softplus(-(w0_ref[...] + _dot_x3(jnp.tanh(lora), w2_ref[...]))) - 0.5
    decay = jnp.exp(-jnp.exp(w))
    a = _sigmoid(a0_ref[...] + _dot_x3(lora, a2_ref[...]))
    g = _dot_x3(_sigmoid(gd), g2_ref[...])
    kk = k * kk_ref[...]
    norm = jnp.sqrt(_dot_sel3(kk * kk, hs))
    kk = kk / jnp.maximum(norm, 1e-12)
    k = k * (1.0 + (a - 1.0) * ka_ref[...])
    bonus = _dot_sel3(r * k * rk_ref[...], hs) * v
    r_ref[0] = r
    k_ref[0] = k
    v_ref[0] = v
    w_ref[0] = decay
    a_ref[0] = -kk
    b_ref[0] = kk * a
    g_ref[0] = g
    bonus_ref[0] = bonus


def _rwkv_prep(p_rw, prev, mu, w0, w2p, a0, a2p, g2, kk, ka, rk, tm):
    bx, tx, _ = p_rw.shape
    row = lambda w: pl.BlockSpec((1, tm, w), lambda b, t: (b, t, 0))
    const = lambda shape: pl.BlockSpec(shape, lambda b, t: tuple(0 for _ in shape))
    out = jax.ShapeDtypeStruct((bx, tx, WIDTH), F32)
    return pl.pallas_call(
        _rwkv_prep_kernel,
        grid=(bx, tx // tm),
        in_specs=[row(RWKV_COLS), row(RWKV_COLS), const((1, RWKV_COLS)), const((1, WIDTH)),
                  const((DECAY_LORA + ICLR_LORA, WIDTH)), const((1, WIDTH)), const((DECAY_LORA + ICLR_LORA, WIDTH)),
                  const((GATE_LORA, WIDTH)), const((1, WIDTH)), const((1, WIDTH)), const((1, WIDTH)),
                  const((WIDTH, WIDTH))],
        out_specs=[row(WIDTH)] * 8,
        out_shape=[out] * 8,
        compiler_params=_cparams("parallel", "arbitrary"),
    )(p_rw, prev, mu, w0, w2p, a0, a2p, g2, kk, ka, rk, _head_ones(N_HEADS))


N_PAIRS = N_HEADS // 2


def _wkv_kernel(r_ref, k_ref, v_ref, w_ref, a_ref, b_ref, s0_ref, j2_ref, y_ref, sT_ref, st_ref, *, tc, nb, n_t):
    ti = pl.program_id(1)

    @pl.when(ti == 0)
    def _():
        for bi in range(nb):
            for p in range(N_PAIRS):
                st_ref[bi * N_PAIRS + p] = jnp.concatenate([s0_ref[bi, 2 * p], s0_ref[bi, 2 * p + 1]], axis=1)

    rows = lax.broadcasted_iota(jnp.int32, (HEAD_DIM, 2 * HEAD_DIM), 0)
    lanes = lax.broadcasted_iota(jnp.int32, (HEAD_DIM, 2 * HEAD_DIM), 1)
    eye2 = (rows == lanes % HEAD_DIM).astype(F32)
    j2 = j2_ref[...]

    grp = min(tc, 8)

    def group(gi, carry):
        t0 = pl.multiple_of(gi * grp, grp)
        for bi in range(nb):
            for p in range(N_PAIRS):
                sl = slice(2 * HEAD_DIM * p, 2 * HEAD_DIM * (p + 1))
                rows_of = lambda ref: ref[bi, pl.ds(t0, grp), sl]
                rr, kk, vv, ww, aa, bb = (rows_of(ref) for ref in (r_ref, k_ref, v_ref, w_ref, a_ref, b_ref))
                idx = bi * N_PAIRS + p
                s = st_ref[idx]
                ys = []
                for u in range(grp):
                    row = lambda x: x[u:u + 1, :]
                    sa = _dot_sel2(s * row(aa), j2)
                    vcol = _dot_sel2(eye2 * row(vv), j2)
                    s = s * row(ww) + sa * row(bb) + vcol * row(kk)
                    yb = _dot_sel2(s * row(rr), j2)
                    ys.append(jnp.sum(yb * eye2, axis=0, keepdims=True))
                st_ref[idx] = s
                y_ref[bi, pl.ds(t0, grp), sl] = jnp.concatenate(ys, axis=0)
        return carry

    lax.fori_loop(0, tc // grp, group, 0)

    @pl.when(ti == n_t - 1)
    def _():
        for bi in range(nb):
            for p in range(N_PAIRS):
                s = st_ref[bi * N_PAIRS + p]
                sT_ref[bi, 2 * p] = s[:, 0:HEAD_DIM]
                sT_ref[bi, 2 * p + 1] = s[:, HEAD_DIM:2 * HEAD_DIM]


def _wkv_scan(r, k, v, w, a, b, s0, tc, nb):
    bx, tx, _ = r.shape
    n_t = tx // tc
    seq = pl.BlockSpec((nb, tc, WIDTH), lambda i, t: (i, t, 0))
    st = pl.BlockSpec((nb, N_HEADS, HEAD_DIM, HEAD_DIM), lambda i, t: (i, 0, 0, 0))
    ri = lax.broadcasted_iota(jnp.int32, (2 * HEAD_DIM, 2 * HEAD_DIM), 0) // HEAD_DIM
    ci = lax.broadcasted_iota(jnp.int32, (2 * HEAD_DIM, 2 * HEAD_DIM), 1) // HEAD_DIM
    j2 = (ri == ci).astype(BF16)
    return pl.pallas_call(
        functools.partial(_wkv_kernel, tc=tc, nb=nb, n_t=n_t),
        grid=(bx // nb, n_t),
        in_specs=[seq] * 6 + [st, pl.BlockSpec((2 * HEAD_DIM, 2 * HEAD_DIM), lambda i, t: (0, 0))],
        out_specs=[seq, st],
        out_shape=[jax.ShapeDtypeStruct((bx, tx, WIDTH), F32),
                   jax.ShapeDtypeStruct((bx, N_HEADS, HEAD_DIM, HEAD_DIM), F32)],
        scratch_shapes=[pltpu.VMEM((nb * N_PAIRS, HEAD_DIM, 2 * HEAD_DIM), F32)],
        compiler_params=_cparams("parallel", "arbitrary"),
    )(r, k, v, w, a, b, s0, j2)


def _outproj_kernel(x_ref, of_ref, y_ref, bonus_ref, g_ref, gt_ref, sc_ref, sh_ref, lw_ref, lb_ref, gf_ref,
                    wo_ref, wr_ref, br_ref, hs_ref, x1_ref, h2_ref, ti_ref, tw_ref):
    hs = hs_ref[...]
    y = y_ref[0]
    mu = _dot_sel3(y, hs) * (1.0 / HEAD_DIM)
    yc = y - mu
    var = _dot_sel3(yc * yc, hs) * (1.0 / HEAD_DIM)
    o_rw = ((yc * lax.rsqrt(var + GN_EPS)) * lw_ref[...] + lb_ref[...] + bonus_ref[0]) * g_ref[0]
    mixed = _dot(o_rw.astype(BF16), wo_ref[WIDTH:2 * WIDTH, :])
    for hd in range(N_HEADS):
        mixed = mixed + _dot(of_ref[0, hd], wo_ref[hd * HEAD_DIM:(hd + 1) * HEAD_DIM, :])
    x1 = x_ref[0] + gt_ref[0] * mixed
    x1_ref[0] = x1
    ms = jnp.mean(x1 * x1, axis=-1, keepdims=True)
    h2 = (x1 * lax.rsqrt(ms + RMS_EPS) * gf_ref[...]) * (1.0 + sc_ref[0]) + sh_ref[0]
    h2_ref[0] = h2.astype(BF16)
    logits = _dot_x3(h2, wr_ref[...]) + br_ref[...]
    lane = lax.broadcasted_iota(jnp.int32, logits.shape, 1).astype(F32)
    vals, idxs = [], []
    for _ in range(TOP_K):
        m = jnp.max(logits, axis=1, keepdims=True)
        idx = jnp.min(jnp.where(logits == m, lane, float(N_EXPERTS)), axis=1, keepdims=True)
        vals.append(m)
        idxs.append(idx)
        logits = jnp.where(lane == idx, -jnp.inf, logits)
    e = [jnp.exp(vv - vals[0]) for vv in vals]
    tot = e[0] + e[1] + e[2] + e[3]
    tw_ref[0] = jnp.concatenate(e, axis=1) / tot
    ti_ref[0] = jnp.concatenate(idxs, axis=1).astype(jnp.int32)


def _outproj(x3, of, y, bonus, g, gt, sc, sh, lnx_w, lnx_b, g_ffn, w_out_b, w_router, b_router, tm):
    bx, tx, d = x3.shape
    per_row = gt.shape[1] != 1
    mod_spec = (pl.BlockSpec((1, tm, d), lambda b, t: (b, t, 0)) if per_row
                else pl.BlockSpec((1, 1, d), lambda b, t: (b, 0, 0)))
    row = lambda w: pl.BlockSpec((1, tm, w), lambda b, t: (b, t, 0))
    const = lambda shape: pl.BlockSpec(shape, lambda b, t: tuple(0 for _ in shape))
    return pl.pallas_call(
        _outproj_kernel,
        grid=(bx, tx // tm),
        in_specs=[row(d), pl.BlockSpec((1, N_HEADS, tm, HEAD_DIM), lambda b, t: (b, 0, t, 0)),
                  row(WIDTH), row(WIDTH), row(WIDTH), mod_spec, mod_spec, mod_spec,
                  const((1, WIDTH)), const((1, WIDTH)), const((1, d)), const((2 * WIDTH, d)),
                  const((d, N_EXPERTS)), const((1, N_EXPERTS)), const((WIDTH, WIDTH))],
        out_specs=[row(d), row(d), row(TOP_K), row(TOP_K)],
        out_shape=[jax.ShapeDtypeStruct((bx, tx, d), F32), jax.ShapeDtypeStruct((bx, tx, d), BF16),
                   jax.ShapeDtypeStruct((bx, tx, TOP_K), jnp.int32), jax.ShapeDtypeStruct((bx, tx, TOP_K), F32)],
        compiler_params=_cparams("parallel", "arbitrary"),
    )(x3, of, y, bonus, g, gt, sc, sh, lnx_w, lnx_b, g_ffn, w_out_b, w_router, b_router, _head_ones(N_HEADS))


MOE_TM = 256


def _moe_kernel(te_ref, tv_ref, x_ref, ws_ref, wg_ref, bg_ref, wu_ref, bu_ref, wd_ref, bd_ref, o_ref):
    del te_ref
    i = pl.program_id(0)

    @pl.when(tv_ref[i] != 0)
    def _():
        x = x_ref[...]
        glu = jnp.minimum(_dot(x, wg_ref[0]) + bg_ref[0], SWIGLU_LIMIT)
        lin = jnp.clip(_dot(x, wu_ref[0]) + bu_ref[0], -SWIGLU_LIMIT, SWIGLU_LIMIT)
        act = glu * _sigmoid(SWIGLU_ALPHA * glu) * (lin + 1.0)
        o_ref[...] = (_dot(act.astype(BF16), wd_ref[0]) + bd_ref[0]) * ws_ref[...]

    @pl.when(tv_ref[i] == 0)
    def _():
        o_ref[...] = jnp.zeros_like(o_ref)


def _moe_gemm(tile_expert, tile_valid, xg, w_slot, wg, bg, wu, bu, wd, bd):
    n_slots, d = xg.shape
    n_tiles = n_slots // MOE_TM
    wspec = pl.BlockSpec((1, d, d), lambda i, te, tv: (te[i], 0, 0))
    bspec = pl.BlockSpec((1, 1, d), lambda i, te, tv: (te[i], 0, 0))
    grid_spec = pltpu.PrefetchScalarGridSpec(
        num_scalar_prefetch=2,
        grid=(n_tiles,),
        in_specs=[pl.BlockSpec((MOE_TM, d), lambda i, te, tv: (i, 0)),
                  pl.BlockSpec((MOE_TM, 1), lambda i, te, tv: (i, 0)),
                  wspec, bspec, wspec, bspec, wspec, bspec],
        out_specs=pl.BlockSpec((MOE_TM, d), lambda i, te, tv: (i, 0)),
    )
    return pl.pallas_call(
        _moe_kernel,
        grid_spec=grid_spec,
        out_shape=jax.ShapeDtypeStruct((n_slots, d), F32),
        compiler_params=_cparams("arbitrary"),
    )(tile_expert, tile_valid, xg, w_slot, wg, bg, wu, bu, wd, bd)


def _final_kernel(x1_ref, yg_ref, gt_ref, o_ref):
    yg = yg_ref[...]
    moe = (yg[0, 0] + yg[1, 0]) + (yg[2, 0] + yg[3, 0])
    o_ref[0] = x1_ref[0] + gt_ref[0] * moe


def _final(x1, yg, gt, tm):
    bx, tx, d = x1.shape
    per_row = gt.shape[1] != 1
    mod_spec = (pl.BlockSpec((1, tm, d), lambda b, t: (b, t, 0)) if per_row
                else pl.BlockSpec((1, 1, d), lambda b, t: (b, 0, 0)))
    row = pl.BlockSpec((1, tm, d), lambda b, t: (b, t, 0))
    return pl.pallas_call(
        _final_kernel,
        grid=(bx, tx // tm),
        in_specs=[row, pl.BlockSpec((TOP_K, 1, tm, d), lambda b, t: (0, b, t, 0)), mod_spec],
        out_specs=row,
        out_shape=jax.ShapeDtypeStruct((bx, tx, d), F32),
        compiler_params=_cparams("parallel", "arbitrary"),
    )(x1, yg, gt)


def _dispatch_plan(top_i, top_w):
    n_assign = top_i.size
    n_slots = n_assign + N_EXPERTS * MOE_TM
    n_tiles = n_slots // MOE_TM
    e_flat = top_i.reshape(-1)
    counts = jnp.zeros((N_EXPERTS,), jnp.int32).at[e_flat].add(1)
    padded = ((counts + MOE_TM - 1) // MOE_TM) * MOE_TM
    pad_end = jnp.cumsum(padded)
    pad_start = pad_end - padded
    cnt_start = jnp.cumsum(counts) - counts
    order = jnp.argsort(e_flat, stable=True).astype(jnp.int32)
    sorted_e = e_flat[order]
    slot_sorted = pad_start[sorted_e] + (jnp.arange(n_assign, dtype=jnp.int32) - cnt_start[sorted_e])
    slot_of_assign = jnp.zeros((n_assign,), jnp.int32).at[order].set(slot_sorted)
    token_of_slot = jnp.zeros((n_slots,), jnp.int32).at[slot_sorted].set(order // TOP_K)
    w_of_slot = jnp.zeros((n_slots,), F32).at[slot_sorted].set(top_w.reshape(-1)[order])
    tile_start = jnp.arange(n_tiles, dtype=jnp.int32) * MOE_TM
    tile_expert = jnp.minimum(jnp.searchsorted(pad_end, tile_start, side="right"), N_EXPERTS - 1).astype(jnp.int32)
    tile_valid = (tile_start < pad_end[-1]).astype(jnp.int32)
    return slot_of_assign, token_of_slot, w_of_slot, tile_expert, tile_valid


def _pick_tile(n, pref):
    t = min(n, pref)
    assert n % t == 0, (n, t)
    return t


def kernel(x_prompt, x_sample, cache_k, cache_v, cache_logf, state_wkv, state_shift, page_table, c_prompt,
           c_sample, w_mod, b_mod, g_mix, g_ffn, w_in, q_norm, k_norm, b_forget, rw_mu, rw_w0, rw_w2, rw_a0,
           rw_a2, rw_g2, rw_kk, rw_ka, rw_rk, lnx_w, lnx_b, w_out, w_router, b_router, w_gate, b_gate, w_up,
           b_up, w_down, b_down):
    depth = w_mod.shape[0]
    assert depth == 1
    n_p, t_p, d = x_prompt.shape
    n_s, t_s, _ = x_sample.shape
    assert d == D_MODEL and t_s * N_HEADS == QROWS
    l = 0

    w_in_l = w_in[l]
    w_r = jnp.concatenate([w_in_l[:, :3 * WIDTH], w_in_l[:, FOX_COLS:], w_in_l[:, 3 * WIDTH:FOX_COLS],
                           jnp.zeros((d, 128 - N_HEADS), F32)], axis=1).astype(BF16)
    wfz_t = w_in_l[:, 3 * WIDTH:FOX_COLS].T.astype(BF16)
    gq = (jnp.tile(q_norm[l], N_HEADS) * ATTN_SCALE).reshape(1, WIDTH)
    gk = jnp.tile(k_norm[l], N_HEADS).reshape(1, WIDTH)
    zpad = jnp.zeros((DECAY_LORA, WIDTH), F32)
    w2p = jnp.concatenate([rw_w2[l], zpad], axis=0)
    a2p = jnp.concatenate([zpad, rw_a2[l]], axis=0)
    row = lambda a: a.reshape(1, -1)

    n_c = n_p + n_s
    pad_c = (-n_c) % 8
    c_all = jnp.concatenate([c_sample, c_prompt, jnp.zeros((pad_c, d), F32)], axis=0)
    mod = _mod(c_all, w_mod[l], b_mod[l])
    mod_s = mod[:n_s].reshape(n_s, 1, N_MOD, d)
    mod_p = mod[n_s:n_s + n_p].reshape(n_p, 1, N_MOD, d)
    mod_s = jnp.broadcast_to(mod_s, (n_s, t_s, N_MOD, d)).reshape(1, n_s * t_s, N_MOD, d)
    mp = [mod_p[:, :, i] for i in range(N_MOD)]
    msm = [mod_s[:, :, i] for i in range(N_MOD)]

    xs3 = x_sample.reshape(1, n_s * t_s, d)
    tm_p = _pick_tile(t_p, 256)
    tm_s = _pick_tile(n_s * t_s, 256)

    proj_args = (row(g_mix[l]), w_r, wfz_t, gq, gk, b_forget[l])
    qh_p, kh_p, vh_p, k_p, v_p, lf_p, lft_p, rw_p = _inproj(x_prompt, mp[1], mp[0], *proj_args, tm_p)
    qh_s, _, _, k_s, v_s, lf_s, _, rw_s = _inproj(xs3, msm[1], msm[0], *proj_args, tm_s)

    ft = _cumsum_t(lft_p, _pick_tile(t_p, 512))
    of_p = _fox_prompt(qh_p, kh_p, vh_p, ft, _pick_tile(t_p, 512))

    q_s = jnp.transpose(qh_s[0], (1, 0, 2)).reshape(n_s, QROWS, HEAD_DIM)
    kn_s = k_s.reshape(n_s, QROWS, HEAD_DIM).astype(BF16)
    vn_s = v_s.reshape(n_s, QROWS, HEAD_DIM).astype(BF16)
    cache_lft = jnp.swapaxes(cache_logf[l], 1, 2)
    o_s = _fox_decode(page_table, q_s, kn_s, vn_s, lf_s.reshape(n_s, QROWS), cache_k[l], cache_v[l], cache_lft)
    of_s = jnp.transpose(o_s.reshape(n_s * t_s, N_HEADS, HEAD_DIM), (1, 0, 2))[None].astype(BF16)

    prep_w = (row(rw_mu[l]), row(rw_w0[l]), w2p, row(rw_a0[l]), a2p, rw_g2[l], row(rw_kk[l]), row(rw_ka[l]),
              row(rw_rk[l]))
    prev_p = jnp.concatenate([jnp.zeros((n_p, 1, RWKV_COLS), F32), rw_p[:, :-1]], axis=1)
    rw_s4 = rw_s.reshape(n_s, t_s, RWKV_COLS)
    prev_s = jnp.concatenate([state_shift[l][:, None, :], rw_s4[:, :-1]], axis=1).reshape(1, n_s * t_s, RWKV_COLS)
    r_p, kk_p, vv_p, w_p, a_p, b_p, g_p, bonus_p = _rwkv_prep(rw_p, prev_p, *prep_w, tm_p)
    r_s, kk_s, vv_s, w_s, a_s, b_s, g_s, bonus_s = _rwkv_prep(rw_s, prev_s, *prep_w, tm_s)
    nb_p = 2 if n_p % 2 == 0 else 1
    y_p, wkv_p = _wkv_scan(r_p, kk_p, vv_p, w_p, a_p, b_p,
                           jnp.zeros((n_p, N_HEADS, HEAD_DIM, HEAD_DIM), F32), _pick_tile(t_p, 256), nb_p)
    to_seq = lambda a: a.reshape(n_s, t_s, WIDTH)
    nb_s = 2 if n_s % 2 == 0 else 1
    y_s, wkv_s = _wkv_scan(to_seq(r_s), to_seq(kk_s), to_seq(vv_s), to_seq(w_s), to_seq(a_s), to_seq(b_s),
                           state_wkv[l], t_s, nb_s)
    y_s = y_s.reshape(1, n_s * t_s, WIDTH)

    out_w = (row(lnx_w[l]), row(lnx_b[l]), row(g_ffn[l]), w_out[l].astype(BF16), w_router[l], row(b_router[l]))
    x1_p, h2_p, ti_p, tw_p = _outproj(x_prompt, of_p, y_p, bonus_p, g_p, mp[2], mp[4], mp[3], *out_w, tm_p)
    x1_s, h2_s, ti_s, tw_s = _outproj(xs3, of_s, y_s, bonus_s, g_s, msm[2], msm[4], msm[3], *out_w, tm_s)

    n_tok_p = n_p * t_p
    h2 = jnp.concatenate([h2_p.reshape(n_tok_p, d), h2_s.reshape(n_s * t_s, d)], axis=0)
    top_i = jnp.concatenate([ti_p.reshape(n_tok_p, TOP_K), ti_s.reshape(n_s * t_s, TOP_K)], axis=0)
    top_w = jnp.concatenate([tw_p.reshape(n_tok_p, TOP_K), tw_s.reshape(n_s * t_s, TOP_K)], axis=0)
    slot_of_assign, token_of_slot, w_of_slot, tile_expert, tile_valid = _dispatch_plan(top_i, top_w)
    xg = jnp.take(h2, token_of_slot, axis=0)
    ys = _moe_gemm(tile_expert, tile_valid, xg, w_of_slot.reshape(-1, 1),
                   w_gate[l].astype(BF16), b_gate[l].reshape(N_EXPERTS, 1, d),
                   w_up[l].astype(BF16), b_up[l].reshape(N_EXPERTS, 1, d),
                   w_down[l].astype(BF16), b_down[l].reshape(N_EXPERTS, 1, d))
    slots = slot_of_assign.reshape(-1, TOP_K).T
    yg = jnp.take(ys, slots, axis=0)
    y_prompt = _final(x1_p, yg[:, :n_tok_p].reshape(TOP_K, n_p, t_p, d), mp[5], tm_p)
    y_sample = _final(x1_s, yg[:, n_tok_p:].reshape(TOP_K, 1, n_s * t_s, d), msm[5], tm_s)

    heads5 = lambda a, nb, nt: a.reshape(1, nb, nt, N_HEADS, HEAD_DIM)
    return (y_prompt, y_sample.reshape(n_s, t_s, d),
            heads5(k_p, n_p, t_p), heads5(v_p, n_p, t_p), lf_p[None], wkv_p[None], rw_p[:, -1][None],
            heads5(k_s, n_s, t_s), heads5(v_s, n_s, t_s), lf_s.reshape(1, n_s, t_s, N_HEADS), wkv_s[None],
            rw_s4[:, -1][None])
```

```python
import functools

import jax
import jax.numpy as jnp
from jax import lax
from jax.experimental import pallas as pl
from jax.experimental.pallas import tpu as pltpu

F32 = jnp.float32
BF16 = jnp.bfloat16

D_MODEL = 1024
HEAD_DIM = 64
N_HEADS = 8
WIDTH = N_HEADS * HEAD_DIM
PAIR = 2 * HEAD_DIM
N_HEAD_PAIRS = N_HEADS // 2
DECAY_LORA = 64
ICLR_LORA = 64
GATE_LORA = 128
FOX_COLS = 3 * WIDTH + N_HEADS
RWKV_COLS = 3 * WIDTH + DECAY_LORA + ICLR_LORA + GATE_LORA
IN_COLS_PADDED = 3 * WIDTH + RWKV_COLS + 128
FZ_OFF = 3 * WIDTH + RWKV_COLS
ATTN_SCALE = HEAD_DIM ** -0.5
N_EXPERTS = 32
TOP_K = 4
SWIGLU_ALPHA = 1.702
SWIGLU_LIMIT = 7.0
RMS_EPS = 1e-6
GN_EPS = 64e-5
N_MOD = 6
PAGE = 128
NEG = -1e30

VMEM_LIMIT = 56 * 1024 * 1024


def _cparams(*sem):
    return pltpu.CompilerParams(dimension_semantics=sem, vmem_limit_bytes=VMEM_LIMIT)


def _dot(a, b):
    return jnp.dot(a, b, preferred_element_type=F32)


def _dot_nt(a, b):
    return lax.dot_general(a, b, (((1,), (1,)), ((), ())), preferred_element_type=F32)


def _split2(x):
    hi = x.astype(BF16)
    lo = (x - hi.astype(F32)).astype(BF16)
    return hi, lo


def _split3(x):
    hi = x.astype(BF16)
    r = x - hi.astype(F32)
    mid = r.astype(BF16)
    lo = (r - mid.astype(F32)).astype(BF16)
    return hi, mid, lo


def _dot_x3(a, b):
    ah, al = _split2(a)
    bh, bl = _split2(b)
    return _dot(ah, bh) + (_dot(ah, bl) + _dot(al, bh))


def _dot_sel2(a, sel):
    hi, lo = _split2(a)
    return _dot(hi, sel) + _dot(lo, sel)


def _dot_sel3(a, sel):
    hi, mid, lo = _split3(a)
    return _dot(hi, sel) + (_dot(mid, sel) + _dot(lo, sel))


def _sigmoid(x):
    return 1.0 / (1.0 + jnp.exp(-x))


def _softplus(x):
    return jnp.maximum(x, 0.0) + jnp.log(1.0 + jnp.exp(-jnp.abs(x)))


def _head_ones(n_heads):
    w = n_heads * HEAD_DIM
    r = lax.broadcasted_iota(jnp.int32, (w, w), 0) // HEAD_DIM
    c = lax.broadcasted_iota(jnp.int32, (w, w), 1) // HEAD_DIM
    return (r == c).astype(BF16)


def _mod_kernel(c_ref, w_ref, b_ref, o_ref):
    c = c_ref[...]
    s = c * _sigmoid(c)
    o_ref[...] = _dot_x3(s, w_ref[...]) + b_ref[...]


def _mod(c, w_mod, b_mod):
    rows, d = c.shape
    n = w_mod.shape[1]
    tn = 768
    return pl.pallas_call(
        _mod_kernel,
        grid=(n // tn,),
        in_specs=[pl.BlockSpec((rows, d), lambda j: (0, 0)),
                  pl.BlockSpec((d, tn), lambda j: (0, j)),
                  pl.BlockSpec((1, tn), lambda j: (0, j))],
        out_specs=pl.BlockSpec((rows, tn), lambda j: (0, j)),
        out_shape=jax.ShapeDtypeStruct((rows, n), F32),
        compiler_params=_cparams("arbitrary"),
    )(c, w_mod, b_mod.reshape(1, n))


def _inproj_kernel(x_ref, sc_ref, sh_ref, g_ref, w_ref, wfz_ref, gq_ref, gk_ref, bf_ref, bfc_ref, hs_ref,
                   qh_ref, kh_ref, vh_ref, k_ref, v_ref, lf_ref, lft_ref, rw_ref):
    x = x_ref[0]
    ms = jnp.mean(x * x, axis=-1, keepdims=True)
    xn = x * lax.rsqrt(ms + RMS_EPS) * g_ref[...]
    h = (xn * (1.0 + sc_ref[0]) + sh_ref[0]).astype(BF16)
    proj = _dot(h, w_ref[...])
    hs = hs_ref[...]

    def head_norm(t, gain):
        m = _dot_sel2(t * t, hs) * (1.0 / HEAD_DIM)
        return t * lax.rsqrt(m + RMS_EPS) * gain

    q = head_norm(proj[:, 0:WIDTH], gq_ref[...])
    k = head_norm(proj[:, WIDTH:2 * WIDTH], gk_ref[...])
    v = proj[:, 2 * WIDTH:3 * WIDTH]
    k_ref[0] = k
    v_ref[0] = v
    qb = q.astype(BF16)
    kb = k.astype(BF16)
    vb = v.astype(BF16)
    for hp in range(N_HEAD_PAIRS):
        sl = slice(hp * PAIR, (hp + 1) * PAIR)
        qh_ref[0, hp] = qb[:, sl]
        kh_ref[0, hp] = kb[:, sl]
        vh_ref[0, hp] = vb[:, sl]
    rw_ref[0] = proj[:, 3 * WIDTH:3 * WIDTH + RWKV_COLS]
    fz = proj[:, FZ_OFF:FZ_OFF + N_HEADS] + bf_ref[...]
    lf_ref[0] = -_softplus(-fz)
    fzt = _dot_nt(wfz_ref[...], h) + bfc_ref[...]
    lft_ref[0] = -_softplus(-fzt)


def _inproj(x3, sc, sh, g_mix, w_r, wfz_t, gq, gk, b_forget, tm):
    bx, tx, d = x3.shape
    per_row = sc.shape[1] != 1
    mod_spec = (pl.BlockSpec((1, tm, d), lambda b, t: (b, t, 0)) if per_row
                else pl.BlockSpec((1, 1, d), lambda b, t: (b, 0, 0)))
    const = lambda shape: pl.BlockSpec(shape, lambda b, t: tuple(0 for _ in shape))
    head_spec = pl.BlockSpec((1, N_HEAD_PAIRS, tm, PAIR), lambda b, t: (b, 0, t, 0))
    row_spec = lambda w: pl.BlockSpec((1, tm, w), lambda b, t: (b, t, 0))
    head_shape = jax.ShapeDtypeStruct((bx, N_HEAD_PAIRS, tx, PAIR), BF16)
    return pl.pallas_call(
        _inproj_kernel,
        grid=(bx, tx // tm),
        in_specs=[row_spec(d), mod_spec, mod_spec, const((1, d)), const((d, IN_COLS_PADDED)),
                  const((N_HEADS, d)), const((1, WIDTH)), const((1, WIDTH)), const((1, N_HEADS)),
                  const((N_HEADS, 1)), const((WIDTH, WIDTH))],
        out_specs=[head_spec, head_spec, head_spec, row_spec(WIDTH), row_spec(WIDTH), row_spec(N_HEADS),
                   pl.BlockSpec((1, N_HEADS, tm), lambda b, t: (b, 0, t)), row_spec(RWKV_COLS)],
        out_shape=[head_shape, head_shape, head_shape,
                   jax.ShapeDtypeStruct((bx, tx, WIDTH), F32), jax.ShapeDtypeStruct((bx, tx, WIDTH), F32),
                   jax.ShapeDtypeStruct((bx, tx, N_HEADS), F32), jax.ShapeDtypeStruct((bx, N_HEADS, tx), F32),
                   jax.ShapeDtypeStruct((bx, tx, RWKV_COLS), F32)],
        compiler_params=_cparams("parallel", "arbitrary"),
    )(x3, sc, sh, g_mix, w_r, wfz_t, gq, gk, b_forget.reshape(1, N_HEADS), b_forget.reshape(N_HEADS, 1),
      _head_ones(N_HEADS))


def _cumsum_kernel(l_ref, u_ref, o_ref, c_ref):
    tc = l_ref.shape[2]

    @pl.when(pl.program_id(1) == 0)
    def _():
        c_ref[...] = jnp.zeros_like(c_ref)

    f = _dot_sel3(l_ref[0], u_ref[...]) + c_ref[:, 0:1]
    o_ref[0] = f
    c_ref[...] = jnp.broadcast_to(f[:, tc - 1:tc], c_ref.shape)


def _cumsum_t(lft, tc):
    b, h, t = lft.shape
    r = lax.broadcasted_iota(jnp.int32, (tc, tc), 0)
    c = lax.broadcasted_iota(jnp.int32, (tc, tc), 1)
    upper = (r <= c).astype(BF16)
    return pl.pallas_call(
        _cumsum_kernel,
        grid=(b, t // tc),
        in_specs=[pl.BlockSpec((1, h, tc), lambda i, j: (i, 0, j)), pl.BlockSpec((tc, tc), lambda i, j: (0, 0))],
        out_specs=pl.BlockSpec((1, h, tc), lambda i, j: (i, 0, j)),
        out_shape=jax.ShapeDtypeStruct((b, h, t), F32),
        scratch_shapes=[pltpu.VMEM((h, 128), F32)],
        compiler_params=_cparams("parallel", "arbitrary"),
    )(lft, upper)


def _attn_kernel(q_ref, k_ref, v_ref, f_ref, o_ref, m_ref, l_ref, acc_ref, *, tq):
    i = pl.program_id(2)
    q2 = q_ref[0, 0]
    lane = lax.broadcasted_iota(jnp.int32, q2.shape, 1)
    zero = jnp.zeros_like(q2)
    qs = (jnp.where(lane < HEAD_DIM, q2, zero), jnp.where(lane >= HEAD_DIM, q2, zero))
    q0 = pl.multiple_of(i * tq, tq)
    f_base = [f_ref[0, 0, hh:hh + 1, pl.ds(q0, tq)][:, 0:1] for hh in range(2)]
    m_ref[...] = jnp.full_like(m_ref, NEG)
    l_ref[...] = jnp.zeros_like(l_ref)
    acc_ref[...] = jnp.zeros_like(acc_ref)

    def tile(j, masked):
        k0 = pl.multiple_of(j * tq, tq)
        k2 = k_ref[0, 0, pl.ds(k0, tq), :]
        v2 = v_ref[0, 0, pl.ds(k0, tq), :]
        for hh in range(2):
            s = _dot_nt(qs[hh], k2) + (f_base[hh] - f_ref[0, 0, hh:hh + 1, pl.ds(k0, tq)])
            if masked:
                r = lax.broadcasted_iota(jnp.int32, s.shape, 0)
                c = lax.broadcasted_iota(jnp.int32, s.shape, 1)
                s = jnp.where(c <= r, s, NEG)
            m_old = m_ref[hh]
            m_new = jnp.maximum(m_old, jnp.max(s, axis=1, keepdims=True))
            p = jnp.exp(s - m_new)
            alpha = jnp.exp(m_old - m_new)
            l_ref[hh] = alpha * l_ref[hh] + jnp.sum(p, axis=1, keepdims=True)
            acc_ref[hh] = alpha * acc_ref[hh] + _dot(p.astype(BF16), v2)
            m_ref[hh] = m_new

    def body(j, carry):
        tile(j, False)
        return carry

    lax.fori_loop(0, i, body, 0)
    tile(i, True)
    o_a = acc_ref[0] / l_ref[0]
    o_b = acc_ref[1] / l_ref[1]
    o_ref[0, 0] = jnp.where(lane < HEAD_DIM, o_a, o_b).astype(o_ref.dtype)


def _fox_prompt(qh, kh, vh, ft, tq):
    b, hp, t, w = qh.shape
    ft4 = ft.reshape(b, hp, 2, t)
    seq_spec = pl.BlockSpec((1, 1, t, w), lambda bi, hi, i: (bi, hi, 0, 0))
    tile_spec = pl.BlockSpec((1, 1, tq, w), lambda bi, hi, i: (bi, hi, i, 0))
    return pl.pallas_call(
        functools.partial(_attn_kernel, tq=tq),
        grid=(b, hp, t // tq),
        in_specs=[tile_spec, seq_spec, seq_spec, pl.BlockSpec((1, 1, 2, t), lambda bi, hi, i: (bi, hi, 0, 0))],
        out_specs=tile_spec,
        out_shape=jax.ShapeDtypeStruct((b, hp, t, w), BF16),
        scratch_shapes=[pltpu.VMEM((2, tq, 1), F32), pltpu.VMEM((2, tq, 1), F32), pltpu.VMEM((2, tq, w), F32)],
        compiler_params=_cparams("parallel", "parallel", "arbitrary"),
    )(qh, kh, vh, ft4)


PAGES_PER_STEP = 8
QROWS = 4 * N_HEADS


def _decode_kernel(pt_ref, q_ref, kn_ref, vn_ref, lfn_ref, cnew_ref, ustrict_ref, fold_ref, *refs, n_steps):
    del pt_ref
    npg = PAGES_PER_STEP
    k_refs = refs[0:npg]
    v_refs = refs[npg:2 * npg]
    lf_refs = refs[2 * npg:3 * npg]
    o_ref, m_ref, l_ref, acc_ref, accn_ref, carry_ref = refs[3 * npg:]
    j = pl.program_id(1)
    q = q_ref[0]
    r = lax.broadcasted_iota(jnp.int32, (QROWS, WIDTH), 0)
    c = lax.broadcasted_iota(jnp.int32, (QROWS, WIDTH), 1)
    own_head = (r % N_HEADS) == (c // HEAD_DIM)
    q_bd = jnp.where(own_head, jnp.concatenate([q] * N_HEADS, axis=1), jnp.zeros((), q.dtype))

    def softmax_step(s):
        m_old = m_ref[...]
        m_new = jnp.maximum(m_old, jnp.max(s, axis=1, keepdims=True))
        p = jnp.exp(s - m_new)
        alpha = jnp.exp(m_old - m_new)
        l_ref[...] = alpha * l_ref[...] + jnp.sum(p, axis=1, keepdims=True)
        m_ref[...] = m_new
        return p.astype(BF16), alpha

    @pl.when(j == 0)
    def _():
        m_ref[...] = jnp.full_like(m_ref, NEG)
        l_ref[...] = jnp.zeros_like(l_ref)
        acc_ref[...] = jnp.zeros_like(acc_ref)
        carry_ref[...] = jnp.zeros_like(carry_ref)
        s = _dot_nt(q, kn_ref[0])
        bias = _dot_sel3(lfn_ref[0], cnew_ref[...])[0:1, :]
        rr = lax.broadcasted_iota(jnp.int32, s.shape, 0)
        cc = lax.broadcasted_iota(jnp.int32, s.shape, 1)
        ok = ((rr % N_HEADS) == (cc % N_HEADS)) & ((cc // N_HEADS) <= (rr // N_HEADS))
        p, _ = softmax_step(jnp.where(ok, s + bias, NEG))
        accn_ref[...] = _dot(p, vn_ref[0])

    ustrict = ustrict_ref[...]
    carry = carry_ref[...]
    scores = []
    for u in range(npg):
        kt = k_refs[u][...].reshape(WIDTH, PAGE).astype(BF16)
        lft = lf_refs[u][...]
        a = jnp.broadcast_to(lft[None], (4, N_HEADS, PAGE)).reshape(QROWS, PAGE)
        hi, mid, lo = _split3(a)
        bx = _dot(jnp.concatenate([hi, mid, lo], axis=0), ustrict)
        bias = bx[0:QROWS] + (bx[QROWS:2 * QROWS] + bx[2 * QROWS:3 * QROWS]) + carry
        carry = carry + jnp.sum(a, axis=1, keepdims=True)
        scores.append(_dot(q_bd, kt) + bias)
    carry_ref[...] = carry
    p, alpha = softmax_step(jnp.concatenate(scores, axis=1))
    acc = alpha * acc_ref[...]
    for u in range(npg):
        vt = v_refs[u][...].reshape(WIDTH, PAGE).astype(BF16)
        acc = acc + _dot_nt(p[:, u * PAGE:(u + 1) * PAGE], vt)
    acc_ref[...] = acc
    accn_ref[...] = alpha * accn_ref[...]

    @pl.when(j == n_steps - 1)
    def _():
        own = jnp.where(own_head, acc_ref[...], 0.0)
        o_ref[0] = (_dot_sel3(own, fold_ref[...]) + accn_ref[...]) / l_ref[...]


def _fox_decode(page_table, q, k_new, v_new, lf_new, cache_kt, cache_vt, cache_lft):
    nb, n_pages = page_table.shape
    npg = PAGES_PER_STEP
    n_steps = n_pages // npg
    ustrict = (lax.broadcasted_iota(jnp.int32, (PAGE, PAGE), 0)
               > lax.broadcasted_iota(jnp.int32, (PAGE, PAGE), 1)).astype(BF16)
    ri = lax.broadcasted_iota(jnp.int32, (QROWS, QROWS), 0)
    ci = lax.broadcasted_iota(jnp.int32, (QROWS, QROWS), 1)
    cnew = -(((ri % N_HEADS) == (ci % N_HEADS)) & ((ri // N_HEADS) <= (ci // N_HEADS))).astype(BF16)
    fold = (lax.broadcasted_iota(jnp.int32, (WIDTH, HEAD_DIM), 0) % HEAD_DIM
            == lax.broadcasted_iota(jnp.int32, (WIDTH, HEAD_DIM), 1)).astype(BF16)

    def page_idx(u):
        return lambda b, j, pt: (pt[b, n_pages - 1 - (j * npg + u)], 0, 0, 0)

    def page_idx3(u):
        return lambda b, j, pt: (pt[b, n_pages - 1 - (j * npg + u)], 0, 0)

    per_b = lambda shape: pl.BlockSpec((1,) + shape, lambda b, j, pt: (b, 0, 0))
    const = lambda shape: pl.BlockSpec(shape, lambda b, j, pt: (0, 0))
    in_specs = ([per_b((QROWS, HEAD_DIM)), per_b((QROWS, HEAD_DIM)), per_b((QROWS, HEAD_DIM)),
                 per_b((8, QROWS)), const((QROWS, QROWS)), const((PAGE, PAGE)), const((WIDTH, HEAD_DIM))]
                + [pl.BlockSpec((None, N_HEADS, HEAD_DIM, PAGE), page_idx(u)) for u in range(npg)]
                + [pl.BlockSpec((None, N_HEADS, HEAD_DIM, PAGE), page_idx(u)) for u in range(npg)]
                + [pl.BlockSpec((None, N_HEADS, PAGE), page_idx3(u)) for u in range(npg)])
    grid_spec = pltpu.PrefetchScalarGridSpec(
        num_scalar_prefetch=1,
        grid=(nb, n_steps),
        in_specs=in_specs,
        out_specs=pl.BlockSpec((1, QROWS, HEAD_DIM), lambda b, j, pt: (b, 0, 0)),
        scratch_shapes=[pltpu.VMEM((QROWS, 1), F32), pltpu.VMEM((QROWS, 1), F32),
                        pltpu.VMEM((QROWS, WIDTH), F32), pltpu.VMEM((QROWS, HEAD_DIM), F32),
                        pltpu.VMEM((QROWS, 1), F32)],
    )
    lfn = jnp.broadcast_to(lf_new.reshape(nb, 1, QROWS), (nb, 8, QROWS))
    return pl.pallas_call(
        functools.partial(_decode_kernel, n_steps=n_steps),
        grid_spec=grid_spec,
        out_shape=jax.ShapeDtypeStruct((nb, QROWS, HEAD_DIM), F32),
        compiler_params=_cparams("parallel", "arbitrary"),
    )(page_table, q, k_new, v_new, lfn, cnew, ustrict, fold,
      *([cache_kt] * npg), *([cache_vt] * npg), *([cache_lft] * npg))


def _rwkv_prep_kernel(p_ref, prev_ref, mu_ref, w0_ref, w2_ref, a0_ref, a2_ref, g2_ref, kk_ref, ka_ref, rk_ref,
                      hs_ref, r_ref, k_ref, v_ref, w_ref, a_ref, b_ref, g_ref, bonus_ref):
    p = p_ref[0]
    z = p + (prev_ref[0] - p) * mu_ref[...]
    r = z[:, 0:WIDTH]
    k = z[:, WIDTH:2 * WIDTH]
    v = z[:, 2 * WIDTH:3 * WIDTH]
    lora = z[:, 3 * WIDTH:3 * WIDTH + DECAY_LORA + ICLR_LORA]
    gd = z[:, 3 * WIDTH + DECAY_LORA + ICLR_LORA:RWKV_COLS]
    hs = hs_ref[...]
    w = -_softplus(-(w0_ref[...] + _dot_x3(jnp.tanh(lora), w2_ref[...]))) - 0.5
    decay = jnp.exp(-jnp.exp(w))
    a = _sigmoid(a0_ref[...] + _dot_x3(lora, a2_ref[...]))
    g = _dot_x3(_sigmoid(gd), g2_ref[...])
    kk = k * kk_ref[...]
    norm = jnp.sqrt(_dot_sel3(kk * kk, hs))
    kk = kk / jnp.maximum(norm, 1e-12)
    k = k * (1.0 + (a - 1.0) * ka_ref[...])
    bonus = _dot_sel3(r * k * rk_ref[...], hs) * v
    r_ref[0] = r
    k_ref[0] = k
    v_ref[0] = v
    w_ref[0] = decay
    a_ref[0] = -kk
    b_ref[0] = kk * a
    g_ref[0] = g
    bonus_ref[0] = bonus


def _rwkv_prep(p_rw, prev, mu, w0, w2p, a0, a2p, g2, kk, ka, rk, tm):
    bx, tx, _ = p_rw.shape
    row = lambda w: pl.BlockSpec((1, tm, w), lambda b, t: (b, t, 0))
    const = lambda shape: pl.BlockSpec(shape, lambda b, t: tuple(0 for _ in shape))
    out = jax.ShapeDtypeStruct((bx, tx, WIDTH), F32)
    return pl.pallas_call(
        _rwkv_prep_kernel,
        grid=(bx, tx // tm),
        in_specs=[row(RWKV_COLS), row(RWKV_COLS), const((1, RWKV_COLS)), const((1, WIDTH)),
                  const((DECAY_LORA + ICLR_LORA, WIDTH)), const((1, WIDTH)), const((DECAY_LORA + ICLR_LORA, WIDTH)),
                  const((GATE_LORA, WIDTH)), const((1, WIDTH)), const((1, WIDTH)), const((1, WIDTH)),
                  const((WIDTH, WIDTH))],
        out_specs=[row(WIDTH)] * 8,
        out_shape=[out] * 8,
        compiler_params=_cparams("parallel", "arbitrary"),
    )(p_rw, prev, mu, w0, w2p, a0, a2p, g2, kk, ka, rk, _head_ones(N_HEADS))


WKV_NB = 2
WKV_HG = 4
WKV_LANES = WKV_HG * HEAD_DIM
WKV_GROUPS = N_HEADS // WKV_HG
WKV_ROWS = WKV_NB * WKV_GROUPS * HEAD_DIM


def _wkv_kernel(r_ref, k_ref, v_ref, w_ref, a_ref, b_ref, s0_ref, ones_ref, y_ref, sT_ref, st_ref, *, tc, n_t):
    ti = pl.program_id(1)
    blocks = [(bi, gg) for bi in range(WKV_NB) for gg in range(WKV_GROUPS)]

    @pl.when(ti == 0)
    def _():
        for n, (bi, gg) in enumerate(blocks):
            st_ref[n * HEAD_DIM:(n + 1) * HEAD_DIM, :] = jnp.concatenate(
                [s0_ref[bi, gg * WKV_HG + q] for q in range(WKV_HG)], axis=1)

    rows = lax.broadcasted_iota(jnp.int32, (WKV_ROWS, WKV_LANES), 0)
    lanes = lax.broadcasted_iota(jnp.int32, (WKV_ROWS, WKV_LANES), 1)
    eye = (rows % HEAD_DIM) == (lanes % HEAD_DIM)
    eye_f = eye.astype(F32)
    ones = ones_ref[...]
    grp = min(tc, 8)

    def bcast(xs, u):
        return jnp.concatenate(
            [jnp.broadcast_to(xs[bi][u:u + 1, gg * WKV_LANES:(gg + 1) * WKV_LANES], (HEAD_DIM, WKV_LANES))
             for bi, gg in blocks], axis=0)

    def group(gi, carry):
        t0 = pl.multiple_of(gi * grp, grp)
        load = lambda ref: [ref[bi, pl.ds(t0, grp), :] for bi in range(WKV_NB)]
        r8, k8, v8, w8, a8, b8 = (load(ref) for ref in (r_ref, k_ref, v_ref, w_ref, a_ref, b_ref))
        s = st_ref[...]
        ys = []
        for u in range(grp):
            hi, lo = _split2(s * bcast(a8, u))
            vdiag = jnp.where(eye, bcast(v8, u), 0.0).astype(BF16)
            red = _dot(jnp.concatenate([hi, lo, vdiag], axis=0), ones)
            sa = red[0:WKV_ROWS] + red[WKV_ROWS:2 * WKV_ROWS]
            vcol = red[2 * WKV_ROWS:3 * WKV_ROWS]
            s = s * bcast(w8, u) + sa * bcast(b8, u) + vcol * bcast(k8, u)
            yb = _dot((s * bcast(r8, u)).astype(BF16), ones) * eye_f
            ys.append([jnp.sum(yb[n * HEAD_DIM:(n + 1) * HEAD_DIM], axis=0, keepdims=True)
                       for n in range(len(blocks))])
        st_ref[...] = s
        for bi in range(WKV_NB):
            y_ref[bi, pl.ds(t0, grp), :] = jnp.concatenate(
                [jnp.concatenate([ys[u][bi * WKV_GROUPS + gg] for gg in range(WKV_GROUPS)], axis=1)
                 for u in range(grp)], axis=0)
        return carry

    lax.fori_loop(0, tc // grp, group, 0)

    @pl.when(ti == n_t - 1)
    def _():
        for n, (bi, gg) in enumerate(blocks):
            for q in range(WKV_HG):
                sT_ref[bi, gg * WKV_HG + q] = st_ref[n * HEAD_DIM:(n + 1) * HEAD_DIM, q * HEAD_DIM:(q + 1) * HEAD_DIM]


def _wkv_scan(r, k, v, w, a, b, s0, tc):
    bx, tx, _ = r.shape
    assert bx % WKV_NB == 0
    n_t = tx // tc
    seq = pl.BlockSpec((WKV_NB, tc, WIDTH), lambda i, t: (i, t, 0))
    st = pl.BlockSpec((WKV_NB, N_HEADS, HEAD_DIM, HEAD_DIM), lambda i, t: (i, 0, 0, 0))
    ri = lax.broadcasted_iota(jnp.int32, (WKV_LANES, WKV_LANES), 0) // HEAD_DIM
    ci = lax.broadcasted_iota(jnp.int32, (WKV_LANES, WKV_LANES), 1) // HEAD_DIM
    ones = (ri == ci).astype(BF16)
    return pl.pallas_call(
        functools.partial(_wkv_kernel, tc=tc, n_t=n_t),
        grid=(bx // WKV_NB, n_t),
        in_specs=[seq] * 6 + [st, pl.BlockSpec((WKV_LANES, WKV_LANES), lambda i, t: (0, 0))],
        out_specs=[seq, st],
        out_shape=[jax.ShapeDtypeStruct((bx, tx, WIDTH), F32),
                   jax.ShapeDtypeStruct((bx, N_HEADS, HEAD_DIM, HEAD_DIM), F32)],
        scratch_shapes=[pltpu.VMEM((WKV_ROWS, WKV_LANES), F32)],
        compiler_params=_cparams("parallel", "arbitrary"),
    )(r, k, v, w, a, b, s0, ones)


def _outproj_kernel(x_ref, of_ref, y_ref, bonus_ref, g_ref, gt_ref, sc_ref, sh_ref, lw_ref, lb_ref, gf_ref,
                    wo_ref, wr_ref, br_ref, hs_ref, x1_ref, h2_ref, ti_ref, tw_ref):
    hs = hs_ref[...]
    y = y_ref[0]
    mu = _dot_sel3(y, hs) * (1.0 / HEAD_DIM)
    yc = y - mu
    var = _dot_sel3(yc * yc, hs) * (1.0 / HEAD_DIM)
    o_rw = ((yc * lax.rsqrt(var + GN_EPS)) * lw_ref[...] + lb_ref[...] + bonus_ref[0]) * g_ref[0]
    mixed = _dot(o_rw.astype(BF16), wo_ref[WIDTH:2 * WIDTH, :])
    for hp in range(N_HEAD_PAIRS):
        mixed = mixed + _dot(of_ref[0, hp], wo_ref[hp * PAIR:(hp + 1) * PAIR, :])
    x1 = x_ref[0] + gt_ref[0] * mixed
    x1_ref[0] = x1
    ms = jnp.mean(x1 * x1, axis=-1, keepdims=True)
    h2 = (x1 * lax.rsqrt(ms + RMS_EPS) * gf_ref[...]) * (1.0 + sc_ref[0]) + sh_ref[0]
    h2_ref[0] = h2.astype(BF16)
    logits = _dot_x3(h2, wr_ref[...]) + br_ref[...]
    lane = lax.broadcasted_iota(jnp.int32, logits.shape, 1).astype(F32)
    vals, idxs = [], []
    for _ in range(TOP_K):
        m = jnp.max(logits, axis=1, keepdims=True)
        idx = jnp.min(jnp.where(logits == m, lane, float(N_EXPERTS)), axis=1, keepdims=True)
        vals.append(m)
        idxs.append(idx)
        logits = jnp.where(lane == idx, -jnp.inf, logits)
    e = [jnp.exp(vv - vals[0]) for vv in vals]
    tot = e[0] + e[1] + e[2] + e[3]
    tw_ref[0] = jnp.concatenate(e, axis=1) / tot
    ti_ref[0] = jnp.concatenate(idxs, axis=1).astype(jnp.int32)


def _outproj(x3, of, y, bonus, g, gt, sc, sh, lnx_w, lnx_b, g_ffn, w_out_b, w_router, b_router, tm):
    bx, tx, d = x3.shape
    per_row = gt.shape[1] != 1
    mod_spec = (pl.BlockSpec((1, tm, d), lambda b, t: (b, t, 0)) if per_row
                else pl.BlockSpec((1, 1, d), lambda b, t: (b, 0, 0)))
    row = lambda w: pl.BlockSpec((1, tm, w), lambda b, t: (b, t, 0))
    const = lambda shape: pl.BlockSpec(shape, lambda b, t: tuple(0 for _ in shape))
    return pl.pallas_call(
        _outproj_kernel,
        grid=(bx, tx // tm),
        in_specs=[row(d), pl.BlockSpec((1, N_HEAD_PAIRS, tm, PAIR), lambda b, t: (b, 0, t, 0)),
                  row(WIDTH), row(WIDTH), row(WIDTH), mod_spec, mod_spec, mod_spec,
                  const((1, WIDTH)), const((1, WIDTH)), const((1, d)), const((2 * WIDTH, d)),
                  const((d, N_EXPERTS)), const((1, N_EXPERTS)), const((WIDTH, WIDTH))],
        out_specs=[row(d), row(d), row(TOP_K), row(TOP_K)],
        out_shape=[jax.ShapeDtypeStruct((bx, tx, d), F32), jax.ShapeDtypeStruct((bx, tx, d), BF16),
                   jax.ShapeDtypeStruct((bx, tx, TOP_K), jnp.int32), jax.ShapeDtypeStruct((bx, tx, TOP_K), F32)],
        compiler_params=_cparams("parallel", "arbitrary"),
    )(x3, of, y, bonus, g, gt, sc, sh, lnx_w, lnx_b, g_ffn, w_out_b, w_router, b_router, _head_ones(N_HEADS))


MOE_TM = 256


def _moe_kernel(te_ref, tv_ref, x_ref, ws_ref, wg_ref, bg_ref, wu_ref, bu_ref, wd_ref, bd_ref, o_ref):
    del te_ref
    i = pl.program_id(0)

    @pl.when(tv_ref[i] != 0)
    def _():
        x = x_ref[...]
        glu = jnp.minimum(_dot(x, wg_ref[0]) + bg_ref[0], SWIGLU_LIMIT)
        lin = jnp.clip(_dot(x, wu_ref[0]) + bu_ref[0], -SWIGLU_LIMIT, SWIGLU_LIMIT)
        act = glu * _sigmoid(SWIGLU_ALPHA * glu) * (lin + 1.0)
        o_ref[...] = (_dot(act.astype(BF16), wd_ref[0]) + bd_ref[0]) * ws_ref[...]

    @pl.when(tv_ref[i] == 0)
    def _():
        o_ref[...] = jnp.zeros_like(o_ref)


def _moe_gemm(tile_expert, tile_valid, xg, w_slot, wg, bg, wu, bu, wd, bd):
    n_slots, d = xg.shape
    n_tiles = n_slots // MOE_TM
    wspec = pl.BlockSpec((1, d, d), lambda i, te, tv: (te[i], 0, 0))
    bspec = pl.BlockSpec((1, 1, d), lambda i, te, tv: (te[i], 0, 0))
    grid_spec = pltpu.PrefetchScalarGridSpec(
        num_scalar_prefetch=2,
        grid=(n_tiles,),
        in_specs=[pl.BlockSpec((MOE_TM, d), lambda i, te, tv: (i, 0)),
                  pl.BlockSpec((MOE_TM, 1), lambda i, te, tv: (i, 0)),
                  wspec, bspec, wspec, bspec, wspec, bspec],
        out_specs=pl.BlockSpec((MOE_TM, d), lambda i, te, tv: (i, 0)),
    )
    return pl.pallas_call(
        _moe_kernel,
        grid_spec=grid_spec,
        out_shape=jax.ShapeDtypeStruct((n_slots, d), F32),
        compiler_params=_cparams("arbitrary"),
    )(tile_expert, tile_valid, xg, w_slot, wg, bg, wu, bu, wd, bd)


def _final_kernel(x1_ref, yg_ref, gt_ref, o_ref):
    yg = yg_ref[...]
    moe = (yg[0, 0] + yg[1, 0]) + (yg[2, 0] + yg[3, 0])
    o_ref[0] = x1_ref[0] + gt_ref[0] * moe


def _final(x1, yg, gt, tm):
    bx, tx, d = x1.shape
    per_row = gt.shape[1] != 1
    mod_spec = (pl.BlockSpec((1, tm, d), lambda b, t: (b, t, 0)) if per_row
                else pl.BlockSpec((1, 1, d), lambda b, t: (b, 0, 0)))
    row = pl.BlockSpec((1, tm, d), lambda b, t: (b, t, 0))
    return pl.pallas_call(
        _final_kernel,
        grid=(bx, tx // tm),
        in_specs=[row, pl.BlockSpec((TOP_K, 1, tm, d), lambda b, t: (0, b, t, 0)), mod_spec],
        out_specs=row,
        out_shape=jax.ShapeDtypeStruct((bx, tx, d), F32),
        compiler_params=_cparams("parallel", "arbitrary"),
    )(x1, yg, gt)


def _dispatch_plan(top_i, top_w):
    n_assign = top_i.size
    n_tiles = -(-n_assign // MOE_TM) + N_EXPERTS
    n_slots = n_tiles * MOE_TM
    e_flat = top_i.reshape(-1)
    order = jnp.argsort(e_flat, stable=True).astype(jnp.int32)
    rank_of_assign = jnp.argsort(order).astype(jnp.int32)
    cnt_end = jnp.searchsorted(e_flat[order], jnp.arange(N_EXPERTS, dtype=jnp.int32), side="right").astype(jnp.int32)
    cnt_start = jnp.concatenate([jnp.zeros((1,), jnp.int32), cnt_end[:-1]])
    counts = cnt_end - cnt_start
    padded = ((counts + MOE_TM - 1) // MOE_TM) * MOE_TM
    pad_end = jnp.cumsum(padded)
    pad_start = pad_end - padded
    tile_start = jnp.arange(n_tiles, dtype=jnp.int32) * MOE_TM
    tile_expert = jnp.minimum(jnp.searchsorted(pad_end, tile_start, side="right"), N_EXPERTS - 1).astype(jnp.int32)
    tile_valid = (tile_start < pad_end[-1]).astype(jnp.int32)
    e_slot = jnp.repeat(tile_expert, MOE_TM)
    rank = jnp.arange(n_slots, dtype=jnp.int32) - pad_start[e_slot]
    live = rank < counts[e_slot]
    src = order[jnp.clip(cnt_start[e_slot] + rank, 0, n_assign - 1)]
    token_of_slot = jnp.where(live, src // TOP_K, 0)
    w_of_slot = jnp.where(live, top_w.reshape(-1)[src], 0.0)
    slot_of_assign = pad_start[e_flat] + (rank_of_assign - cnt_start[e_flat])
    return slot_of_assign, token_of_slot, w_of_slot, tile_expert, tile_valid


def _pick_tile(n, pref):
    t = min(n, pref)
    assert n % t == 0, (n, t)
    return t


def kernel(x_prompt, x_sample, cache_k, cache_v, cache_logf, state_wkv, state_shift, page_table, c_prompt,
           c_sample, w_mod, b_mod, g_mix, g_ffn, w_in, q_norm, k_norm, b_forget, rw_mu, rw_w0, rw_w2, rw_a0,
           rw_a2, rw_g2, rw_kk, rw_ka, rw_rk, lnx_w, lnx_b, w_out, w_router, b_router, w_gate, b_gate, w_up,
           b_up, w_down, b_down):
    depth = w_mod.shape[0]
    assert depth == 1
    n_p, t_p, d = x_prompt.shape
    n_s, t_s, _ = x_sample.shape
    assert d == D_MODEL and t_s * N_HEADS == QROWS
    l = 0

    w_in_l = w_in[l]
    w_r = jnp.concatenate([w_in_l[:, :3 * WIDTH], w_in_l[:, FOX_COLS:], w_in_l[:, 3 * WIDTH:FOX_COLS],
                           jnp.zeros((d, 128 - N_HEADS), F32)], axis=1).astype(BF16)
    wfz_t = w_in_l[:, 3 * WIDTH:FOX_COLS].T.astype(BF16)
    gq = (jnp.tile(q_norm[l], N_HEADS) * ATTN_SCALE).reshape(1, WIDTH)
    gk = jnp.tile(k_norm[l], N_HEADS).reshape(1, WIDTH)
    zpad = jnp.zeros((DECAY_LORA, WIDTH), F32)
    w2p = jnp.concatenate([rw_w2[l], zpad], axis=0)
    a2p = jnp.concatenate([zpad, rw_a2[l]], axis=0)
    row = lambda a: a.reshape(1, -1)

    n_c = n_p + n_s
    pad_c = (-n_c) % 8
    c_all = jnp.concatenate([c_sample, c_prompt, jnp.zeros((pad_c, d), F32)], axis=0)
    mod = _mod(c_all, w_mod[l], b_mod[l])
    mod_s = mod[:n_s].reshape(n_s, 1, N_MOD, d)
    mod_p = mod[n_s:n_s + n_p].reshape(n_p, 1, N_MOD, d)
    mod_s = jnp.broadcast_to(mod_s, (n_s, t_s, N_MOD, d)).reshape(1, n_s * t_s, N_MOD, d)
    mp = [mod_p[:, :, i] for i in range(N_MOD)]
    msm = [mod_s[:, :, i] for i in range(N_MOD)]

    xs3 = x_sample.reshape(1, n_s * t_s, d)
    tm_p = _pick_tile(t_p, 256)
    tm_s = _pick_tile(n_s * t_s, 256)

    proj_args = (row(g_mix[l]), w_r, wfz_t, gq, gk, b_forget[l])
    qh_p, kh_p, vh_p, k_p, v_p, lf_p, lft_p, rw_p = _inproj(x_prompt, mp[1], mp[0], *proj_args, tm_p)
    qh_s, _, _, k_s, v_s, lf_s, _, rw_s = _inproj(xs3, msm[1], msm[0], *proj_args, tm_s)

    ft = _cumsum_t(lft_p, _pick_tile(t_p, 512))
    of_p = _fox_prompt(qh_p, kh_p, vh_p, ft, _pick_tile(t_p, 512))

    q_s = jnp.transpose(qh_s[0], (1, 0, 2)).reshape(n_s, QROWS, HEAD_DIM)
    kn_s = k_s.reshape(n_s, QROWS, HEAD_DIM).astype(BF16)
    vn_s = v_s.reshape(n_s, QROWS, HEAD_DIM).astype(BF16)
    cache_kt = jnp.transpose(cache_k[l], (0, 2, 3, 1))
    cache_vt = jnp.transpose(cache_v[l], (0, 2, 3, 1))
    cache_lft = jnp.swapaxes(cache_logf[l], 1, 2)
    o_s = _fox_decode(page_table, q_s, kn_s, vn_s, lf_s.reshape(n_s, QROWS), cache_kt, cache_vt, cache_lft)
    of_s = jnp.transpose(o_s.reshape(n_s * t_s, N_HEAD_PAIRS, PAIR), (1, 0, 2))[None].astype(BF16)

    prep_w = (row(rw_mu[l]), row(rw_w0[l]), w2p, row(rw_a0[l]), a2p, rw_g2[l], row(rw_kk[l]), row(rw_ka[l]),
              row(rw_rk[l]))
    prev_p = jnp.concatenate([jnp.zeros((n_p, 1, RWKV_COLS), F32), rw_p[:, :-1]], axis=1)
    rw_s4 = rw_s.reshape(n_s, t_s, RWKV_COLS)
    prev_s = jnp.concatenate([state_shift[l][:, None, :], rw_s4[:, :-1]], axis=1).reshape(1, n_s * t_s, RWKV_COLS)
    r_p, kk_p, vv_p, w_p, a_p, b_p, g_p, bonus_p = _rwkv_prep(rw_p, prev_p, *prep_w, tm_p)
    r_s, kk_s, vv_s, w_s, a_s, b_s, g_s, bonus_s = _rwkv_prep(rw_s, prev_s, *prep_w, tm_s)
    y_p, wkv_p = _wkv_scan(r_p, kk_p, vv_p, w_p, a_p, b_p,
                           jnp.zeros((n_p, N_HEADS, HEAD_DIM, HEAD_DIM), F32), _pick_tile(t_p, 256))
    to_seq = lambda a: a.reshape(n_s, t_s, WIDTH)
    y_s, wkv_s = _wkv_scan(to_seq(r_s), to_seq(kk_s), to_seq(vv_s), to_seq(w_s), to_seq(a_s), to_seq(b_s),
                           state_wkv[l], t_s)
    y_s = y_s.reshape(1, n_s * t_s, WIDTH)

    out_w = (row(lnx_w[l]), row(lnx_b[l]), row(g_ffn[l]), w_out[l].astype(BF16), w_router[l], row(b_router[l]))
    x1_p, h2_p, ti_p, tw_p = _outproj(x_prompt, of_p, y_p, bonus_p, g_p, mp[2], mp[4], mp[3], *out_w, tm_p)
    x1_s, h2_s, ti_s, tw_s = _outproj(xs3, of_s, y_s, bonus_s, g_s, msm[2], msm[4], msm[3], *out_w, tm_s)

    n_tok_p = n_p * t_p
    h2 = jnp.concatenate([h2_p.reshape(n_tok_p, d), h2_s.reshape(n_s * t_s, d)], axis=0)
    top_i = jnp.concatenate([ti_p.reshape(n_tok_p, TOP_K), ti_s.reshape(n_s * t_s, TOP_K)], axis=0)
    top_w = jnp.concatenate([tw_p.reshape(n_tok_p, TOP_K), tw_s.reshape(n_s * t_s, TOP_K)], axis=0)
    slot_of_assign, token_of_slot, w_of_slot, tile_expert, tile_valid = _dispatch_plan(top_i, top_w)
    take_rows = lambda a, idx: a.at[idx].get(mode="promise_in_bounds")
    xg = take_rows(h2, token_of_slot)
    ys = _moe_gemm(tile_expert, tile_valid, xg, w_of_slot.reshape(-1, 1),
                   w_gate[l].astype(BF16), b_gate[l].reshape(N_EXPERTS, 1, d),
                   w_up[l].astype(BF16), b_up[l].reshape(N_EXPERTS, 1, d),
                   w_down[l].astype(BF16), b_down[l].reshape(N_EXPERTS, 1, d))
    slots = slot_of_assign.reshape(-1, TOP_K).T
    yg_p = take_rows(ys, slots[:, :n_tok_p])
    yg_s = take_rows(ys, slots[:, n_tok_p:])
    y_prompt = _final(x1_p, yg_p.reshape(TOP_K, n_p, t_p, d), mp[5], tm_p)
    y_sample = _final(x1_s, yg_s.reshape(TOP_K, 1, n_s * t_s, d), msm[5], tm_s)

    heads5 = lambda a, nb, nt: a.reshape(1, nb, nt, N_HEADS, HEAD_DIM)
    return (y_prompt, y_sample.reshape(n_s, t_s, d),
            heads5(k_p, n_p, t_p), heads5(v_p, n_p, t_p), lf_p[None], wkv_p[None], rw_p[:, -1][None],
            heads5(k_s, n_s, t_s), heads5(v_s, n_s, t_s), lf_s.reshape(1, n_s, t_s, N_HEADS), wkv_s[None],
            rw_s4[:, -1][None])
```

```python
import functools

import jax
import jax.numpy as jnp
from jax import lax
from jax.experimental import pallas as pl
from jax.experimental.pallas import tpu as pltpu

F32 = jnp.float32
BF16 = jnp.bfloat16

D_MODEL = 1024
HEAD_DIM = 64
N_HEADS = 8
WIDTH = N_HEADS * HEAD_DIM
PAIR = 2 * HEAD_DIM
N_HEAD_PAIRS = N_HEADS // 2
DECAY_LORA = 64
ICLR_LORA = 64
GATE_LORA = 128
FOX_COLS = 3 * WIDTH + N_HEADS
RWKV_COLS = 3 * WIDTH + DECAY_LORA + ICLR_LORA + GATE_LORA
IN_COLS_PADDED = 3 * WIDTH + RWKV_COLS + 128
FZ_OFF = 3 * WIDTH + RWKV_COLS
ATTN_SCALE = HEAD_DIM ** -0.5
N_EXPERTS = 32
TOP_K = 4
SWIGLU_ALPHA = 1.702
SWIGLU_LIMIT = 7.0
RMS_EPS = 1e-6
GN_EPS = 64e-5
N_MOD = 6
PAGE = 128
NEG = -1e30

VMEM_LIMIT = 56 * 1024 * 1024


def _cparams(*sem):
    return pltpu.CompilerParams(dimension_semantics=sem, vmem_limit_bytes=VMEM_LIMIT)


def _dot(a, b):
    return jnp.dot(a, b, preferred_element_type=F32)


def _dot_nt(a, b):
    return lax.dot_general(a, b, (((1,), (1,)), ((), ())), preferred_element_type=F32)


def _split2(x):
    hi = x.astype(BF16)
    lo = (x - hi.astype(F32)).astype(BF16)
    return hi, lo


def _split3(x):
    hi = x.astype(BF16)
    r = x - hi.astype(F32)
    mid = r.astype(BF16)
    lo = (r - mid.astype(F32)).astype(BF16)
    return hi, mid, lo


def _dot_x3(a, b):
    ah, al = _split2(a)
    bh, bl = _split2(b)
    return _dot(ah, bh) + (_dot(ah, bl) + _dot(al, bh))


def _dot_sel2(a, sel):
    hi, lo = _split2(a)
    return _dot(hi, sel) + _dot(lo, sel)


def _dot_sel3(a, sel):
    hi, mid, lo = _split3(a)
    return _dot(hi, sel) + (_dot(mid, sel) + _dot(lo, sel))


def _sigmoid(x):
    return 1.0 / (1.0 + jnp.exp(-x))


def _softplus(x):
    return jnp.maximum(x, 0.0) + jnp.log(1.0 + jnp.exp(-jnp.abs(x)))


def _head_ones(n_heads):
    w = n_heads * HEAD_DIM
    r = lax.broadcasted_iota(jnp.int32, (w, w), 0) // HEAD_DIM
    c = lax.broadcasted_iota(jnp.int32, (w, w), 1) // HEAD_DIM
    return (r == c).astype(BF16)


def _mod_kernel(c_ref, w_ref, b_ref, o_ref):
    c = c_ref[...]
    s = c * _sigmoid(c)
    o_ref[...] = _dot_x3(s, w_ref[...]) + b_ref[...]


def _mod(c, w_mod, b_mod):
    rows, d = c.shape
    n = w_mod.shape[1]
    tn = 768
    return pl.pallas_call(
        _mod_kernel,
        grid=(n // tn,),
        in_specs=[pl.BlockSpec((rows, d), lambda j: (0, 0)),
                  pl.BlockSpec((d, tn), lambda j: (0, j)),
                  pl.BlockSpec((1, tn), lambda j: (0, j))],
        out_specs=pl.BlockSpec((rows, tn), lambda j: (0, j)),
        out_shape=jax.ShapeDtypeStruct((rows, n), F32),
        compiler_params=_cparams("arbitrary"),
    )(c, w_mod, b_mod.reshape(1, n))


def _inproj_kernel(x_ref, sc_ref, sh_ref, g_ref, w_ref, wfz_ref, gq_ref, gk_ref, bf_ref, bfc_ref, hs_ref,
                   qh_ref, kh_ref, vh_ref, k_ref, v_ref, lf_ref, lft_ref, rw_ref):
    x = x_ref[0]
    ms = jnp.mean(x * x, axis=-1, keepdims=True)
    xn = x * lax.rsqrt(ms + RMS_EPS) * g_ref[...]
    h = (xn * (1.0 + sc_ref[0]) + sh_ref[0]).astype(BF16)
    proj = _dot(h, w_ref[...])
    hs = hs_ref[...]

    def head_norm(t, gain):
        m = _dot_sel2(t * t, hs) * (1.0 / HEAD_DIM)
        return t * lax.rsqrt(m + RMS_EPS) * gain

    q = head_norm(proj[:, 0:WIDTH], gq_ref[...])
    k = head_norm(proj[:, WIDTH:2 * WIDTH], gk_ref[...])
    v = proj[:, 2 * WIDTH:3 * WIDTH]
    k_ref[0] = k
    v_ref[0] = v
    qb = q.astype(BF16)
    kb = k.astype(BF16)
    vb = v.astype(BF16)
    for hp in range(N_HEAD_PAIRS):
        sl = slice(hp * PAIR, (hp + 1) * PAIR)
        qh_ref[0, hp] = qb[:, sl]
        kh_ref[0, hp] = kb[:, sl]
        vh_ref[0, hp] = vb[:, sl]
    rw_ref[0] = proj[:, 3 * WIDTH:3 * WIDTH + RWKV_COLS]
    fz = proj[:, FZ_OFF:FZ_OFF + N_HEADS] + bf_ref[...]
    lf_ref[0] = -_softplus(-fz)
    fzt = _dot_nt(wfz_ref[...], h) + bfc_ref[...]
    lft_ref[0] = -_softplus(-fzt)


def _inproj(x3, sc, sh, g_mix, w_r, wfz_t, gq, gk, b_forget, tm):
    bx, tx, d = x3.shape
    per_row = sc.shape[1] != 1
    mod_spec = (pl.BlockSpec((1, tm, d), lambda b, t: (b, t, 0)) if per_row
                else pl.BlockSpec((1, 1, d), lambda b, t: (b, 0, 0)))
    const = lambda shape: pl.BlockSpec(shape, lambda b, t: tuple(0 for _ in shape))
    head_spec = pl.BlockSpec((1, N_HEAD_PAIRS, tm, PAIR), lambda b, t: (b, 0, t, 0))
    row_spec = lambda w: pl.BlockSpec((1, tm, w), lambda b, t: (b, t, 0))
    head_shape = jax.ShapeDtypeStruct((bx, N_HEAD_PAIRS, tx, PAIR), BF16)
    return pl.pallas_call(
        _inproj_kernel,
        grid=(bx, tx // tm),
        in_specs=[row_spec(d), mod_spec, mod_spec, const((1, d)), const((d, IN_COLS_PADDED)),
                  const((N_HEADS, d)), const((1, WIDTH)), const((1, WIDTH)), const((1, N_HEADS)),
                  const((N_HEADS, 1)), const((WIDTH, WIDTH))],
        out_specs=[head_spec, head_spec, head_spec, row_spec(WIDTH), row_spec(WIDTH), row_spec(N_HEADS),
                   pl.BlockSpec((1, N_HEADS, tm), lambda b, t: (b, 0, t)), row_spec(RWKV_COLS)],
        out_shape=[head_shape, head_shape, head_shape,
                   jax.ShapeDtypeStruct((bx, tx, WIDTH), F32), jax.ShapeDtypeStruct((bx, tx, WIDTH), F32),
                   jax.ShapeDtypeStruct((bx, tx, N_HEADS), F32), jax.ShapeDtypeStruct((bx, N_HEADS, tx), F32),
                   jax.ShapeDtypeStruct((bx, tx, RWKV_COLS), F32)],
        compiler_params=_cparams("parallel", "arbitrary"),
    )(x3, sc, sh, g_mix, w_r, wfz_t, gq, gk, b_forget.reshape(1, N_HEADS), b_forget.reshape(N_HEADS, 1),
      _head_ones(N_HEADS))


def _cumsum_kernel(l_ref, u_ref, o_ref, c_ref):
    tc = l_ref.shape[2]

    @pl.when(pl.program_id(1) == 0)
    def _():
        c_ref[...] = jnp.zeros_like(c_ref)

    f = _dot_sel3(l_ref[0], u_ref[...]) + c_ref[:, 0:1]
    o_ref[0] = f
    c_ref[...] = jnp.broadcast_to(f[:, tc - 1:tc], c_ref.shape)


def _cumsum_t(lft, tc):
    b, h, t = lft.shape
    r = lax.broadcasted_iota(jnp.int32, (tc, tc), 0)
    c = lax.broadcasted_iota(jnp.int32, (tc, tc), 1)
    upper = (r <= c).astype(BF16)
    return pl.pallas_call(
        _cumsum_kernel,
        grid=(b, t // tc),
        in_specs=[pl.BlockSpec((1, h, tc), lambda i, j: (i, 0, j)), pl.BlockSpec((tc, tc), lambda i, j: (0, 0))],
        out_specs=pl.BlockSpec((1, h, tc), lambda i, j: (i, 0, j)),
        out_shape=jax.ShapeDtypeStruct((b, h, t), F32),
        scratch_shapes=[pltpu.VMEM((h, 128), F32)],
        compiler_params=_cparams("parallel", "arbitrary"),
    )(lft, upper)


def _attn_kernel(q_ref, k_ref, v_ref, f_ref, o_ref, m_ref, l_ref, acc_ref, *, tq, tk):
    i = pl.program_id(2)
    q2 = q_ref[0, 0]
    lane = lax.broadcasted_iota(jnp.int32, q2.shape, 1)
    zero = jnp.zeros_like(q2)
    qs = (jnp.where(lane < HEAD_DIM, q2, zero), jnp.where(lane >= HEAD_DIM, q2, zero))
    q0 = pl.multiple_of(i * tq, tq)
    f_base = [f_ref[0, 0, hh:hh + 1, pl.ds(q0, tq)][:, 0:1] for hh in range(2)]
    m_ref[...] = jnp.full_like(m_ref, NEG)
    l_ref[...] = jnp.zeros_like(l_ref)
    acc_ref[...] = jnp.zeros_like(acc_ref)

    def tile(j, masked):
        k0 = pl.multiple_of(j * tk, tk)
        k2 = k_ref[0, 0, pl.ds(k0, tk), :]
        v2 = v_ref[0, 0, pl.ds(k0, tk), :]
        for hh in range(2):
            s = _dot_nt(qs[hh], k2) + (f_base[hh] - f_ref[0, 0, hh:hh + 1, pl.ds(k0, tk)])
            if masked:
                r = lax.broadcasted_iota(jnp.int32, s.shape, 0)
                c = lax.broadcasted_iota(jnp.int32, s.shape, 1)
                s = jnp.where(c + k0 <= r + q0, s, NEG)
            m_old = m_ref[hh]
            m_new = jnp.maximum(m_old, jnp.max(s, axis=1, keepdims=True))
            p = jnp.exp(s - jnp.concatenate([m_new] * (tk // 128), axis=1))
            alpha = jnp.exp(m_old - m_new)
            l_ref[hh] = alpha * l_ref[hh] + jnp.sum(p, axis=1, keepdims=True)
            acc_ref[hh] = alpha * acc_ref[hh] + _dot(p.astype(BF16), v2)
            m_ref[hh] = m_new

    def body(j, carry):
        tile(j, False)
        return carry

    n_sub = tq // tk
    lax.fori_loop(0, i * n_sub, body, 0)
    for dd in range(n_sub):
        tile(i * n_sub + dd, True)
    o_a = acc_ref[0] / l_ref[0]
    o_b = acc_ref[1] / l_ref[1]
    o_ref[0, 0] = jnp.where(lane < HEAD_DIM, o_a, o_b).astype(o_ref.dtype)


def _fox_prompt(qh, kh, vh, ft, tq, tk):
    b, hp, t, w = qh.shape
    assert tq % tk == 0 and tk % 128 == 0
    ft4 = ft.reshape(b, hp, 2, t)
    seq_spec = pl.BlockSpec((1, 1, t, w), lambda bi, hi, i: (bi, hi, 0, 0))
    tile_spec = pl.BlockSpec((1, 1, tq, w), lambda bi, hi, i: (bi, hi, i, 0))
    return pl.pallas_call(
        functools.partial(_attn_kernel, tq=tq, tk=tk),
        grid=(b, hp, t // tq),
        in_specs=[tile_spec, seq_spec, seq_spec, pl.BlockSpec((1, 1, 2, t), lambda bi, hi, i: (bi, hi, 0, 0))],
        out_specs=tile_spec,
        out_shape=jax.ShapeDtypeStruct((b, hp, t, w), BF16),
        scratch_shapes=[pltpu.VMEM((2, tq, 128), F32), pltpu.VMEM((2, tq, 128), F32), pltpu.VMEM((2, tq, w), F32)],
        compiler_params=_cparams("parallel", "parallel", "arbitrary"),
    )(qh, kh, vh, ft4)


PAGES_PER_STEP = 8
QROWS = 4 * N_HEADS


def _decode_kernel(pt_ref, q_ref, kn_ref, vn_ref, lfn_ref, cnew_ref, ustrict_ref, fold_ref, *refs, n_steps):
    del pt_ref
    npg = PAGES_PER_STEP
    k_refs = refs[0:npg]
    v_refs = refs[npg:2 * npg]
    lf_refs = refs[2 * npg:3 * npg]
    o_ref, m_ref, l_ref, acc_ref, accn_ref, carry_ref = refs[3 * npg:]
    j = pl.program_id(1)
    q = q_ref[0]
    r = lax.broadcasted_iota(jnp.int32, (QROWS, WIDTH), 0)
    c = lax.broadcasted_iota(jnp.int32, (QROWS, WIDTH), 1)
    own_head = (r % N_HEADS) == (c // HEAD_DIM)
    q_bd = jnp.where(own_head, jnp.concatenate([q] * N_HEADS, axis=1), jnp.zeros((), q.dtype))

    def softmax_step(s):
        m_old = m_ref[...]
        m_new = jnp.maximum(m_old, jnp.max(s, axis=1, keepdims=True))
        p = jnp.exp(s - m_new)
        alpha = jnp.exp(m_old - m_new)
        l_ref[...] = alpha * l_ref[...] + jnp.sum(p, axis=1, keepdims=True)
        m_ref[...] = m_new
        return p.astype(BF16), alpha

    @pl.when(j == 0)
    def _():
        m_ref[...] = jnp.full_like(m_ref, NEG)
        l_ref[...] = jnp.zeros_like(l_ref)
        acc_ref[...] = jnp.zeros_like(acc_ref)
        carry_ref[...] = jnp.zeros_like(carry_ref)
        s = _dot_nt(q, kn_ref[0])
        bias = _dot_sel3(lfn_ref[0], cnew_ref[...])[0:1, :]
        rr = lax.broadcasted_iota(jnp.int32, s.shape, 0)
        cc = lax.broadcasted_iota(jnp.int32, s.shape, 1)
        ok = ((rr % N_HEADS) == (cc % N_HEADS)) & ((cc // N_HEADS) <= (rr // N_HEADS))
        p, _ = softmax_step(jnp.where(ok, s + bias, NEG))
        accn_ref[...] = _dot(p, vn_ref[0])

    ustrict = ustrict_ref[...]
    carry = carry_ref[...]
    scores = []
    for u in range(npg):
        kt = k_refs[u][...].reshape(WIDTH, PAGE).astype(BF16)
        lft = lf_refs[u][...]
        a = jnp.broadcast_to(lft[None], (4, N_HEADS, PAGE)).reshape(QROWS, PAGE)
        hi, mid, lo = _split3(a)
        bx = _dot(jnp.concatenate([hi, mid, lo], axis=0), ustrict)
        bias = bx[0:QROWS] + (bx[QROWS:2 * QROWS] + bx[2 * QROWS:3 * QROWS]) + carry
        carry = carry + jnp.sum(a, axis=1, keepdims=True)
        scores.append(_dot(q_bd, kt) + bias)
    carry_ref[...] = carry
    p, alpha = softmax_step(jnp.concatenate(scores, axis=1))
    acc = alpha * acc_ref[...]
    for u in range(npg):
        vt = v_refs[u][...].reshape(WIDTH, PAGE).astype(BF16)
        acc = acc + _dot_nt(p[:, u * PAGE:(u + 1) * PAGE], vt)
    acc_ref[...] = acc
    accn_ref[...] = alpha * accn_ref[...]

    @pl.when(j == n_steps - 1)
    def _():
        own = jnp.where(own_head, acc_ref[...], 0.0)
        o_ref[0] = (_dot_sel3(own, fold_ref[...]) + accn_ref[...]) / l_ref[...]


def _fox_decode(page_table, q, k_new, v_new, lf_new, cache_kt, cache_vt, cache_lft):
    nb, n_pages = page_table.shape
    npg = PAGES_PER_STEP
    n_steps = n_pages // npg
    ustrict = (lax.broadcasted_iota(jnp.int32, (PAGE, PAGE), 0)
               > lax.broadcasted_iota(jnp.int32, (PAGE, PAGE), 1)).astype(BF16)
    ri = lax.broadcasted_iota(jnp.int32, (QROWS, QROWS), 0)
    ci = lax.broadcasted_iota(jnp.int32, (QROWS, QROWS), 1)
    cnew = -(((ri % N_HEADS) == (ci % N_HEADS)) & ((ri // N_HEADS) <= (ci // N_HEADS))).astype(BF16)
    fold = (lax.broadcasted_iota(jnp.int32, (WIDTH, HEAD_DIM), 0) % HEAD_DIM
            == lax.broadcasted_iota(jnp.int32, (WIDTH, HEAD_DIM), 1)).astype(BF16)

    def page_idx(u):
        return lambda b, j, pt: (pt[b, n_pages - 1 - (j * npg + u)], 0, 0, 0)

    def page_idx3(u):
        return lambda b, j, pt: (pt[b, n_pages - 1 - (j * npg + u)], 0, 0)

    per_b = lambda shape: pl.BlockSpec((1,) + shape, lambda b, j, pt: (b, 0, 0))
    const = lambda shape: pl.BlockSpec(shape, lambda b, j, pt: (0, 0))
    in_specs = ([per_b((QROWS, HEAD_DIM)), per_b((QROWS, HEAD_DIM)), per_b((QROWS, HEAD_DIM)),
                 per_b((8, QROWS)), const((QROWS, QROWS)), const((PAGE, PAGE)), const((WIDTH, HEAD_DIM))]
                + [pl.BlockSpec((None, N_HEADS, HEAD_DIM, PAGE), page_idx(u)) for u in range(npg)]
                + [pl.BlockSpec((None, N_HEADS, HEAD_DIM, PAGE), page_idx(u)) for u in range(npg)]
                + [pl.BlockSpec((None, N_HEADS, PAGE), page_idx3(u)) for u in range(npg)])
    grid_spec = pltpu.PrefetchScalarGridSpec(
        num_scalar_prefetch=1,
        grid=(nb, n_steps),
        in_specs=in_specs,
        out_specs=pl.BlockSpec((1, QROWS, HEAD_DIM), lambda b, j, pt: (b, 0, 0)),
        scratch_shapes=[pltpu.VMEM((QROWS, 1), F32), pltpu.VMEM((QROWS, 1), F32),
                        pltpu.VMEM((QROWS, WIDTH), F32), pltpu.VMEM((QROWS, HEAD_DIM), F32),
                        pltpu.VMEM((QROWS, 1), F32)],
    )
    lfn = jnp.broadcast_to(lf_new.reshape(nb, 1, QROWS), (nb, 8, QROWS))
    return pl.pallas_call(
        functools.partial(_decode_kernel, n_steps=n_steps),
        grid_spec=grid_spec,
        out_shape=jax.ShapeDtypeStruct((nb, QROWS, HEAD_DIM), F32),
        compiler_params=_cparams("parallel", "arbitrary"),
    )(page_table, q, k_new, v_new, lfn, cnew, ustrict, fold,
      *([cache_kt] * npg), *([cache_vt] * npg), *([cache_lft] * npg))


def _rwkv_prep_kernel(p_ref, prev_ref, mu_ref, w0_ref, w2_ref, a0_ref, a2_ref, g2_ref, kk_ref, ka_ref, rk_ref,
                      hs_ref, r_ref, k_ref, v_ref, w_ref, a_ref, b_ref, g_ref, bonus_ref):
    p = p_ref[0]
    z = p + (prev_ref[0] - p) * mu_ref[...]
    r = z[:, 0:WIDTH]
    k = z[:, WIDTH:2 * WIDTH]
    v = z[:, 2 * WIDTH:3 * WIDTH]
    lora = z[:, 3 * WIDTH:3 * WIDTH + DECAY_LORA + ICLR_LORA]
    gd = z[:, 3 * WIDTH + DECAY_LORA + ICLR_LORA:RWKV_COLS]
    hs = hs_ref[...]
    w = -_softplus(-(w0_ref[...] + _dot_x3(jnp.tanh(lora), w2_ref[...]))) - 0.5
    decay = jnp.exp(-jnp.exp(w))
    a = _sigmoid(a0_ref[...] + _dot_x3(lora, a2_ref[...]))
    g = _dot_x3(_sigmoid(gd), g2_ref[...])
    kk = k * kk_ref[...]
    norm = jnp.sqrt(_dot_sel3(kk * kk, hs))
    kk = kk / jnp.maximum(norm, 1e-12)
    k = k * (1.0 + (a - 1.0) * ka_ref[...])
    bonus = _dot_sel3(r * k * rk_ref[...], hs) * v
    r_ref[0] = r
    k_ref[0] = k
    v_ref[0] = v
    w_ref[0] = decay
    a_ref[0] = -kk
    b_ref[0] = kk * a
    g_ref[0] = g
    bonus_ref[0] = bonus


def _rwkv_prep(p_rw, prev, mu, w0, w2p, a0, a2p, g2, kk, ka, rk, tm):
    bx, tx, _ = p_rw.shape
    row = lambda w: pl.BlockSpec((1, tm, w), lambda b, t: (b, t, 0))
    const = lambda shape: pl.BlockSpec(shape, lambda b, t: tuple(0 for _ in shape))
    out = jax.ShapeDtypeStruct((bx, tx, WIDTH), F32)
    return pl.pallas_call(
        _rwkv_prep_kernel,
        grid=(bx, tx // tm),
        in_specs=[row(RWKV_COLS), row(RWKV_COLS), const((1, RWKV_COLS)), const((1, WIDTH)),
                  const((DECAY_LORA + ICLR_LORA, WIDTH)), const((1, WIDTH)), const((DECAY_LORA + ICLR_LORA, WIDTH)),
                  const((GATE_LORA, WIDTH)), const((1, WIDTH)), const((1, WIDTH)), const((1, WIDTH)),
                  const((WIDTH, WIDTH))],
        out_specs=[row(WIDTH)] * 8,
        out_shape=[out] * 8,
        compiler_params=_cparams("parallel", "arbitrary"),
    )(p_rw, prev, mu, w0, w2p, a0, a2p, g2, kk, ka, rk, _head_ones(N_HEADS))


WKV_NB = 2
WKV_HG = 4
WKV_LANES = WKV_HG * HEAD_DIM
WKV_GROUPS = N_HEADS // WKV_HG
WKV_ROWS = WKV_NB * WKV_GROUPS * HEAD_DIM


def _wkv_kernel(r_ref, k_ref, v_ref, w_ref, a_ref, b_ref, s0_ref, ones_ref, y_ref, sT_ref, st_ref, *, tc, n_t):
    ti = pl.program_id(1)
    blocks = [(bi, gg) for bi in range(WKV_NB) for gg in range(WKV_GROUPS)]

    @pl.when(ti == 0)
    def _():
        for n, (bi, gg) in enumerate(blocks):
            st_ref[n * HEAD_DIM:(n + 1) * HEAD_DIM, :] = jnp.concatenate(
                [s0_ref[bi, gg * WKV_HG + q] for q in range(WKV_HG)], axis=1)

    rows = lax.broadcasted_iota(jnp.int32, (WKV_ROWS, WKV_LANES), 0)
    lanes = lax.broadcasted_iota(jnp.int32, (WKV_ROWS, WKV_LANES), 1)
    eye = (rows % HEAD_DIM) == (lanes % HEAD_DIM)
    eye_f = eye.astype(F32)
    ones = ones_ref[...]
    grp = min(tc, 8)

    def bcast(xs, u):
        return jnp.concatenate(
            [jnp.broadcast_to(xs[bi][u:u + 1, gg * WKV_LANES:(gg + 1) * WKV_LANES], (HEAD_DIM, WKV_LANES))
             for bi, gg in blocks], axis=0)

    def group(gi, carry):
        t0 = pl.multiple_of(gi * grp, grp)
        load = lambda ref: [ref[bi, pl.ds(t0, grp), :] for bi in range(WKV_NB)]
        r8, k8, v8, w8, a8, b8 = (load(ref) for ref in (r_ref, k_ref, v_ref, w_ref, a_ref, b_ref))
        vdiag = lambda u: jnp.where(eye, bcast(v8, u), 0.0).astype(BF16)
        s = st_ref[...]
        ys = []

        def y_rows(yb):
            yb = yb * eye_f
            return [jnp.sum(yb[n * HEAD_DIM:(n + 1) * HEAD_DIM], axis=0, keepdims=True) for n in range(len(blocks))]

        vcol = _dot(vdiag(0), ones)
        ybf = None
        for u in range(grp):
            hi, lo = _split2(s * bcast(a8, u))
            red = _dot(jnp.concatenate([hi, lo], axis=0), ones)
            side = [vdiag(u + 1)] if u + 1 < grp else []
            if ybf is not None:
                side.append(ybf)
            misc = _dot(jnp.concatenate(side, axis=0), ones) if side else None
            s = s * bcast(w8, u) + (red[0:WKV_ROWS] + red[WKV_ROWS:2 * WKV_ROWS]) * bcast(b8, u) + vcol * bcast(k8, u)
            if misc is not None:
                off = 0
                if u + 1 < grp:
                    vcol = misc[0:WKV_ROWS]
                    off = WKV_ROWS
                if ybf is not None:
                    ys.append(y_rows(misc[off:off + WKV_ROWS]))
            ybf = (s * bcast(r8, u)).astype(BF16)
        ys.append(y_rows(_dot(ybf, ones)))
        st_ref[...] = s
        for bi in range(WKV_NB):
            y_ref[bi, pl.ds(t0, grp), :] = jnp.concatenate(
                [jnp.concatenate([ys[u][bi * WKV_GROUPS + gg] for gg in range(WKV_GROUPS)], axis=1)
                 for u in range(grp)], axis=0)
        return carry

    lax.fori_loop(0, tc // grp, group, 0)

    @pl.when(ti == n_t - 1)
    def _():
        for n, (bi, gg) in enumerate(blocks):
            for q in range(WKV_HG):
                sT_ref[bi, gg * WKV_HG + q] = st_ref[n * HEAD_DIM:(n + 1) * HEAD_DIM, q * HEAD_DIM:(q + 1) * HEAD_DIM]


def _wkv_scan(r, k, v, w, a, b, s0, tc):
    bx, tx, _ = r.shape
    assert bx % WKV_NB == 0
    n_t = tx // tc
    seq = pl.BlockSpec((WKV_NB, tc, WIDTH), lambda i, t: (i, t, 0))
    st = pl.BlockSpec((WKV_NB, N_HEADS, HEAD_DIM, HEAD_DIM), lambda i, t: (i, 0, 0, 0))
    ri = lax.broadcasted_iota(jnp.int32, (WKV_LANES, WKV_LANES), 0) // HEAD_DIM
    ci = lax.broadcasted_iota(jnp.int32, (WKV_LANES, WKV_LANES), 1) // HEAD_DIM
    ones = (ri == ci).astype(BF16)
    return pl.pallas_call(
        functools.partial(_wkv_kernel, tc=tc, n_t=n_t),
        grid=(bx // WKV_NB, n_t),
        in_specs=[seq] * 6 + [st, pl.BlockSpec((WKV_LANES, WKV_LANES), lambda i, t: (0, 0))],
        out_specs=[seq, st],
        out_shape=[jax.ShapeDtypeStruct((bx, tx, WIDTH), F32),
                   jax.ShapeDtypeStruct((bx, N_HEADS, HEAD_DIM, HEAD_DIM), F32)],
        scratch_shapes=[pltpu.VMEM((WKV_ROWS, WKV_LANES), F32)],
        compiler_params=_cparams("parallel", "arbitrary"),
    )(r, k, v, w, a, b, s0, ones)


def _outproj_kernel(x_ref, of_ref, y_ref, bonus_ref, g_ref, gt_ref, sc_ref, sh_ref, lw_ref, lb_ref, gf_ref,
                    wo_ref, wr_ref, br_ref, hs_ref, lt_ref, x1_ref, h2_ref, ti_ref, tw_ref, rk_ref, cnt_ref, run_ref):
    @pl.when(pl.program_id(1) == 0)
    def _():
        run_ref[...] = jnp.zeros_like(run_ref)

    hs = hs_ref[...]
    y = y_ref[0]
    mu = _dot_sel3(y, hs) * (1.0 / HEAD_DIM)
    yc = y - mu
    var = _dot_sel3(yc * yc, hs) * (1.0 / HEAD_DIM)
    o_rw = ((yc * lax.rsqrt(var + GN_EPS)) * lw_ref[...] + lb_ref[...] + bonus_ref[0]) * g_ref[0]
    mixed = _dot(o_rw.astype(BF16), wo_ref[WIDTH:2 * WIDTH, :])
    for hp in range(N_HEAD_PAIRS):
        mixed = mixed + _dot(of_ref[0, hp], wo_ref[hp * PAIR:(hp + 1) * PAIR, :])
    x1 = x_ref[0] + gt_ref[0] * mixed
    x1_ref[0] = x1
    ms = jnp.mean(x1 * x1, axis=-1, keepdims=True)
    h2 = (x1 * lax.rsqrt(ms + RMS_EPS) * gf_ref[...]) * (1.0 + sc_ref[0]) + sh_ref[0]
    h2_ref[0] = h2.astype(BF16)
    logits = _dot_x3(h2, wr_ref[...]) + br_ref[...]
    lane = lax.broadcasted_iota(jnp.int32, logits.shape, 1).astype(F32)
    vals, idxs = [], []
    for _ in range(TOP_K):
        m = jnp.max(logits, axis=1, keepdims=True)
        idx = jnp.min(jnp.where(logits == m, lane, float(N_EXPERTS)), axis=1, keepdims=True)
        vals.append(m)
        idxs.append(idx)
        logits = jnp.where(lane == idx, -jnp.inf, logits)
    e = [jnp.exp(vv - vals[0]) for vv in vals]
    tot = e[0] + e[1] + e[2] + e[3]
    tw_ref[0] = jnp.concatenate(e, axis=1) / tot
    ti_ref[0] = jnp.concatenate(idxs, axis=1).astype(jnp.int32)
    chosen = [lane == idx for idx in idxs]
    multi_hot = jnp.where(chosen[0] | chosen[1] | chosen[2] | chosen[3], 1.0, 0.0)
    before = _dot(lt_ref[...], multi_hot.astype(BF16)) + run_ref[...]
    ranks = [jnp.sum(jnp.where(c, before, 0.0), axis=1, keepdims=True) for c in chosen]
    rk_ref[0] = jnp.concatenate(ranks, axis=1).astype(jnp.int32)
    run_ref[...] = run_ref[...] + jnp.sum(multi_hot, axis=0, keepdims=True)
    cnt_ref[0] = run_ref[...]


def _outproj(x3, of, y, bonus, g, gt, sc, sh, lnx_w, lnx_b, g_ffn, w_out_b, w_router, b_router, tm):
    bx, tx, d = x3.shape
    per_row = gt.shape[1] != 1
    mod_spec = (pl.BlockSpec((1, tm, d), lambda b, t: (b, t, 0)) if per_row
                else pl.BlockSpec((1, 1, d), lambda b, t: (b, 0, 0)))
    row = lambda w: pl.BlockSpec((1, tm, w), lambda b, t: (b, t, 0))
    const = lambda shape: pl.BlockSpec(shape, lambda b, t: tuple(0 for _ in shape))
    lower = (lax.broadcasted_iota(jnp.int32, (tm, tm), 0) > lax.broadcasted_iota(jnp.int32, (tm, tm), 1)).astype(BF16)
    return pl.pallas_call(
        _outproj_kernel,
        grid=(bx, tx // tm),
        in_specs=[row(d), pl.BlockSpec((1, N_HEAD_PAIRS, tm, PAIR), lambda b, t: (b, 0, t, 0)),
                  row(WIDTH), row(WIDTH), row(WIDTH), mod_spec, mod_spec, mod_spec,
                  const((1, WIDTH)), const((1, WIDTH)), const((1, d)), const((2 * WIDTH, d)),
                  const((d, N_EXPERTS)), const((1, N_EXPERTS)), const((WIDTH, WIDTH)), const((tm, tm))],
        out_specs=[row(d), row(d), row(TOP_K), row(TOP_K), row(TOP_K),
                   pl.BlockSpec((1, 1, N_EXPERTS), lambda b, t: (b, 0, 0))],
        out_shape=[jax.ShapeDtypeStruct((bx, tx, d), F32), jax.ShapeDtypeStruct((bx, tx, d), BF16),
                   jax.ShapeDtypeStruct((bx, tx, TOP_K), jnp.int32), jax.ShapeDtypeStruct((bx, tx, TOP_K), F32),
                   jax.ShapeDtypeStruct((bx, tx, TOP_K), jnp.int32), jax.ShapeDtypeStruct((bx, 1, N_EXPERTS), F32)],
        scratch_shapes=[pltpu.VMEM((1, N_EXPERTS), F32)],
        compiler_params=_cparams("parallel", "arbitrary"),
    )(x3, of, y, bonus, g, gt, sc, sh, lnx_w, lnx_b, g_ffn, w_out_b, w_router, b_router, _head_ones(N_HEADS), lower)


MOE_TM = 256


def _moe_kernel(te_ref, tv_ref, x_ref, ws_ref, wg_ref, bg_ref, wu_ref, bu_ref, wd_ref, bd_ref, o_ref):
    del te_ref
    i = pl.program_id(0)

    @pl.when(tv_ref[i] != 0)
    def _():
        x = x_ref[...]
        glu = jnp.minimum(_dot(x, wg_ref[0]) + bg_ref[0], SWIGLU_LIMIT)
        lin = jnp.clip(_dot(x, wu_ref[0]) + bu_ref[0], -SWIGLU_LIMIT, SWIGLU_LIMIT)
        act = glu * _sigmoid(SWIGLU_ALPHA * glu) * (lin + 1.0)
        o_ref[...] = ((_dot(act.astype(BF16), wd_ref[0]) + bd_ref[0]) * ws_ref[...]).astype(o_ref.dtype)

    @pl.when(tv_ref[i] == 0)
    def _():
        o_ref[...] = jnp.zeros_like(o_ref)


def _moe_gemm(tile_expert, tile_valid, xg, w_slot, wg, bg, wu, bu, wd, bd):
    n_slots, d = xg.shape
    n_tiles = n_slots // MOE_TM
    wspec = pl.BlockSpec((1, d, d), lambda i, te, tv: (te[i], 0, 0))
    bspec = pl.BlockSpec((1, 1, d), lambda i, te, tv: (te[i], 0, 0))
    grid_spec = pltpu.PrefetchScalarGridSpec(
        num_scalar_prefetch=2,
        grid=(n_tiles,),
        in_specs=[pl.BlockSpec((MOE_TM, d), lambda i, te, tv: (i, 0)),
                  pl.BlockSpec((MOE_TM, 1), lambda i, te, tv: (i, 0)),
                  wspec, bspec, wspec, bspec, wspec, bspec],
        out_specs=pl.BlockSpec((MOE_TM, d), lambda i, te, tv: (i, 0)),
    )
    return pl.pallas_call(
        _moe_kernel,
        grid_spec=grid_spec,
        out_shape=jax.ShapeDtypeStruct((n_slots, d), BF16),
        compiler_params=_cparams("arbitrary"),
    )(tile_expert, tile_valid, xg, w_slot, wg, bg, wu, bu, wd, bd)


def _final_kernel(x1_ref, yg_ref, gt_ref, o_ref):
    yg = yg_ref[...].astype(F32)
    moe = (yg[0, 0] + yg[1, 0]) + (yg[2, 0] + yg[3, 0])
    o_ref[0] = x1_ref[0] + gt_ref[0] * moe


def _final(x1, yg, gt, tm):
    bx, tx, d = x1.shape
    per_row = gt.shape[1] != 1
    mod_spec = (pl.BlockSpec((1, tm, d), lambda b, t: (b, t, 0)) if per_row
                else pl.BlockSpec((1, 1, d), lambda b, t: (b, 0, 0)))
    row = pl.BlockSpec((1, tm, d), lambda b, t: (b, t, 0))
    return pl.pallas_call(
        _final_kernel,
        grid=(bx, tx // tm),
        in_specs=[row, pl.BlockSpec((TOP_K, 1, tm, d), lambda b, t: (0, b, t, 0)), mod_spec],
        out_specs=row,
        out_shape=jax.ShapeDtypeStruct((bx, tx, d), F32),
        compiler_params=_cparams("parallel", "arbitrary"),
    )(x1, yg, gt)


def _dispatch_plan(top_i, top_w, rank_in_seg, seg_counts, seg_tokens):
    n_assign = top_i.size
    n_tiles = -(-n_assign // MOE_TM) + N_EXPERTS
    n_slots = n_tiles * MOE_TM
    e_flat = top_i.reshape(-1)
    order = jnp.argsort(e_flat, stable=True).astype(jnp.int32)
    counts = jnp.sum(seg_counts, axis=0)
    cnt_start = jnp.cumsum(counts) - counts
    seg_base = jnp.cumsum(seg_counts, axis=0) - seg_counts
    padded = ((counts + MOE_TM - 1) // MOE_TM) * MOE_TM
    pad_end = jnp.cumsum(padded)
    pad_start = pad_end - padded
    tile_start = jnp.arange(n_tiles, dtype=jnp.int32) * MOE_TM
    tile_expert = jnp.minimum(jnp.searchsorted(pad_end, tile_start, side="right"), N_EXPERTS - 1).astype(jnp.int32)
    tile_valid = (tile_start < pad_end[-1]).astype(jnp.int32)
    e_slot = jnp.repeat(tile_expert, MOE_TM)
    rank = jnp.arange(n_slots, dtype=jnp.int32) - pad_start[e_slot]
    live = rank < counts[e_slot]
    src = order[jnp.clip(cnt_start[e_slot] + rank, 0, n_assign - 1)]
    token_of_slot = jnp.where(live, src // TOP_K, 0)
    w_of_slot = jnp.where(live, top_w.reshape(-1)[src], 0.0)
    tok_base = jnp.repeat(seg_base + pad_start[None, :], jnp.asarray(seg_tokens), axis=0,
                          total_repeat_length=sum(seg_tokens))
    hit = top_i[:, :, None] == jnp.arange(N_EXPERTS, dtype=jnp.int32)[None, None, :]
    slot_of_assign = jnp.sum(jnp.where(hit, tok_base[:, None, :], 0), axis=-1) + rank_in_seg
    return slot_of_assign, token_of_slot, w_of_slot, tile_expert, tile_valid


def _pick_tile(n, pref):
    t = min(n, pref)
    assert n % t == 0, (n, t)
    return t


def kernel(x_prompt, x_sample, cache_k, cache_v, cache_logf, state_wkv, state_shift, page_table, c_prompt,
           c_sample, w_mod, b_mod, g_mix, g_ffn, w_in, q_norm, k_norm, b_forget, rw_mu, rw_w0, rw_w2, rw_a0,
           rw_a2, rw_g2, rw_kk, rw_ka, rw_rk, lnx_w, lnx_b, w_out, w_router, b_router, w_gate, b_gate, w_up,
           b_up, w_down, b_down):
    depth = w_mod.shape[0]
    assert depth == 1
    n_p, t_p, d = x_prompt.shape
    n_s, t_s, _ = x_sample.shape
    assert d == D_MODEL and t_s * N_HEADS == QROWS
    l = 0

    w_in_l = w_in[l]
    w_r = jnp.concatenate([w_in_l[:, :3 * WIDTH], w_in_l[:, FOX_COLS:], w_in_l[:, 3 * WIDTH:FOX_COLS],
                           jnp.zeros((d, 128 - N_HEADS), F32)], axis=1).astype(BF16)
    wfz_t = w_in_l[:, 3 * WIDTH:FOX_COLS].T.astype(BF16)
    gq = (jnp.tile(q_norm[l], N_HEADS) * ATTN_SCALE).reshape(1, WIDTH)
    gk = jnp.tile(k_norm[l], N_HEADS).reshape(1, WIDTH)
    zpad = jnp.zeros((DECAY_LORA, WIDTH), F32)
    w2p = jnp.concatenate([rw_w2[l], zpad], axis=0)
    a2p = jnp.concatenate([zpad, rw_a2[l]], axis=0)
    row = lambda a: a.reshape(1, -1)

    n_c = n_p + n_s
    pad_c = (-n_c) % 8
    c_all = jnp.concatenate([c_sample, c_prompt, jnp.zeros((pad_c, d), F32)], axis=0)
    mod = _mod(c_all, w_mod[l], b_mod[l])
    mod_s = mod[:n_s].reshape(n_s, 1, N_MOD, d)
    mod_p = mod[n_s:n_s + n_p].reshape(n_p, 1, N_MOD, d)
    mod_s = jnp.broadcast_to(mod_s, (n_s, t_s, N_MOD, d)).reshape(1, n_s * t_s, N_MOD, d)
    mp = [mod_p[:, :, i] for i in range(N_MOD)]
    msm = [mod_s[:, :, i] for i in range(N_MOD)]

    xs3 = x_sample.reshape(1, n_s * t_s, d)
    tm_p = _pick_tile(t_p, 256)
    tm_s = _pick_tile(n_s * t_s, 256)

    proj_args = (row(g_mix[l]), w_r, wfz_t, gq, gk, b_forget[l])
    qh_p, kh_p, vh_p, k_p, v_p, lf_p, lft_p, rw_p = _inproj(x_prompt, mp[1], mp[0], *proj_args, tm_p)
    qh_s, _, _, k_s, v_s, lf_s, _, rw_s = _inproj(xs3, msm[1], msm[0], *proj_args, tm_s)

    ft = _cumsum_t(lft_p, _pick_tile(t_p, 512))
    tq = _pick_tile(t_p, 1024)
    of_p = _fox_prompt(qh_p, kh_p, vh_p, ft, tq, _pick_tile(tq, 512))

    q_s = jnp.transpose(qh_s[0], (1, 0, 2)).reshape(n_s, QROWS, HEAD_DIM)
    kn_s = k_s.reshape(n_s, QROWS, HEAD_DIM).astype(BF16)
    vn_s = v_s.reshape(n_s, QROWS, HEAD_DIM).astype(BF16)
    cache_kt = jnp.transpose(cache_k[l], (0, 2, 3, 1))
    cache_vt = jnp.transpose(cache_v[l], (0, 2, 3, 1))
    cache_lft = jnp.swapaxes(cache_logf[l], 1, 2)
    o_s = _fox_decode(page_table, q_s, kn_s, vn_s, lf_s.reshape(n_s, QROWS), cache_kt, cache_vt, cache_lft)
    of_s = jnp.transpose(o_s.reshape(n_s * t_s, N_HEAD_PAIRS, PAIR), (1, 0, 2))[None].astype(BF16)

    prep_w = (row(rw_mu[l]), row(rw_w0[l]), w2p, row(rw_a0[l]), a2p, rw_g2[l], row(rw_kk[l]), row(rw_ka[l]),
              row(rw_rk[l]))
    prev_p = jnp.concatenate([jnp.zeros((n_p, 1, RWKV_COLS), F32), rw_p[:, :-1]], axis=1)
    rw_s4 = rw_s.reshape(n_s, t_s, RWKV_COLS)
    prev_s = jnp.concatenate([state_shift[l][:, None, :], rw_s4[:, :-1]], axis=1).reshape(1, n_s * t_s, RWKV_COLS)
    r_p, kk_p, vv_p, w_p, a_p, b_p, g_p, bonus_p = _rwkv_prep(rw_p, prev_p, *prep_w, tm_p)
    r_s, kk_s, vv_s, w_s, a_s, b_s, g_s, bonus_s = _rwkv_prep(rw_s, prev_s, *prep_w, tm_s)
    y_p, wkv_p = _wkv_scan(r_p, kk_p, vv_p, w_p, a_p, b_p,
                           jnp.zeros((n_p, N_HEADS, HEAD_DIM, HEAD_DIM), F32), _pick_tile(t_p, 256))
    to_seq = lambda a: a.reshape(n_s, t_s, WIDTH)
    y_s, wkv_s = _wkv_scan(to_seq(r_s), to_seq(kk_s), to_seq(vv_s), to_seq(w_s), to_seq(a_s), to_seq(b_s),
                           state_wkv[l], t_s)
    y_s = y_s.reshape(1, n_s * t_s, WIDTH)

    out_w = (row(lnx_w[l]), row(lnx_b[l]), row(g_ffn[l]), w_out[l].astype(BF16), w_router[l], row(b_router[l]))
    x1_p, h2_p, ti_p, tw_p, rk_p, cnt_p = _outproj(x_prompt, of_p, y_p, bonus_p, g_p, mp[2], mp[4], mp[3],
                                                   *out_w, tm_p)
    x1_s, h2_s, ti_s, tw_s, rk_s, cnt_s = _outproj(xs3, of_s, y_s, bonus_s, g_s, msm[2], msm[4], msm[3],
                                                   *out_w, tm_s)

    n_tok_p = n_p * t_p
    both = lambda a_p, a_s, w: jnp.concatenate([a_p.reshape(n_tok_p, w), a_s.reshape(n_s * t_s, w)], axis=0)
    h2 = both(h2_p, h2_s, d)
    top_i = both(ti_p, ti_s, TOP_K)
    top_w = both(tw_p, tw_s, TOP_K)
    seg_counts = jnp.concatenate([cnt_p.reshape(n_p, N_EXPERTS), cnt_s.reshape(1, N_EXPERTS)], axis=0)
    slot_of_assign, token_of_slot, w_of_slot, tile_expert, tile_valid = _dispatch_plan(
        top_i, top_w, both(rk_p, rk_s, TOP_K), seg_counts.astype(jnp.int32), [t_p] * n_p + [n_s * t_s])
    take_rows = lambda a, idx: a.at[idx].get(mode="promise_in_bounds")
    xg = take_rows(h2, token_of_slot)
    ys = _moe_gemm(tile_expert, tile_valid, xg, w_of_slot.reshape(-1, 1),
                   w_gate[l].astype(BF16), b_gate[l].reshape(N_EXPERTS, 1, d),
                   w_up[l].astype(BF16), b_up[l].reshape(N_EXPERTS, 1, d),
                   w_down[l].astype(BF16), b_down[l].reshape(N_EXPERTS, 1, d))
    slots = slot_of_assign.reshape(-1, TOP_K).T
    yg_p = take_rows(ys, slots[:, :n_tok_p])
    yg_s = take_rows(ys, slots[:, n_tok_p:])
    y_prompt = _final(x1_p, yg_p.reshape(TOP_K, n_p, t_p, d), mp[5], tm_p)
    y_sample = _final(x1_s, yg_s.reshape(TOP_K, 1, n_s * t_s, d), msm[5], tm_s)

    heads5 = lambda a, nb, nt: a.reshape(1, nb, nt, N_HEADS, HEAD_DIM)
    return (y_prompt, y_sample.reshape(n_s, t_s, d),
            heads5(k_p, n_p, t_p), heads5(v_p, n_p, t_p), lf_p[None], wkv_p[None], rw_p[:, -1][None],
            heads5(k_s, n_s, t_s), heads5(v_s, n_s, t_s), lf_s.reshape(1, n_s, t_s, N_HEADS), wkv_s[None],
            rw_s4[:, -1][None])
```

```python
import functools

import jax
import jax.numpy as jnp
from jax import lax
from jax.experimental import pallas as pl
from jax.experimental.pallas import tpu as pltpu

F32 = jnp.float32
BF16 = jnp.bfloat16

D_MODEL = 1024
HEAD_DIM = 64
N_HEADS = 8
WIDTH = N_HEADS * HEAD_DIM
PAIR = 2 * HEAD_DIM
N_HEAD_PAIRS = N_HEADS // 2
DECAY_LORA = 64
ICLR_LORA = 64
GATE_LORA = 128
FOX_COLS = 3 * WIDTH + N_HEADS
RWKV_COLS = 3 * WIDTH + DECAY_LORA + ICLR_LORA + GATE_LORA
IN_COLS_PADDED = 3 * WIDTH + RWKV_COLS + 128
FZ_OFF = 3 * WIDTH + RWKV_COLS
ATTN_SCALE = HEAD_DIM ** -0.5
N_EXPERTS = 32
TOP_K = 4
SWIGLU_ALPHA = 1.702
SWIGLU_LIMIT = 7.0
RMS_EPS = 1e-6
GN_EPS = 64e-5
N_MOD = 6
PAGE = 128
NEG = -1e30

VMEM_LIMIT = 56 * 1024 * 1024


def _cparams(*sem):
    return pltpu.CompilerParams(dimension_semantics=sem, vmem_limit_bytes=VMEM_LIMIT)


def _dot(a, b):
    return jnp.dot(a, b, preferred_element_type=F32)


def _dot_nt(a, b):
    return lax.dot_general(a, b, (((1,), (1,)), ((), ())), preferred_element_type=F32)


def _split2(x):
    hi = x.astype(BF16)
    lo = (x - hi.astype(F32)).astype(BF16)
    return hi, lo


def _split3(x):
    hi = x.astype(BF16)
    r = x - hi.astype(F32)
    mid = r.astype(BF16)
    lo = (r - mid.astype(F32)).astype(BF16)
    return hi, mid, lo


def _dot_x3(a, b):
    ah, al = _split2(a)
    bh, bl = _split2(b)
    return _dot(ah, bh) + (_dot(ah, bl) + _dot(al, bh))


def _dot_sel2(a, sel):
    hi, lo = _split2(a)
    return _dot(hi, sel) + _dot(lo, sel)


def _dot_sel3(a, sel):
    hi, mid, lo = _split3(a)
    return _dot(hi, sel) + (_dot(mid, sel) + _dot(lo, sel))


def _sigmoid(x):
    return 1.0 / (1.0 + jnp.exp(-x))


def _softplus(x):
    return jnp.maximum(x, 0.0) + jnp.log(1.0 + jnp.exp(-jnp.abs(x)))


def _head_ones(n_heads):
    w = n_heads * HEAD_DIM
    r = lax.broadcasted_iota(jnp.int32, (w, w), 0) // HEAD_DIM
    c = lax.broadcasted_iota(jnp.int32, (w, w), 1) // HEAD_DIM
    return (r == c).astype(BF16)


def _mod_kernel(c_ref, w_ref, b_ref, o_ref):
    c = c_ref[...]
    s = c * _sigmoid(c)
    o_ref[...] = _dot_x3(s, w_ref[...]) + b_ref[...]


def _mod(c, w_mod, b_mod):
    rows, d = c.shape
    n = w_mod.shape[1]
    tn = 768
    return pl.pallas_call(
        _mod_kernel,
        grid=(n // tn,),
        in_specs=[pl.BlockSpec((rows, d), lambda j: (0, 0)),
                  pl.BlockSpec((d, tn), lambda j: (0, j)),
                  pl.BlockSpec((1, tn), lambda j: (0, j))],
        out_specs=pl.BlockSpec((rows, tn), lambda j: (0, j)),
        out_shape=jax.ShapeDtypeStruct((rows, n), F32),
        compiler_params=_cparams("arbitrary"),
    )(c, w_mod, b_mod.reshape(1, n))


def _inproj_kernel(x_ref, sc_ref, sh_ref, g_ref, w_ref, wfz_ref, gq_ref, gk_ref, bf_ref, bfc_ref, hs_ref,
                   qh_ref, kh_ref, vh_ref, k_ref, v_ref, lf_ref, lft_ref, rw_ref):
    x = x_ref[0]
    ms = jnp.mean(x * x, axis=-1, keepdims=True)
    xn = x * lax.rsqrt(ms + RMS_EPS) * g_ref[...]
    h = (xn * (1.0 + sc_ref[0]) + sh_ref[0]).astype(BF16)
    proj = _dot(h, w_ref[...])
    hs = hs_ref[...]

    def head_norm(t, gain):
        m = _dot_sel2(t * t, hs) * (1.0 / HEAD_DIM)
        return t * lax.rsqrt(m + RMS_EPS) * gain

    q = head_norm(proj[:, 0:WIDTH], gq_ref[...])
    k = head_norm(proj[:, WIDTH:2 * WIDTH], gk_ref[...])
    v = proj[:, 2 * WIDTH:3 * WIDTH]
    k_ref[0] = k
    v_ref[0] = v
    qb = q.astype(BF16)
    kb = k.astype(BF16)
    vb = v.astype(BF16)
    for hp in range(N_HEAD_PAIRS):
        sl = slice(hp * PAIR, (hp + 1) * PAIR)
        qh_ref[0, hp] = qb[:, sl]
        kh_ref[0, hp] = kb[:, sl]
        vh_ref[0, hp] = vb[:, sl]
    rw_ref[0] = proj[:, 3 * WIDTH:3 * WIDTH + RWKV_COLS]
    fz = proj[:, FZ_OFF:FZ_OFF + N_HEADS] + bf_ref[...]
    lf_ref[0] = -_softplus(-fz)
    fzt = _dot_nt(wfz_ref[...], h) + bfc_ref[...]
    lft_ref[0] = -_softplus(-fzt)


def _inproj(x3, sc, sh, g_mix, w_r, wfz_t, gq, gk, b_forget, tm):
    bx, tx, d = x3.shape
    per_row = sc.shape[1] != 1
    mod_spec = (pl.BlockSpec((1, tm, d), lambda b, t: (b, t, 0)) if per_row
                else pl.BlockSpec((1, 1, d), lambda b, t: (b, 0, 0)))
    const = lambda shape: pl.BlockSpec(shape, lambda b, t: tuple(0 for _ in shape))
    head_spec = pl.BlockSpec((1, N_HEAD_PAIRS, tm, PAIR), lambda b, t: (b, 0, t, 0))
    row_spec = lambda w: pl.BlockSpec((1, tm, w), lambda b, t: (b, t, 0))
    head_shape = jax.ShapeDtypeStruct((bx, N_HEAD_PAIRS, tx, PAIR), BF16)
    return pl.pallas_call(
        _inproj_kernel,
        grid=(bx, tx // tm),
        in_specs=[row_spec(d), mod_spec, mod_spec, const((1, d)), const((d, IN_COLS_PADDED)),
                  const((N_HEADS, d)), const((1, WIDTH)), const((1, WIDTH)), const((1, N_HEADS)),
                  const((N_HEADS, 1)), const((WIDTH, WIDTH))],
        out_specs=[head_spec, head_spec, head_spec, row_spec(WIDTH), row_spec(WIDTH), row_spec(N_HEADS),
                   pl.BlockSpec((1, N_HEADS, tm), lambda b, t: (b, 0, t)), row_spec(RWKV_COLS)],
        out_shape=[head_shape, head_shape, head_shape,
                   jax.ShapeDtypeStruct((bx, tx, WIDTH), F32), jax.ShapeDtypeStruct((bx, tx, WIDTH), F32),
                   jax.ShapeDtypeStruct((bx, tx, N_HEADS), F32), jax.ShapeDtypeStruct((bx, N_HEADS, tx), F32),
                   jax.ShapeDtypeStruct((bx, tx, RWKV_COLS), F32)],
        compiler_params=_cparams("parallel", "arbitrary"),
    )(x3, sc, sh, g_mix, w_r, wfz_t, gq, gk, b_forget.reshape(1, N_HEADS), b_forget.reshape(N_HEADS, 1),
      _head_ones(N_HEADS))


def _cumsum_kernel(l_ref, u_ref, o_ref, c_ref):
    tc = l_ref.shape[2]

    @pl.when(pl.program_id(1) == 0)
    def _():
        c_ref[...] = jnp.zeros_like(c_ref)

    f = _dot_sel3(l_ref[0], u_ref[...]) + c_ref[:, 0:1]
    o_ref[0] = f
    c_ref[...] = jnp.broadcast_to(f[:, tc - 1:tc], c_ref.shape)


def _cumsum_t(lft, tc):
    b, h, t = lft.shape
    r = lax.broadcasted_iota(jnp.int32, (tc, tc), 0)
    c = lax.broadcasted_iota(jnp.int32, (tc, tc), 1)
    upper = (r <= c).astype(BF16)
    return pl.pallas_call(
        _cumsum_kernel,
        grid=(b, t // tc),
        in_specs=[pl.BlockSpec((1, h, tc), lambda i, j: (i, 0, j)), pl.BlockSpec((tc, tc), lambda i, j: (0, 0))],
        out_specs=pl.BlockSpec((1, h, tc), lambda i, j: (i, 0, j)),
        out_shape=jax.ShapeDtypeStruct((b, h, t), F32),
        scratch_shapes=[pltpu.VMEM((h, 128), F32)],
        compiler_params=_cparams("parallel", "arbitrary"),
    )(lft, upper)


def _attn_kernel(q_ref, k_ref, v_ref, f_ref, o_ref, m_ref, l_ref, acc_ref, *, tq, tk):
    i = pl.program_id(2)
    q2 = q_ref[0, 0]
    lane = lax.broadcasted_iota(jnp.int32, q2.shape, 1)
    zero = jnp.zeros_like(q2)
    qs = (jnp.where(lane < HEAD_DIM, q2, zero), jnp.where(lane >= HEAD_DIM, q2, zero))
    q0 = pl.multiple_of(i * tq, tq)
    f_base = [f_ref[0, 0, hh:hh + 1, pl.ds(q0, tq)][:, 0:1] for hh in range(2)]
    m_ref[...] = jnp.full_like(m_ref, NEG)
    l_ref[...] = jnp.zeros_like(l_ref)
    acc_ref[...] = jnp.zeros_like(acc_ref)

    def tile(j, masked):
        k0 = pl.multiple_of(j * tk, tk)
        k2 = k_ref[0, 0, pl.ds(k0, tk), :]
        v2 = v_ref[0, 0, pl.ds(k0, tk), :]
        for hh in range(2):
            s = _dot_nt(qs[hh], k2) + (f_base[hh] - f_ref[0, 0, hh:hh + 1, pl.ds(k0, tk)])
            if masked:
                r = lax.broadcasted_iota(jnp.int32, s.shape, 0)
                c = lax.broadcasted_iota(jnp.int32, s.shape, 1)
                s = jnp.where(c + k0 <= r + q0, s, NEG)
            m_old = m_ref[hh]
            m_new = jnp.maximum(m_old, jnp.max(s, axis=1, keepdims=True))
            p = jnp.exp(s - jnp.concatenate([m_new] * (tk // 128), axis=1))
            alpha = jnp.exp(m_old - m_new)
            l_ref[hh] = alpha * l_ref[hh] + jnp.sum(p, axis=1, keepdims=True)
            acc_ref[hh] = alpha * acc_ref[hh] + _dot(p.astype(BF16), v2)
            m_ref[hh] = m_new

    def body(j, carry):
        tile(j, False)
        return carry

    n_sub = tq // tk
    lax.fori_loop(0, i * n_sub, body, 0)
    for dd in range(n_sub):
        tile(i * n_sub + dd, True)
    o_a = acc_ref[0] / l_ref[0]
    o_b = acc_ref[1] / l_ref[1]
    o_ref[0, 0] = jnp.where(lane < HEAD_DIM, o_a, o_b).astype(o_ref.dtype)


def _fox_prompt(qh, kh, vh, ft, tq, tk):
    b, hp, t, w = qh.shape
    assert tq % tk == 0 and tk % 128 == 0
    ft4 = ft.reshape(b, hp, 2, t)
    seq_spec = pl.BlockSpec((1, 1, t, w), lambda bi, hi, i: (bi, hi, 0, 0))
    tile_spec = pl.BlockSpec((1, 1, tq, w), lambda bi, hi, i: (bi, hi, i, 0))
    return pl.pallas_call(
        functools.partial(_attn_kernel, tq=tq, tk=tk),
        grid=(b, hp, t // tq),
        in_specs=[tile_spec, seq_spec, seq_spec, pl.BlockSpec((1, 1, 2, t), lambda bi, hi, i: (bi, hi, 0, 0))],
        out_specs=tile_spec,
        out_shape=jax.ShapeDtypeStruct((b, hp, t, w), BF16),
        scratch_shapes=[pltpu.VMEM((2, tq, 128), F32), pltpu.VMEM((2, tq, 128), F32), pltpu.VMEM((2, tq, w), F32)],
        compiler_params=_cparams("parallel", "parallel", "arbitrary"),
    )(qh, kh, vh, ft4)


PAGES_PER_STEP = 8
QROWS = 4 * N_HEADS


def _decode_kernel(pt_ref, q_ref, kn_ref, vn_ref, lfn_ref, cnew_ref, ustrict_ref, fold_ref, *refs, n_steps):
    del pt_ref
    npg = PAGES_PER_STEP
    k_refs = refs[0:npg]
    v_refs = refs[npg:2 * npg]
    lf_refs = refs[2 * npg:3 * npg]
    o_ref, m_ref, l_ref, acc_ref, accn_ref, carry_ref = refs[3 * npg:]
    j = pl.program_id(1)
    q = q_ref[0]
    r = lax.broadcasted_iota(jnp.int32, (QROWS, WIDTH), 0)
    c = lax.broadcasted_iota(jnp.int32, (QROWS, WIDTH), 1)
    own_head = (r % N_HEADS) == (c // HEAD_DIM)
    q_bd = jnp.where(own_head, jnp.concatenate([q] * N_HEADS, axis=1), jnp.zeros((), q.dtype))

    def softmax_step(s):
        m_old = m_ref[...]
        m_new = jnp.maximum(m_old, jnp.max(s, axis=1, keepdims=True))
        p = jnp.exp(s - m_new)
        alpha = jnp.exp(m_old - m_new)
        l_ref[...] = alpha * l_ref[...] + jnp.sum(p, axis=1, keepdims=True)
        m_ref[...] = m_new
        return p.astype(BF16), alpha

    @pl.when(j == 0)
    def _():
        m_ref[...] = jnp.full_like(m_ref, NEG)
        l_ref[...] = jnp.zeros_like(l_ref)
        acc_ref[...] = jnp.zeros_like(acc_ref)
        carry_ref[...] = jnp.zeros_like(carry_ref)
        s = _dot_nt(q, kn_ref[0])
        bias = _dot_sel3(lfn_ref[0], cnew_ref[...])[0:1, :]
        rr = lax.broadcasted_iota(jnp.int32, s.shape, 0)
        cc = lax.broadcasted_iota(jnp.int32, s.shape, 1)
        ok = ((rr % N_HEADS) == (cc % N_HEADS)) & ((cc // N_HEADS) <= (rr // N_HEADS))
        p, _ = softmax_step(jnp.where(ok, s + bias, NEG))
        accn_ref[...] = _dot(p, vn_ref[0])

    ustrict = ustrict_ref[...]
    carry = carry_ref[...]
    scores = []
    for u in range(npg):
        kt = k_refs[u][...].reshape(WIDTH, PAGE).astype(BF16)
        lft = lf_refs[u][...]
        a = jnp.broadcast_to(lft[None], (4, N_HEADS, PAGE)).reshape(QROWS, PAGE)
        hi, mid, lo = _split3(a)
        bx = _dot(jnp.concatenate([hi, mid, lo], axis=0), ustrict)
        bias = bx[0:QROWS] + (bx[QROWS:2 * QROWS] + bx[2 * QROWS:3 * QROWS]) + carry
        carry = carry + jnp.sum(a, axis=1, keepdims=True)
        scores.append(_dot(q_bd, kt) + bias)
    carry_ref[...] = carry
    p, alpha = softmax_step(jnp.concatenate(scores, axis=1))
    acc = alpha * acc_ref[...]
    for u in range(npg):
        vt = v_refs[u][...].reshape(WIDTH, PAGE).astype(BF16)
        acc = acc + _dot_nt(p[:, u * PAGE:(u + 1) * PAGE], vt)
    acc_ref[...] = acc
    accn_ref[...] = alpha * accn_ref[...]

    @pl.when(j == n_steps - 1)
    def _():
        own = jnp.where(own_head, acc_ref[...], 0.0)
        o_ref[0] = (_dot_sel3(own, fold_ref[...]) + accn_ref[...]) / l_ref[...]


def _fox_decode(page_table, q, k_new, v_new, lf_new, cache_kt, cache_vt, cache_lft):
    nb, n_pages = page_table.shape
    npg = PAGES_PER_STEP
    n_steps = n_pages // npg
    ustrict = (lax.broadcasted_iota(jnp.int32, (PAGE, PAGE), 0)
               > lax.broadcasted_iota(jnp.int32, (PAGE, PAGE), 1)).astype(BF16)
    ri = lax.broadcasted_iota(jnp.int32, (QROWS, QROWS), 0)
    ci = lax.broadcasted_iota(jnp.int32, (QROWS, QROWS), 1)
    cnew = -(((ri % N_HEADS) == (ci % N_HEADS)) & ((ri // N_HEADS) <= (ci // N_HEADS))).astype(BF16)
    fold = (lax.broadcasted_iota(jnp.int32, (WIDTH, HEAD_DIM), 0) % HEAD_DIM
            == lax.broadcasted_iota(jnp.int32, (WIDTH, HEAD_DIM), 1)).astype(BF16)

    def page_idx(u):
        return lambda b, j, pt: (pt[b, n_pages - 1 - (j * npg + u)], 0, 0, 0)

    def page_idx3(u):
        return lambda b, j, pt: (pt[b, n_pages - 1 - (j * npg + u)], 0, 0)

    per_b = lambda shape: pl.BlockSpec((1,) + shape, lambda b, j, pt: (b, 0, 0))
    const = lambda shape: pl.BlockSpec(shape, lambda b, j, pt: (0, 0))
    in_specs = ([per_b((QROWS, HEAD_DIM)), per_b((QROWS, HEAD_DIM)), per_b((QROWS, HEAD_DIM)),
                 per_b((8, QROWS)), const((QROWS, QROWS)), const((PAGE, PAGE)), const((WIDTH, HEAD_DIM))]
                + [pl.BlockSpec((None, N_HEADS, HEAD_DIM, PAGE), page_idx(u)) for u in range(npg)]
                + [pl.BlockSpec((None, N_HEADS, HEAD_DIM, PAGE), page_idx(u)) for u in range(npg)]
                + [pl.BlockSpec((None, N_HEADS, PAGE), page_idx3(u)) for u in range(npg)])
    grid_spec = pltpu.PrefetchScalarGridSpec(
        num_scalar_prefetch=1,
        grid=(nb, n_steps),
        in_specs=in_specs,
        out_specs=pl.BlockSpec((1, QROWS, HEAD_DIM), lambda b, j, pt: (b, 0, 0)),
        scratch_shapes=[pltpu.VMEM((QROWS, 1), F32), pltpu.VMEM((QROWS, 1), F32),
                        pltpu.VMEM((QROWS, WIDTH), F32), pltpu.VMEM((QROWS, HEAD_DIM), F32),
                        pltpu.VMEM((QROWS, 1), F32)],
    )
    lfn = jnp.broadcast_to(lf_new.reshape(nb, 1, QROWS), (nb, 8, QROWS))
    return pl.pallas_call(
        functools.partial(_decode_kernel, n_steps=n_steps),
        grid_spec=grid_spec,
        out_shape=jax.ShapeDtypeStruct((nb, QROWS, HEAD_DIM), F32),
        compiler_params=_cparams("parallel", "arbitrary"),
    )(page_table, q, k_new, v_new, lfn, cnew, ustrict, fold,
      *([cache_kt] * npg), *([cache_vt] * npg), *([cache_lft] * npg))


def _rwkv_prep_kernel(p_ref, prev_ref, first_ref, mu_ref, w0_ref, w2_ref, a0_ref, a2_ref, g2_ref, kk_ref, ka_ref,
                      rk_ref, hs_ref, r_ref, k_ref, v_ref, w_ref, a_ref, b_ref, g_ref, bonus_ref, *, shift_rows):
    p = p_ref[0]
    if shift_rows:
        carry = jnp.where(pl.program_id(1) == 0, first_ref[0], prev_ref[0, 7:8, :])
        row0 = lax.broadcasted_iota(jnp.int32, p.shape, 0) == 0
        prev = jnp.where(row0, carry, pltpu.roll(p, 1, 0))
    else:
        prev = prev_ref[0]
    z = p + (prev - p) * mu_ref[...]
    r = z[:, 0:WIDTH]
    k = z[:, WIDTH:2 * WIDTH]
    v = z[:, 2 * WIDTH:3 * WIDTH]
    lora = z[:, 3 * WIDTH:3 * WIDTH + DECAY_LORA + ICLR_LORA]
    gd = z[:, 3 * WIDTH + DECAY_LORA + ICLR_LORA:RWKV_COLS]
    hs = hs_ref[...]
    w = -_softplus(-(w0_ref[...] + _dot_x3(jnp.tanh(lora), w2_ref[...]))) - 0.5
    decay = jnp.exp(-jnp.exp(w))
    a = _sigmoid(a0_ref[...] + _dot_x3(lora, a2_ref[...]))
    g = _dot_x3(_sigmoid(gd), g2_ref[...])
    kk = k * kk_ref[...]
    norm = jnp.sqrt(_dot_sel3(kk * kk, hs))
    kk = kk / jnp.maximum(norm, 1e-12)
    k = k * (1.0 + (a - 1.0) * ka_ref[...])
    bonus = _dot_sel3(r * k * rk_ref[...], hs) * v
    r_ref[0] = r
    k_ref[0] = k
    v_ref[0] = v
    w_ref[0] = decay
    a_ref[0] = -kk
    b_ref[0] = kk * a
    g_ref[0] = g
    bonus_ref[0] = bonus


def _rwkv_prep(p_rw, prev, first, mu, w0, w2p, a0, a2p, g2, kk, ka, rk, tm):
    bx, tx, _ = p_rw.shape
    row = lambda w: pl.BlockSpec((1, tm, w), lambda b, t: (b, t, 0))
    const = lambda shape: pl.BlockSpec(shape, lambda b, t: tuple(0 for _ in shape))
    out = jax.ShapeDtypeStruct((bx, tx, WIDTH), F32)
    shift_rows = prev is None
    if shift_rows:
        prev = p_rw
        prev_spec = pl.BlockSpec((1, 8, RWKV_COLS), lambda b, t: (b, jnp.maximum(t * (tm // 8) - 1, 0), 0))
    else:
        prev_spec = row(RWKV_COLS)
    return pl.pallas_call(
        functools.partial(_rwkv_prep_kernel, shift_rows=shift_rows),
        grid=(bx, tx // tm),
        in_specs=[row(RWKV_COLS), prev_spec, pl.BlockSpec((1, 1, RWKV_COLS), lambda b, t: (b, 0, 0)),
                  const((1, RWKV_COLS)), const((1, WIDTH)),
                  const((DECAY_LORA + ICLR_LORA, WIDTH)), const((1, WIDTH)), const((DECAY_LORA + ICLR_LORA, WIDTH)),
                  const((GATE_LORA, WIDTH)), const((1, WIDTH)), const((1, WIDTH)), const((1, WIDTH)),
                  const((WIDTH, WIDTH))],
        out_specs=[row(WIDTH)] * 8,
        out_shape=[out] * 8,
        compiler_params=_cparams("parallel", "arbitrary"),
    )(p_rw, prev, first, mu, w0, w2p, a0, a2p, g2, kk, ka, rk, _head_ones(N_HEADS))


WKV_NB = 2
WKV_HG = 4
WKV_LANES = WKV_HG * HEAD_DIM
WKV_GROUPS = N_HEADS // WKV_HG
WKV_ROWS = WKV_NB * WKV_GROUPS * HEAD_DIM


def _wkv_kernel(r_ref, k_ref, v_ref, w_ref, a_ref, b_ref, s0_ref, ones_ref, y_ref, sT_ref, st_ref, *, tc, n_t):
    ti = pl.program_id(1)
    blocks = [(bi, gg) for bi in range(WKV_NB) for gg in range(WKV_GROUPS)]

    @pl.when(ti == 0)
    def _():
        for n, (bi, gg) in enumerate(blocks):
            st_ref[n * HEAD_DIM:(n + 1) * HEAD_DIM, :] = jnp.concatenate(
                [s0_ref[bi, gg * WKV_HG + q] for q in range(WKV_HG)], axis=1)

    rows = lax.broadcasted_iota(jnp.int32, (WKV_ROWS, WKV_LANES), 0)
    lanes = lax.broadcasted_iota(jnp.int32, (WKV_ROWS, WKV_LANES), 1)
    eye = (rows % HEAD_DIM) == (lanes % HEAD_DIM)
    eye_f = eye.astype(F32)
    ones = ones_ref[...]
    grp = min(tc, 8)

    def bcast(xs, u):
        return jnp.concatenate(
            [jnp.broadcast_to(xs[bi][u:u + 1, gg * WKV_LANES:(gg + 1) * WKV_LANES], (HEAD_DIM, WKV_LANES))
             for bi, gg in blocks], axis=0)

    def group(gi, carry):
        t0 = pl.multiple_of(gi * grp, grp)
        load = lambda ref: [ref[bi, pl.ds(t0, grp), :] for bi in range(WKV_NB)]
        r8, k8, v8, w8, a8, b8 = (load(ref) for ref in (r_ref, k_ref, v_ref, w_ref, a_ref, b_ref))
        vdiag = lambda u: jnp.where(eye, bcast(v8, u), 0.0).astype(BF16)
        s = st_ref[...]
        ys = []

        def y_rows(yb):
            yb = yb * eye_f
            return [jnp.sum(yb[n * HEAD_DIM:(n + 1) * HEAD_DIM], axis=0, keepdims=True) for n in range(len(blocks))]

        vcol = _dot(vdiag(0), ones)
        ybf = None
        for u in range(grp):
            sa = _dot((s * bcast(a8, u)).astype(BF16), ones)
            side = [vdiag(u + 1)] if u + 1 < grp else []
            if ybf is not None:
                side.append(ybf)
            misc = _dot(jnp.concatenate(side, axis=0), ones) if side else None
            s = s * bcast(w8, u) + sa * bcast(b8, u) + vcol * bcast(k8, u)
            if misc is not None:
                off = 0
                if u + 1 < grp:
                    vcol = misc[0:WKV_ROWS]
                    off = WKV_ROWS
                if ybf is not None:
                    ys.append(y_rows(misc[off:off + WKV_ROWS]))
            ybf = (s * bcast(r8, u)).astype(BF16)
        ys.append(y_rows(_dot(ybf, ones)))
        st_ref[...] = s
        for bi in range(WKV_NB):
            y_ref[bi, pl.ds(t0, grp), :] = jnp.concatenate(
                [jnp.concatenate([ys[u][bi * WKV_GROUPS + gg] for gg in range(WKV_GROUPS)], axis=1)
                 for u in range(grp)], axis=0)
        return carry

    lax.fori_loop(0, tc // grp, group, 0)

    @pl.when(ti == n_t - 1)
    def _():
        for n, (bi, gg) in enumerate(blocks):
            for q in range(WKV_HG):
                sT_ref[bi, gg * WKV_HG + q] = st_ref[n * HEAD_DIM:(n + 1) * HEAD_DIM, q * HEAD_DIM:(q + 1) * HEAD_DIM]


def _wkv_scan(r, k, v, w, a, b, s0, tc):
    bx, tx, _ = r.shape
    assert bx % WKV_NB == 0
    n_t = tx // tc
    seq = pl.BlockSpec((WKV_NB, tc, WIDTH), lambda i, t: (i, t, 0))
    st = pl.BlockSpec((WKV_NB, N_HEADS, HEAD_DIM, HEAD_DIM), lambda i, t: (i, 0, 0, 0))
    ri = lax.broadcasted_iota(jnp.int32, (WKV_LANES, WKV_LANES), 0) // HEAD_DIM
    ci = lax.broadcasted_iota(jnp.int32, (WKV_LANES, WKV_LANES), 1) // HEAD_DIM
    ones = (ri == ci).astype(BF16)
    return pl.pallas_call(
        functools.partial(_wkv_kernel, tc=tc, n_t=n_t),
        grid=(bx // WKV_NB, n_t),
        in_specs=[seq] * 6 + [st, pl.BlockSpec((WKV_LANES, WKV_LANES), lambda i, t: (0, 0))],
        out_specs=[seq, st],
        out_shape=[jax.ShapeDtypeStruct((bx, tx, WIDTH), F32),
                   jax.ShapeDtypeStruct((bx, N_HEADS, HEAD_DIM, HEAD_DIM), F32)],
        scratch_shapes=[pltpu.VMEM((WKV_ROWS, WKV_LANES), F32)],
        compiler_params=_cparams("parallel", "arbitrary"),
    )(r, k, v, w, a, b, s0, ones)


def _outproj_kernel(x_ref, of_ref, y_ref, bonus_ref, g_ref, gt_ref, sc_ref, sh_ref, lw_ref, lb_ref, gf_ref,
                    wo_ref, wr_ref, br_ref, hs_ref, lt_ref, x1_ref, h2_ref, ti_ref, tw_ref, rk_ref, cnt_ref, run_ref):
    @pl.when(pl.program_id(1) == 0)
    def _():
        run_ref[...] = jnp.zeros_like(run_ref)

    hs = hs_ref[...]
    y = y_ref[0]
    mu = _dot_sel3(y, hs) * (1.0 / HEAD_DIM)
    yc = y - mu
    var = _dot_sel3(yc * yc, hs) * (1.0 / HEAD_DIM)
    o_rw = ((yc * lax.rsqrt(var + GN_EPS)) * lw_ref[...] + lb_ref[...] + bonus_ref[0]) * g_ref[0]
    mixed = _dot(o_rw.astype(BF16), wo_ref[WIDTH:2 * WIDTH, :])
    for hp in range(N_HEAD_PAIRS):
        mixed = mixed + _dot(of_ref[0, hp], wo_ref[hp * PAIR:(hp + 1) * PAIR, :])
    x1 = x_ref[0] + gt_ref[0] * mixed
    x1_ref[0] = x1
    ms = jnp.mean(x1 * x1, axis=-1, keepdims=True)
    h2 = (x1 * lax.rsqrt(ms + RMS_EPS) * gf_ref[...]) * (1.0 + sc_ref[0]) + sh_ref[0]
    h2_ref[0] = h2.astype(BF16)
    logits = _dot_x3(h2, wr_ref[...]) + br_ref[...]
    lane = lax.broadcasted_iota(jnp.int32, logits.shape, 1).astype(F32)
    vals, idxs = [], []
    for _ in range(TOP_K):
        m = jnp.max(logits, axis=1, keepdims=True)
        idx = jnp.min(jnp.where(logits == m, lane, float(N_EXPERTS)), axis=1, keepdims=True)
        vals.append(m)
        idxs.append(idx)
        logits = jnp.where(lane == idx, -jnp.inf, logits)
    e = [jnp.exp(vv - vals[0]) for vv in vals]
    tot = e[0] + e[1] + e[2] + e[3]
    tw_ref[0] = jnp.concatenate(e, axis=1) / tot
    ti_ref[0] = jnp.concatenate(idxs, axis=1).astype(jnp.int32)
    chosen = [lane == idx for idx in idxs]
    multi_hot = jnp.where(chosen[0] | chosen[1] | chosen[2] | chosen[3], 1.0, 0.0)
    before = _dot(lt_ref[...], multi_hot.astype(BF16)) + run_ref[...]
    ranks = [jnp.sum(jnp.where(c, before, 0.0), axis=1, keepdims=True) for c in chosen]
    rk_ref[0] = jnp.concatenate(ranks, axis=1).astype(jnp.int32)
    run_ref[...] = run_ref[...] + jnp.sum(multi_hot, axis=0, keepdims=True)
    cnt_ref[0] = run_ref[...]


def _outproj(x3, of, y, bonus, g, gt, sc, sh, lnx_w, lnx_b, g_ffn, w_out_b, w_router, b_router, tm):
    bx, tx, d = x3.shape
    per_row = gt.shape[1] != 1
    mod_spec = (pl.BlockSpec((1, tm, d), lambda b, t: (b, t, 0)) if per_row
                else pl.BlockSpec((1, 1, d), lambda b, t: (b, 0, 0)))
    row = lambda w: pl.BlockSpec((1, tm, w), lambda b, t: (b, t, 0))
    const = lambda shape: pl.BlockSpec(shape, lambda b, t: tuple(0 for _ in shape))
    lower = (lax.broadcasted_iota(jnp.int32, (tm, tm), 0) > lax.broadcasted_iota(jnp.int32, (tm, tm), 1)).astype(BF16)
    return pl.pallas_call(
        _outproj_kernel,
        grid=(bx, tx // tm),
        in_specs=[row(d), pl.BlockSpec((1, N_HEAD_PAIRS, tm, PAIR), lambda b, t: (b, 0, t, 0)),
                  row(WIDTH), row(WIDTH), row(WIDTH), mod_spec, mod_spec, mod_spec,
                  const((1, WIDTH)), const((1, WIDTH)), const((1, d)), const((2 * WIDTH, d)),
                  const((d, N_EXPERTS)), const((1, N_EXPERTS)), const((WIDTH, WIDTH)), const((tm, tm))],
        out_specs=[row(d), row(d), row(TOP_K), row(TOP_K), row(TOP_K),
                   pl.BlockSpec((1, 1, N_EXPERTS), lambda b, t: (b, 0, 0))],
        out_shape=[jax.ShapeDtypeStruct((bx, tx, d), F32), jax.ShapeDtypeStruct((bx, tx, d), BF16),
                   jax.ShapeDtypeStruct((bx, tx, TOP_K), jnp.int32), jax.ShapeDtypeStruct((bx, tx, TOP_K), F32),
                   jax.ShapeDtypeStruct((bx, tx, TOP_K), jnp.int32), jax.ShapeDtypeStruct((bx, 1, N_EXPERTS), F32)],
        scratch_shapes=[pltpu.VMEM((1, N_EXPERTS), F32)],
        compiler_params=_cparams("parallel", "arbitrary"),
    )(x3, of, y, bonus, g, gt, sc, sh, lnx_w, lnx_b, g_ffn, w_out_b, w_router, b_router, _head_ones(N_HEADS), lower)


MOE_TM = 256


def _moe_kernel(te_ref, tf_ref, x_ref, wg_ref, bg_ref, wu_ref, bu_ref, wd_ref, bd_ref, o_ref, wb_ref):
    del te_ref
    i = pl.program_id(0)

    @pl.when(tf_ref[i] == 2)
    def _():
        wb_ref[0] = wg_ref[0].astype(BF16)
        wb_ref[1] = wu_ref[0].astype(BF16)
        wb_ref[2] = wd_ref[0].astype(BF16)

    @pl.when(tf_ref[i] != 0)
    def _():
        x = x_ref[...]
        glu = jnp.minimum(_dot(x, wb_ref[0]) + bg_ref[0], SWIGLU_LIMIT)
        lin = jnp.clip(_dot(x, wb_ref[1]) + bu_ref[0], -SWIGLU_LIMIT, SWIGLU_LIMIT)
        act = glu * _sigmoid(SWIGLU_ALPHA * glu) * (lin + 1.0)
        o_ref[...] = (_dot(act.astype(BF16), wb_ref[2]) + bd_ref[0]).astype(o_ref.dtype)

    @pl.when(tf_ref[i] == 0)
    def _():
        o_ref[...] = jnp.zeros_like(o_ref)


def _moe_gemm(tile_expert, tile_flag, xg, wg, bg, wu, bu, wd, bd):
    n_slots, d = xg.shape
    n_tiles = n_slots // MOE_TM
    wspec = pl.BlockSpec((1, d, d), lambda i, te, tf: (te[i], 0, 0))
    bspec = pl.BlockSpec((1, 1, d), lambda i, te, tf: (te[i], 0, 0))
    grid_spec = pltpu.PrefetchScalarGridSpec(
        num_scalar_prefetch=2,
        grid=(n_tiles,),
        in_specs=[pl.BlockSpec((MOE_TM, d), lambda i, te, tf: (i, 0)), wspec, bspec, wspec, bspec, wspec, bspec],
        out_specs=pl.BlockSpec((MOE_TM, d), lambda i, te, tf: (i, 0)),
        scratch_shapes=[pltpu.VMEM((3, d, d), BF16)],
    )
    return pl.pallas_call(
        _moe_kernel,
        grid_spec=grid_spec,
        out_shape=jax.ShapeDtypeStruct((n_slots, d), BF16),
        compiler_params=_cparams("arbitrary"),
    )(tile_expert, tile_flag, xg, wg, bg, wu, bu, wd, bd)


def _final_kernel(x1_ref, yg_ref, tw_ref, gt_ref, o_ref):
    tw = tw_ref[0]
    moe = tw[:, 0:1] * yg_ref[0, 0].astype(F32)
    for kk in range(1, TOP_K):
        moe = moe + tw[:, kk:kk + 1] * yg_ref[kk, 0].astype(F32)
    o_ref[0] = x1_ref[0] + gt_ref[0] * moe


def _final(x1, yg, tw, gt, tm):
    bx, tx, d = x1.shape
    per_row = gt.shape[1] != 1
    mod_spec = (pl.BlockSpec((1, tm, d), lambda b, t: (b, t, 0)) if per_row
                else pl.BlockSpec((1, 1, d), lambda b, t: (b, 0, 0)))
    row = pl.BlockSpec((1, tm, d), lambda b, t: (b, t, 0))
    return pl.pallas_call(
        _final_kernel,
        grid=(bx, tx // tm),
        in_specs=[row, pl.BlockSpec((TOP_K, 1, tm, d), lambda b, t: (0, b, t, 0)),
                  pl.BlockSpec((1, tm, TOP_K), lambda b, t: (b, t, 0)), mod_spec],
        out_specs=row,
        out_shape=jax.ShapeDtypeStruct((bx, tx, d), F32),
        compiler_params=_cparams("parallel", "arbitrary"),
    )(x1, yg, tw, gt)


def _dispatch_plan(top_i, rank_in_seg, seg_counts, seg_tokens):
    n_assign = top_i.size
    n_tiles = -(-n_assign // MOE_TM) + N_EXPERTS
    n_slots = n_tiles * MOE_TM
    e_flat = top_i.reshape(-1)
    order = jnp.argsort(e_flat, stable=True).astype(jnp.int32)
    counts = jnp.sum(seg_counts, axis=0)
    cnt_start = jnp.cumsum(counts) - counts
    seg_base = jnp.cumsum(seg_counts, axis=0) - seg_counts
    padded = ((counts + MOE_TM - 1) // MOE_TM) * MOE_TM
    pad_end = jnp.cumsum(padded)
    pad_start = pad_end - padded
    experts = jnp.arange(N_EXPERTS, dtype=jnp.int32)
    tile_start = jnp.arange(n_tiles, dtype=jnp.int32) * MOE_TM
    tile_expert = jnp.minimum(jnp.sum((tile_start[:, None] >= pad_end[None, :]).astype(jnp.int32), axis=1),
                              N_EXPERTS - 1)
    tile_live = tile_start < pad_end[-1]
    is_first = jnp.concatenate([jnp.ones((1,), bool), tile_expert[1:] != tile_expert[:-1]])
    tile_flag = jnp.where(tile_live, jnp.where(is_first, 2, 1), 0).astype(jnp.int32)
    of_tile = lambda tab: jnp.sum(jnp.where(tile_expert[:, None] == experts[None, :], tab[None, :], 0), axis=1)
    rank = (tile_start - of_tile(pad_start))[:, None] + jnp.arange(MOE_TM, dtype=jnp.int32)[None, :]
    live = rank < of_tile(counts)[:, None]
    src = order.at[jnp.clip(of_tile(cnt_start)[:, None] + rank, 0, n_assign - 1)].get(mode="promise_in_bounds")
    token_of_slot = jnp.where(live, src // TOP_K, 0).reshape(n_slots)
    tok_base = jnp.repeat(seg_base + pad_start[None, :], jnp.asarray(seg_tokens), axis=0,
                          total_repeat_length=sum(seg_tokens))
    hit = top_i[:, :, None] == experts[None, None, :]
    slot_of_assign = jnp.sum(jnp.where(hit, tok_base[:, None, :], 0), axis=-1) + rank_in_seg
    return slot_of_assign, token_of_slot, tile_expert, tile_flag


def _pick_tile(n, pref):
    t = min(n, pref)
    assert n % t == 0, (n, t)
    return t


def kernel(x_prompt, x_sample, cache_k, cache_v, cache_logf, state_wkv, state_shift, page_table, c_prompt,
           c_sample, w_mod, b_mod, g_mix, g_ffn, w_in, q_norm, k_norm, b_forget, rw_mu, rw_w0, rw_w2, rw_a0,
           rw_a2, rw_g2, rw_kk, rw_ka, rw_rk, lnx_w, lnx_b, w_out, w_router, b_router, w_gate, b_gate, w_up,
           b_up, w_down, b_down):
    depth = w_mod.shape[0]
    assert depth == 1
    n_p, t_p, d = x_prompt.shape
    n_s, t_s, _ = x_sample.shape
    assert d == D_MODEL and t_s * N_HEADS == QROWS
    l = 0

    w_in_l = w_in[l]
    w_r = jnp.concatenate([w_in_l[:, :3 * WIDTH], w_in_l[:, FOX_COLS:], w_in_l[:, 3 * WIDTH:FOX_COLS],
                           jnp.zeros((d, 128 - N_HEADS), F32)], axis=1).astype(BF16)
    wfz_t = w_in_l[:, 3 * WIDTH:FOX_COLS].T.astype(BF16)
    gq = (jnp.tile(q_norm[l], N_HEADS) * ATTN_SCALE).reshape(1, WIDTH)
    gk = jnp.tile(k_norm[l], N_HEADS).reshape(1, WIDTH)
    zpad = jnp.zeros((DECAY_LORA, WIDTH), F32)
    w2p = jnp.concatenate([rw_w2[l], zpad], axis=0)
    a2p = jnp.concatenate([zpad, rw_a2[l]], axis=0)
    row = lambda a: a.reshape(1, -1)

    n_c = n_p + n_s
    pad_c = (-n_c) % 8
    c_all = jnp.concatenate([c_sample, c_prompt, jnp.zeros((pad_c, d), F32)], axis=0)
    mod = _mod(c_all, w_mod[l], b_mod[l])
    mod_s = mod[:n_s].reshape(n_s, 1, N_MOD, d)
    mod_p = mod[n_s:n_s + n_p].reshape(n_p, 1, N_MOD, d)
    mod_s = jnp.broadcast_to(mod_s, (n_s, t_s, N_MOD, d)).reshape(1, n_s * t_s, N_MOD, d)
    mp = [mod_p[:, :, i] for i in range(N_MOD)]
    msm = [mod_s[:, :, i] for i in range(N_MOD)]

    xs3 = x_sample.reshape(1, n_s * t_s, d)
    tm_p = _pick_tile(t_p, 256)
    tm_s = _pick_tile(n_s * t_s, 256)

    proj_args = (row(g_mix[l]), w_r, wfz_t, gq, gk, b_forget[l])
    qh_p, kh_p, vh_p, k_p, v_p, lf_p, lft_p, rw_p = _inproj(x_prompt, mp[1], mp[0], *proj_args, tm_p)
    qh_s, _, _, k_s, v_s, lf_s, _, rw_s = _inproj(xs3, msm[1], msm[0], *proj_args, tm_s)

    ft = _cumsum_t(lft_p, _pick_tile(t_p, 512))
    tq = _pick_tile(t_p, 1024)
    of_p = _fox_prompt(qh_p, kh_p, vh_p, ft, tq, _pick_tile(tq, 512))

    q_s = jnp.transpose(qh_s[0], (1, 0, 2)).reshape(n_s, QROWS, HEAD_DIM)
    kn_s = k_s.reshape(n_s, QROWS, HEAD_DIM).astype(BF16)
    vn_s = v_s.reshape(n_s, QROWS, HEAD_DIM).astype(BF16)
    cache_kt = jnp.transpose(cache_k[l], (0, 2, 3, 1))
    cache_vt = jnp.transpose(cache_v[l], (0, 2, 3, 1))
    cache_lft = jnp.swapaxes(cache_logf[l], 1, 2)
    o_s = _fox_decode(page_table, q_s, kn_s, vn_s, lf_s.reshape(n_s, QROWS), cache_kt, cache_vt, cache_lft)
    of_s = jnp.transpose(o_s.reshape(n_s * t_s, N_HEAD_PAIRS, PAIR), (1, 0, 2))[None].astype(BF16)

    prep_w = (row(rw_mu[l]), row(rw_w0[l]), w2p, row(rw_a0[l]), a2p, rw_g2[l], row(rw_kk[l]), row(rw_ka[l]),
              row(rw_rk[l]))
    rw_s4 = rw_s.reshape(n_s, t_s, RWKV_COLS)
    prev_s = jnp.concatenate([state_shift[l][:, None, :], rw_s4[:, :-1]], axis=1).reshape(1, n_s * t_s, RWKV_COLS)
    r_p, kk_p, vv_p, w_p, a_p, b_p, g_p, bonus_p = _rwkv_prep(
        rw_p, None, jnp.zeros((n_p, 1, RWKV_COLS), F32), *prep_w, tm_p)
    r_s, kk_s, vv_s, w_s, a_s, b_s, g_s, bonus_s = _rwkv_prep(
        rw_s, prev_s, jnp.zeros((1, 1, RWKV_COLS), F32), *prep_w, tm_s)
    y_p, wkv_p = _wkv_scan(r_p, kk_p, vv_p, w_p, a_p, b_p,
                           jnp.zeros((n_p, N_HEADS, HEAD_DIM, HEAD_DIM), F32), _pick_tile(t_p, 256))
    to_seq = lambda a: a.reshape(n_s, t_s, WIDTH)
    y_s, wkv_s = _wkv_scan(to_seq(r_s), to_seq(kk_s), to_seq(vv_s), to_seq(w_s), to_seq(a_s), to_seq(b_s),
                           state_wkv[l], t_s)
    y_s = y_s.reshape(1, n_s * t_s, WIDTH)

    out_w = (row(lnx_w[l]), row(lnx_b[l]), row(g_ffn[l]), w_out[l].astype(BF16), w_router[l], row(b_router[l]))
    x1_p, h2_p, ti_p, tw_p, rk_p, cnt_p = _outproj(x_prompt, of_p, y_p, bonus_p, g_p, mp[2], mp[4], mp[3],
                                                   *out_w, tm_p)
    x1_s, h2_s, ti_s, tw_s, rk_s, cnt_s = _outproj(xs3, of_s, y_s, bonus_s, g_s, msm[2], msm[4], msm[3],
                                                   *out_w, tm_s)

    n_tok_p = n_p * t_p
    both = lambda a_p, a_s, w: jnp.concatenate([a_p.reshape(n_tok_p, w), a_s.reshape(n_s * t_s, w)], axis=0)
    h2 = both(h2_p, h2_s, d)
    top_i = both(ti_p, ti_s, TOP_K)
    seg_counts = jnp.concatenate([cnt_p.reshape(n_p, N_EXPERTS), cnt_s.reshape(1, N_EXPERTS)], axis=0)
    slot_of_assign, token_of_slot, tile_expert, tile_flag = _dispatch_plan(
        top_i, both(rk_p, rk_s, TOP_K), seg_counts.astype(jnp.int32), [t_p] * n_p + [n_s * t_s])
    take_rows = lambda a, idx: a.at[idx].get(mode="promise_in_bounds")
    xg = take_rows(h2, token_of_slot)
    ys = _moe_gemm(tile_expert, tile_flag, xg,
                   w_gate[l], b_gate[l].reshape(N_EXPERTS, 1, d), w_up[l], b_up[l].reshape(N_EXPERTS, 1, d),
                   w_down[l], b_down[l].reshape(N_EXPERTS, 1, d))
    slots = slot_of_assign.T
    yg_p = take_rows(ys, slots[:, :n_tok_p])
    yg_s = take_rows(ys, slots[:, n_tok_p:])
    y_prompt = _final(x1_p, yg_p.reshape(TOP_K, n_p, t_p, d), tw_p, mp[5], tm_p)
    y_sample = _final(x1_s, yg_s.reshape(TOP_K, 1, n_s * t_s, d), tw_s, msm[5], tm_s)

    heads5 = lambda a, nb, nt: a.reshape(1, nb, nt, N_HEADS, HEAD_DIM)
    return (y_prompt, y_sample.reshape(n_s, t_s, d),
            heads5(k_p, n_p, t_p), heads5(v_p, n_p, t_p), lf_p[None], wkv_p[None], rw_p[:, -1][None],
            heads5(k_s, n_s, t_s), heads5(v_s, n_s, t_s), lf_s.reshape(1, n_s, t_s, N_HEADS), wkv_s[None],
            rw_s4[:, -1][None])
```

```python
import functools

import jax
import jax.numpy as jnp
from jax import lax
from jax.experimental import pallas as pl
from jax.experimental.pallas import tpu as pltpu

F32 = jnp.float32
BF16 = jnp.bfloat16

D_MODEL = 1024
HEAD_DIM = 64
N_HEADS = 8
WIDTH = N_HEADS * HEAD_DIM
PAIR = 2 * HEAD_DIM
N_HEAD_PAIRS = N_HEADS // 2
DECAY_LORA = 64
ICLR_LORA = 64
GATE_LORA = 128
FOX_COLS = 3 * WIDTH + N_HEADS
RWKV_COLS = 3 * WIDTH + DECAY_LORA + ICLR_LORA + GATE_LORA
IN_COLS_PADDED = 3 * WIDTH + RWKV_COLS + 128
FZ_OFF = 3 * WIDTH + RWKV_COLS
ATTN_SCALE = HEAD_DIM ** -0.5
N_EXPERTS = 32
TOP_K = 4
SWIGLU_ALPHA = 1.702
SWIGLU_LIMIT = 7.0
RMS_EPS = 1e-6
GN_EPS = 64e-5
N_MOD = 6
PAGE = 128
NEG = -1e30

VMEM_LIMIT = 56 * 1024 * 1024


def _cparams(*sem):
    return pltpu.CompilerParams(dimension_semantics=sem, vmem_limit_bytes=VMEM_LIMIT)


def _dot(a, b):
    return jnp.dot(a, b, preferred_element_type=F32)


def _dot_nt(a, b):
    return lax.dot_general(a, b, (((1,), (1,)), ((), ())), preferred_element_type=F32)


def _split2(x):
    hi = x.astype(BF16)
    lo = (x - hi.astype(F32)).astype(BF16)
    return hi, lo


def _split3(x):
    hi = x.astype(BF16)
    r = x - hi.astype(F32)
    mid = r.astype(BF16)
    lo = (r - mid.astype(F32)).astype(BF16)
    return hi, mid, lo


def _dot_x3(a, b):
    ah, al = _split2(a)
    bh, bl = _split2(b)
    return _dot(ah, bh) + (_dot(ah, bl) + _dot(al, bh))


def _dot_sel2(a, sel):
    hi, lo = _split2(a)
    return _dot(hi, sel) + _dot(lo, sel)


def _dot_sel3(a, sel):
    hi, mid, lo = _split3(a)
    return _dot(hi, sel) + (_dot(mid, sel) + _dot(lo, sel))


def _sigmoid(x):
    return 1.0 / (1.0 + jnp.exp(-x))


def _softplus(x):
    return jnp.maximum(x, 0.0) + jnp.log(1.0 + jnp.exp(-jnp.abs(x)))


def _head_ones(n_heads):
    w = n_heads * HEAD_DIM
    r = lax.broadcasted_iota(jnp.int32, (w, w), 0) // HEAD_DIM
    c = lax.broadcasted_iota(jnp.int32, (w, w), 1) // HEAD_DIM
    return (r == c).astype(BF16)


def _mod_kernel(c_ref, w_ref, b_ref, o_ref):
    c = c_ref[...]
    s = c * _sigmoid(c)
    o_ref[...] = _dot_x3(s, w_ref[...]) + b_ref[...]


def _mod(c, w_mod, b_mod):
    rows, d = c.shape
    n = w_mod.shape[1]
    tn = 768
    return pl.pallas_call(
        _mod_kernel,
        grid=(n // tn,),
        in_specs=[pl.BlockSpec((rows, d), lambda j: (0, 0)),
                  pl.BlockSpec((d, tn), lambda j: (0, j)),
                  pl.BlockSpec((1, tn), lambda j: (0, j))],
        out_specs=pl.BlockSpec((rows, tn), lambda j: (0, j)),
        out_shape=jax.ShapeDtypeStruct((rows, n), F32),
        compiler_params=_cparams("arbitrary"),
    )(c, w_mod, b_mod.reshape(1, n))


def _inproj_kernel(x_ref, sc_ref, sh_ref, g_ref, w_ref, wfz_ref, gq_ref, gk_ref, bf_ref, bfc_ref, hs_ref,
                   qh_ref, kh_ref, vh_ref, k_ref, v_ref, lf_ref, lft_ref, rw_ref):
    x = x_ref[0]
    ms = jnp.mean(x * x, axis=-1, keepdims=True)
    xn = x * lax.rsqrt(ms + RMS_EPS) * g_ref[...]
    h = (xn * (1.0 + sc_ref[0]) + sh_ref[0]).astype(BF16)
    proj = _dot(h, w_ref[...])
    hs = hs_ref[...]

    def head_norm(t, gain):
        m = _dot_sel2(t * t, hs) * (1.0 / HEAD_DIM)
        return t * lax.rsqrt(m + RMS_EPS) * gain

    q = head_norm(proj[:, 0:WIDTH], gq_ref[...])
    k = head_norm(proj[:, WIDTH:2 * WIDTH], gk_ref[...])
    v = proj[:, 2 * WIDTH:3 * WIDTH]
    k_ref[0] = k
    v_ref[0] = v
    qb = q.astype(BF16)
    kb = k.astype(BF16)
    vb = v.astype(BF16)
    for hp in range(N_HEAD_PAIRS):
        sl = slice(hp * PAIR, (hp + 1) * PAIR)
        qh_ref[0, hp] = qb[:, sl]
        kh_ref[0, hp] = kb[:, sl]
        vh_ref[0, hp] = vb[:, sl]
    rw_ref[0] = proj[:, 3 * WIDTH:3 * WIDTH + RWKV_COLS]
    fz = proj[:, FZ_OFF:FZ_OFF + N_HEADS] + bf_ref[...]
    lf_ref[0] = -_softplus(-fz)
    fzt = _dot_nt(wfz_ref[...], h) + bfc_ref[...]
    lft_ref[0] = -_softplus(-fzt)


def _inproj(x3, sc, sh, g_mix, w_r, wfz_t, gq, gk, b_forget, tm):
    bx, tx, d = x3.shape
    per_row = sc.shape[1] != 1
    mod_spec = (pl.BlockSpec((1, tm, d), lambda b, t: (b, t, 0)) if per_row
                else pl.BlockSpec((1, 1, d), lambda b, t: (b, 0, 0)))
    const = lambda shape: pl.BlockSpec(shape, lambda b, t: tuple(0 for _ in shape))
    head_spec = pl.BlockSpec((1, N_HEAD_PAIRS, tm, PAIR), lambda b, t: (b, 0, t, 0))
    row_spec = lambda w: pl.BlockSpec((1, tm, w), lambda b, t: (b, t, 0))
    head_shape = jax.ShapeDtypeStruct((bx, N_HEAD_PAIRS, tx, PAIR), BF16)
    return pl.pallas_call(
        _inproj_kernel,
        grid=(bx, tx // tm),
        in_specs=[row_spec(d), mod_spec, mod_spec, const((1, d)), const((d, IN_COLS_PADDED)),
                  const((N_HEADS, d)), const((1, WIDTH)), const((1, WIDTH)), const((1, N_HEADS)),
                  const((N_HEADS, 1)), const((WIDTH, WIDTH))],
        out_specs=[head_spec, head_spec, head_spec, row_spec(WIDTH), row_spec(WIDTH), row_spec(N_HEADS),
                   pl.BlockSpec((1, N_HEADS, tm), lambda b, t: (b, 0, t)), row_spec(RWKV_COLS)],
        out_shape=[head_shape, head_shape, head_shape,
                   jax.ShapeDtypeStruct((bx, tx, WIDTH), F32), jax.ShapeDtypeStruct((bx, tx, WIDTH), F32),
                   jax.ShapeDtypeStruct((bx, tx, N_HEADS), F32), jax.ShapeDtypeStruct((bx, N_HEADS, tx), F32),
                   jax.ShapeDtypeStruct((bx, tx, RWKV_COLS), F32)],
        compiler_params=_cparams("parallel", "arbitrary"),
    )(x3, sc, sh, g_mix, w_r, wfz_t, gq, gk, b_forget.reshape(1, N_HEADS), b_forget.reshape(N_HEADS, 1),
      _head_ones(N_HEADS))


def _cumsum_kernel(l_ref, u_ref, o_ref, c_ref):
    tc = l_ref.shape[2]

    @pl.when(pl.program_id(1) == 0)
    def _():
        c_ref[...] = jnp.zeros_like(c_ref)

    f = _dot_sel3(l_ref[0], u_ref[...]) + c_ref[:, 0:1]
    o_ref[0] = f
    c_ref[...] = jnp.broadcast_to(f[:, tc - 1:tc], c_ref.shape)


def _cumsum_t(lft, tc):
    b, h, t = lft.shape
    r = lax.broadcasted_iota(jnp.int32, (tc, tc), 0)
    c = lax.broadcasted_iota(jnp.int32, (tc, tc), 1)
    upper = (r <= c).astype(BF16)
    return pl.pallas_call(
        _cumsum_kernel,
        grid=(b, t // tc),
        in_specs=[pl.BlockSpec((1, h, tc), lambda i, j: (i, 0, j)), pl.BlockSpec((tc, tc), lambda i, j: (0, 0))],
        out_specs=pl.BlockSpec((1, h, tc), lambda i, j: (i, 0, j)),
        out_shape=jax.ShapeDtypeStruct((b, h, t), F32),
        scratch_shapes=[pltpu.VMEM((h, 128), F32)],
        compiler_params=_cparams("parallel", "arbitrary"),
    )(lft, upper)


def _attn_kernel(q_ref, k_ref, v_ref, f_ref, o_ref, m_ref, acc_ref, *, tq, tk):
    i = pl.program_id(2)
    q2 = q_ref[0, 0]
    lane = lax.broadcasted_iota(jnp.int32, q2.shape, 1)
    zero = jnp.zeros_like(q2)
    qs = (jnp.where(lane < HEAD_DIM, q2, zero), jnp.where(lane >= HEAD_DIM, q2, zero))
    q0 = pl.multiple_of(i * tq, tq)
    f_base = [f_ref[0, 0, hh:hh + 1, pl.ds(q0, tq)][:, 0:1] for hh in range(2)]
    m_ref[...] = jnp.full_like(m_ref, NEG)
    acc_ref[...] = jnp.zeros_like(acc_ref)

    def tile(j, masked):
        k0 = pl.multiple_of(j * tk, tk)
        k2 = k_ref[0, 0, pl.ds(k0, tk), :]
        v2 = v_ref[0, 0, pl.ds(k0, tk), :]
        vlane = lax.broadcasted_iota(jnp.int32, v2.shape, 1)
        one = jnp.ones((), v2.dtype)
        vs = (jnp.where(vlane < HEAD_DIM, v2, one), jnp.where(vlane >= HEAD_DIM, v2, one))
        for hh in range(2):
            s = _dot_nt(qs[hh], k2) + (f_base[hh] - f_ref[0, 0, hh:hh + 1, pl.ds(k0, tk)])
            if masked:
                r = lax.broadcasted_iota(jnp.int32, s.shape, 0)
                c = lax.broadcasted_iota(jnp.int32, s.shape, 1)
                s = jnp.where(c + k0 <= r + q0, s, NEG)
            m_old = m_ref[hh]
            m_new = jnp.maximum(m_old, jnp.max(s, axis=1, keepdims=True))
            p = jnp.exp(s - jnp.concatenate([m_new] * (tk // 128), axis=1))
            alpha = jnp.exp(m_old - m_new)
            acc_ref[hh] = alpha * acc_ref[hh] + _dot(p.astype(BF16), vs[hh])
            m_ref[hh] = m_new

    def body(j, carry):
        tile(j, False)
        return carry

    n_sub = tq // tk
    lax.fori_loop(0, i * n_sub, body, 0)
    for dd in range(n_sub):
        tile(i * n_sub + dd, True)
    acc_a = acc_ref[0]
    acc_b = acc_ref[1]
    o_a = acc_a / pltpu.roll(acc_a, HEAD_DIM, 1)
    o_b = acc_b / pltpu.roll(acc_b, HEAD_DIM, 1)
    o_ref[0, 0] = jnp.where(lane < HEAD_DIM, o_a, o_b).astype(o_ref.dtype)


def _fox_prompt(qh, kh, vh, ft, tq, tk):
    b, hp, t, w = qh.shape
    assert tq % tk == 0 and tk % 128 == 0
    ft4 = ft.reshape(b, hp, 2, t)
    seq_spec = pl.BlockSpec((1, 1, t, w), lambda bi, hi, i: (bi, hi, 0, 0))
    tile_spec = pl.BlockSpec((1, 1, tq, w), lambda bi, hi, i: (bi, hi, i, 0))
    return pl.pallas_call(
        functools.partial(_attn_kernel, tq=tq, tk=tk),
        grid=(b, hp, t // tq),
        in_specs=[tile_spec, seq_spec, seq_spec, pl.BlockSpec((1, 1, 2, t), lambda bi, hi, i: (bi, hi, 0, 0))],
        out_specs=tile_spec,
        out_shape=jax.ShapeDtypeStruct((b, hp, t, w), BF16),
        scratch_shapes=[pltpu.VMEM((2, tq, 128), F32), pltpu.VMEM((2, tq, w), F32)],
        compiler_params=_cparams("parallel", "parallel", "arbitrary"),
    )(qh, kh, vh, ft4)


PAGES_PER_STEP = 8
QROWS = 4 * N_HEADS


def _decode_kernel(pt_ref, q_ref, kn_ref, vn_ref, lfn_ref, cnew_ref, ustrict_ref, fold_ref, *refs, n_steps):
    del pt_ref
    npg = PAGES_PER_STEP
    k_refs = refs[0:npg]
    v_refs = refs[npg:2 * npg]
    lf_refs = refs[2 * npg:3 * npg]
    o_ref, m_ref, l_ref, acc_ref, accn_ref, carry_ref = refs[3 * npg:]
    j = pl.program_id(1)
    q = q_ref[0]
    r = lax.broadcasted_iota(jnp.int32, (QROWS, WIDTH), 0)
    c = lax.broadcasted_iota(jnp.int32, (QROWS, WIDTH), 1)
    own_head = (r % N_HEADS) == (c // HEAD_DIM)
    q_bd = jnp.where(own_head, jnp.concatenate([q] * N_HEADS, axis=1), jnp.zeros((), q.dtype))

    def softmax_step(s):
        m_old = m_ref[...]
        m_new = jnp.maximum(m_old, jnp.max(s, axis=1, keepdims=True))
        p = jnp.exp(s - m_new)
        alpha = jnp.exp(m_old - m_new)
        l_ref[...] = alpha * l_ref[...] + jnp.sum(p, axis=1, keepdims=True)
        m_ref[...] = m_new
        return p.astype(BF16), alpha

    @pl.when(j == 0)
    def _():
        m_ref[...] = jnp.full_like(m_ref, NEG)
        l_ref[...] = jnp.zeros_like(l_ref)
        acc_ref[...] = jnp.zeros_like(acc_ref)
        carry_ref[...] = jnp.zeros_like(carry_ref)
        s = _dot_nt(q, kn_ref[0])
        bias = _dot_sel3(lfn_ref[0], cnew_ref[...])[0:1, :]
        rr = lax.broadcasted_iota(jnp.int32, s.shape, 0)
        cc = lax.broadcasted_iota(jnp.int32, s.shape, 1)
        ok = ((rr % N_HEADS) == (cc % N_HEADS)) & ((cc // N_HEADS) <= (rr // N_HEADS))
        p, _ = softmax_step(jnp.where(ok, s + bias, NEG))
        accn_ref[...] = _dot(p, vn_ref[0])

    ustrict = ustrict_ref[...]
    carry = carry_ref[...]
    scores = []
    for u in range(npg):
        kt = k_refs[u][...].reshape(WIDTH, PAGE).astype(BF16)
        lft = lf_refs[u][...]
        a = jnp.broadcast_to(lft[None], (4, N_HEADS, PAGE)).reshape(QROWS, PAGE)
        hi, mid, lo = _split3(a)
        bx = _dot(jnp.concatenate([hi, mid, lo], axis=0), ustrict)
        bias = bx[0:QROWS] + (bx[QROWS:2 * QROWS] + bx[2 * QROWS:3 * QROWS]) + carry
        carry = carry + jnp.sum(a, axis=1, keepdims=True)
        scores.append(_dot(q_bd, kt) + bias)
    carry_ref[...] = carry
    p, alpha = softmax_step(jnp.concatenate(scores, axis=1))
    acc = alpha * acc_ref[...]
    for u in range(npg):
        vt = v_refs[u][...].reshape(WIDTH, PAGE).astype(BF16)
        acc = acc + _dot_nt(p[:, u * PAGE:(u + 1) * PAGE], vt)
    acc_ref[...] = acc
    accn_ref[...] = alpha * accn_ref[...]

    @pl.when(j == n_steps - 1)
    def _():
        own = jnp.where(own_head, acc_ref[...], 0.0)
        o_ref[0] = (_dot_sel3(own, fold_ref[...]) + accn_ref[...]) / l_ref[...]


def _fox_decode(page_table, q, k_new, v_new, lf_new, cache_kt, cache_vt, cache_lft):
    nb, n_pages = page_table.shape
    npg = PAGES_PER_STEP
    n_steps = n_pages // npg
    ustrict = (lax.broadcasted_iota(jnp.int32, (PAGE, PAGE), 0)
               > lax.broadcasted_iota(jnp.int32, (PAGE, PAGE), 1)).astype(BF16)
    ri = lax.broadcasted_iota(jnp.int32, (QROWS, QROWS), 0)
    ci = lax.broadcasted_iota(jnp.int32, (QROWS, QROWS), 1)
    cnew = -(((ri % N_HEADS) == (ci % N_HEADS)) & ((ri // N_HEADS) <= (ci // N_HEADS))).astype(BF16)
    fold = (lax.broadcasted_iota(jnp.int32, (WIDTH, HEAD_DIM), 0) % HEAD_DIM
            == lax.broadcasted_iota(jnp.int32, (WIDTH, HEAD_DIM), 1)).astype(BF16)

    def page_idx(u):
        return lambda b, j, pt: (pt[b, n_pages - 1 - (j * npg + u)], 0, 0, 0)

    def page_idx3(u):
        return lambda b, j, pt: (pt[b, n_pages - 1 - (j * npg + u)], 0, 0)

    per_b = lambda shape: pl.BlockSpec((1,) + shape, lambda b, j, pt: (b, 0, 0))
    const = lambda shape: pl.BlockSpec(shape, lambda b, j, pt: (0, 0))
    in_specs = ([per_b((QROWS, HEAD_DIM)), per_b((QROWS, HEAD_DIM)), per_b((QROWS, HEAD_DIM)),
                 per_b((8, QROWS)), const((QROWS, QROWS)), const((PAGE, PAGE)), const((WIDTH, HEAD_DIM))]
                + [pl.BlockSpec((None, N_HEADS, HEAD_DIM, PAGE), page_idx(u)) for u in range(npg)]
                + [pl.BlockSpec((None, N_HEADS, HEAD_DIM, PAGE), page_idx(u)) for u in range(npg)]
                + [pl.BlockSpec((None, N_HEADS, PAGE), page_idx3(u)) for u in range(npg)])
    grid_spec = pltpu.PrefetchScalarGridSpec(
        num_scalar_prefetch=1,
        grid=(nb, n_steps),
        in_specs=in_specs,
        out_specs=pl.BlockSpec((1, QROWS, HEAD_DIM), lambda b, j, pt: (b, 0, 0)),
        scratch_shapes=[pltpu.VMEM((QROWS, 1), F32), pltpu.VMEM((QROWS, 1), F32),
                        pltpu.VMEM((QROWS, WIDTH), F32), pltpu.VMEM((QROWS, HEAD_DIM), F32),
                        pltpu.VMEM((QROWS, 1), F32)],
    )
    lfn = jnp.broadcast_to(lf_new.reshape(nb, 1, QROWS), (nb, 8, QROWS))
    return pl.pallas_call(
        functools.partial(_decode_kernel, n_steps=n_steps),
        grid_spec=grid_spec,
        out_shape=jax.ShapeDtypeStruct((nb, QROWS, HEAD_DIM), F32),
        compiler_params=_cparams("parallel", "arbitrary"),
    )(page_table, q, k_new, v_new, lfn, cnew, ustrict, fold,
      *([cache_kt] * npg), *([cache_vt] * npg), *([cache_lft] * npg))


def _rwkv_prep_kernel(p_ref, prev_ref, first_ref, mu_ref, w0_ref, w2_ref, a0_ref, a2_ref, g2_ref, kk_ref, ka_ref,
                      rk_ref, hs_ref, r_ref, k_ref, v_ref, w_ref, a_ref, b_ref, g_ref, bonus_ref, *, shift_rows):
    p = p_ref[0]
    if shift_rows:
        carry = jnp.where(pl.program_id(1) == 0, first_ref[0], prev_ref[0, 7:8, :])
        row0 = lax.broadcasted_iota(jnp.int32, p.shape, 0) == 0
        prev = jnp.where(row0, carry, pltpu.roll(p, 1, 0))
    else:
        prev = prev_ref[0]
    z = p + (prev - p) * mu_ref[...]
    r = z[:, 0:WIDTH]
    k = z[:, WIDTH:2 * WIDTH]
    v = z[:, 2 * WIDTH:3 * WIDTH]
    lora = z[:, 3 * WIDTH:3 * WIDTH + DECAY_LORA + ICLR_LORA]
    gd = z[:, 3 * WIDTH + DECAY_LORA + ICLR_LORA:RWKV_COLS]
    hs = hs_ref[...]
    w = -_softplus(-(w0_ref[...] + _dot_x3(jnp.tanh(lora), w2_ref[...]))) - 0.5
    decay = jnp.exp(-jnp.exp(w))
    a = _sigmoid(a0_ref[...] + _dot_x3(lora, a2_ref[...]))
    g = _dot_x3(_sigmoid(gd), g2_ref[...])
    kk = k * kk_ref[...]
    norm = jnp.sqrt(_dot_sel3(kk * kk, hs))
    kk = kk / jnp.maximum(norm, 1e-12)
    k = k * (1.0 + (a - 1.0) * ka_ref[...])
    bonus = _dot_sel3(r * k * rk_ref[...], hs) * v
    r_ref[0] = r
    k_ref[0] = k
    v_ref[0] = v
    w_ref[0] = decay
    a_ref[0] = -kk
    b_ref[0] = kk * a
    g_ref[0] = g
    bonus_ref[0] = bonus


def _rwkv_prep(p_rw, prev, first, mu, w0, w2p, a0, a2p, g2, kk, ka, rk, tm):
    bx, tx, _ = p_rw.shape
    row = lambda w: pl.BlockSpec((1, tm, w), lambda b, t: (b, t, 0))
    const = lambda shape: pl.BlockSpec(shape, lambda b, t: tuple(0 for _ in shape))
    out = jax.ShapeDtypeStruct((bx, tx, WIDTH), F32)
    shift_rows = prev is None
    if shift_rows:
        prev = p_rw
        prev_spec = pl.BlockSpec((1, 8, RWKV_COLS), lambda b, t: (b, jnp.maximum(t * (tm // 8) - 1, 0), 0))
    else:
        prev_spec = row(RWKV_COLS)
    return pl.pallas_call(
        functools.partial(_rwkv_prep_kernel, shift_rows=shift_rows),
        grid=(bx, tx // tm),
        in_specs=[row(RWKV_COLS), prev_spec, pl.BlockSpec((1, 1, RWKV_COLS), lambda b, t: (b, 0, 0)),
                  const((1, RWKV_COLS)), const((1, WIDTH)),
                  const((DECAY_LORA + ICLR_LORA, WIDTH)), const((1, WIDTH)), const((DECAY_LORA + ICLR_LORA, WIDTH)),
                  const((GATE_LORA, WIDTH)), const((1, WIDTH)), const((1, WIDTH)), const((1, WIDTH)),
                  const((WIDTH, WIDTH))],
        out_specs=[row(WIDTH)] * 8,
        out_shape=[out] * 8,
        compiler_params=_cparams("parallel", "arbitrary"),
    )(p_rw, prev, first, mu, w0, w2p, a0, a2p, g2, kk, ka, rk, _head_ones(N_HEADS))


WKV_NB = 2
WKV_HG = 4
WKV_LANES = WKV_HG * HEAD_DIM
WKV_GROUPS = N_HEADS // WKV_HG
WKV_ROWS = WKV_NB * WKV_GROUPS * HEAD_DIM


def _wkv_kernel(r_ref, k_ref, v_ref, w_ref, a_ref, b_ref, s0_ref, ones_ref, y_ref, sT_ref, st_ref, *, tc, n_t):
    ti = pl.program_id(1)
    blocks = [(bi, gg) for bi in range(WKV_NB) for gg in range(WKV_GROUPS)]

    @pl.when(ti == 0)
    def _():
        for n, (bi, gg) in enumerate(blocks):
            st_ref[n * HEAD_DIM:(n + 1) * HEAD_DIM, :] = jnp.concatenate(
                [s0_ref[bi, gg * WKV_HG + q] for q in range(WKV_HG)], axis=1)

    rows = lax.broadcasted_iota(jnp.int32, (WKV_ROWS, WKV_LANES), 0)
    lanes = lax.broadcasted_iota(jnp.int32, (WKV_ROWS, WKV_LANES), 1)
    eye = (rows % HEAD_DIM) == (lanes % HEAD_DIM)
    eye_f = eye.astype(F32)
    ones = ones_ref[...]
    grp = min(tc, 8)

    def bcast(xs, u):
        return jnp.concatenate(
            [jnp.broadcast_to(xs[bi][u:u + 1, gg * WKV_LANES:(gg + 1) * WKV_LANES], (HEAD_DIM, WKV_LANES))
             for bi, gg in blocks], axis=0)

    def group(gi, carry):
        t0 = pl.multiple_of(gi * grp, grp)
        load = lambda ref: [ref[bi, pl.ds(t0, grp), :] for bi in range(WKV_NB)]
        r8, k8, v8, w8, a8, b8 = (load(ref) for ref in (r_ref, k_ref, v_ref, w_ref, a_ref, b_ref))
        vdiag = lambda u: jnp.where(eye, bcast(v8, u), 0.0).astype(BF16)
        s = st_ref[...]
        ys = []

        def y_rows(yb):
            yb = yb * eye_f
            return [jnp.sum(yb[n * HEAD_DIM:(n + 1) * HEAD_DIM], axis=0, keepdims=True) for n in range(len(blocks))]

        vcol = _dot(vdiag(0), ones)
        ybf = None
        for u in range(grp):
            sa = _dot((s * bcast(a8, u)).astype(BF16), ones)
            side = [vdiag(u + 1)] if u + 1 < grp else []
            if ybf is not None:
                side.append(ybf)
            misc = _dot(jnp.concatenate(side, axis=0), ones) if side else None
            s = s * bcast(w8, u) + sa * bcast(b8, u) + vcol * bcast(k8, u)
            if misc is not None:
                off = 0
                if u + 1 < grp:
                    vcol = misc[0:WKV_ROWS]
                    off = WKV_ROWS
                if ybf is not None:
                    ys.append(y_rows(misc[off:off + WKV_ROWS]))
            ybf = (s * bcast(r8, u)).astype(BF16)
        ys.append(y_rows(_dot(ybf, ones)))
        st_ref[...] = s
        for bi in range(WKV_NB):
            y_ref[bi, pl.ds(t0, grp), :] = jnp.concatenate(
                [jnp.concatenate([ys[u][bi * WKV_GROUPS + gg] for gg in range(WKV_GROUPS)], axis=1)
                 for u in range(grp)], axis=0)
        return carry

    lax.fori_loop(0, tc // grp, group, 0)

    @pl.when(ti == n_t - 1)
    def _():
        for n, (bi, gg) in enumerate(blocks):
            for q in range(WKV_HG):
                sT_ref[bi, gg * WKV_HG + q] = st_ref[n * HEAD_DIM:(n + 1) * HEAD_DIM, q * HEAD_DIM:(q + 1) * HEAD_DIM]


def _wkv_scan(r, k, v, w, a, b, s0, tc):
    bx, tx, _ = r.shape
    assert bx % WKV_NB == 0
    n_t = tx // tc
    seq = pl.BlockSpec((WKV_NB, tc, WIDTH), lambda i, t: (i, t, 0))
    st = pl.BlockSpec((WKV_NB, N_HEADS, HEAD_DIM, HEAD_DIM), lambda i, t: (i, 0, 0, 0))
    ri = lax.broadcasted_iota(jnp.int32, (WKV_LANES, WKV_LANES), 0) // HEAD_DIM
    ci = lax.broadcasted_iota(jnp.int32, (WKV_LANES, WKV_LANES), 1) // HEAD_DIM
    ones = (ri == ci).astype(BF16)
    return pl.pallas_call(
        functools.partial(_wkv_kernel, tc=tc, n_t=n_t),
        grid=(bx // WKV_NB, n_t),
        in_specs=[seq] * 6 + [st, pl.BlockSpec((WKV_LANES, WKV_LANES), lambda i, t: (0, 0))],
        out_specs=[seq, st],
        out_shape=[jax.ShapeDtypeStruct((bx, tx, WIDTH), F32),
                   jax.ShapeDtypeStruct((bx, N_HEADS, HEAD_DIM, HEAD_DIM), F32)],
        scratch_shapes=[pltpu.VMEM((WKV_ROWS, WKV_LANES), F32)],
        compiler_params=_cparams("parallel", "arbitrary"),
    )(r, k, v, w, a, b, s0, ones)


def _outproj_kernel(x_ref, of_ref, y_ref, bonus_ref, g_ref, gt_ref, sc_ref, sh_ref, lw_ref, lb_ref, gf_ref,
                    wo_ref, wr_ref, br_ref, hs_ref, lt_ref, x1_ref, h2_ref, ti_ref, tw_ref, rk_ref, cnt_ref, run_ref):
    @pl.when(pl.program_id(1) == 0)
    def _():
        run_ref[...] = jnp.zeros_like(run_ref)

    hs = hs_ref[...]
    y = y_ref[0]
    mu = _dot_sel3(y, hs) * (1.0 / HEAD_DIM)
    yc = y - mu
    var = _dot_sel3(yc * yc, hs) * (1.0 / HEAD_DIM)
    o_rw = ((yc * lax.rsqrt(var + GN_EPS)) * lw_ref[...] + lb_ref[...] + bonus_ref[0]) * g_ref[0]
    mixed = _dot(o_rw.astype(BF16), wo_ref[WIDTH:2 * WIDTH, :])
    for hp in range(N_HEAD_PAIRS):
        mixed = mixed + _dot(of_ref[0, hp], wo_ref[hp * PAIR:(hp + 1) * PAIR, :])
    x1 = x_ref[0] + gt_ref[0] * mixed
    x1_ref[0] = x1
    ms = jnp.mean(x1 * x1, axis=-1, keepdims=True)
    h2 = (x1 * lax.rsqrt(ms + RMS_EPS) * gf_ref[...]) * (1.0 + sc_ref[0]) + sh_ref[0]
    h2_ref[0] = h2.astype(BF16)
    logits = _dot_x3(h2, wr_ref[...]) + br_ref[...]
    lane = lax.broadcasted_iota(jnp.int32, logits.shape, 1).astype(F32)
    vals, idxs = [], []
    for _ in range(TOP_K):
        m = jnp.max(logits, axis=1, keepdims=True)
        idx = jnp.min(jnp.where(logits == m, lane, float(N_EXPERTS)), axis=1, keepdims=True)
        vals.append(m)
        idxs.append(idx)
        logits = jnp.where(lane == idx, -jnp.inf, logits)
    e = [jnp.exp(vv - vals[0]) for vv in vals]
    tot = e[0] + e[1] + e[2] + e[3]
    tw_ref[0] = jnp.concatenate(e, axis=1) / tot
    ti_ref[0] = jnp.concatenate(idxs, axis=1).astype(jnp.int32)
    chosen = [lane == idx for idx in idxs]
    multi_hot = jnp.where(chosen[0] | chosen[1] | chosen[2] | chosen[3], 1.0, 0.0)
    before = _dot(lt_ref[...], multi_hot.astype(BF16)) + run_ref[...]
    ranks = [jnp.sum(jnp.where(c, before, 0.0), axis=1, keepdims=True) for c in chosen]
    rk_ref[0] = jnp.concatenate(ranks, axis=1).astype(jnp.int32)
    run_ref[...] = run_ref[...] + jnp.sum(multi_hot, axis=0, keepdims=True)
    cnt_ref[0] = run_ref[...]


def _outproj(x3, of, y, bonus, g, gt, sc, sh, lnx_w, lnx_b, g_ffn, w_out_b, w_router, b_router, tm):
    bx, tx, d = x3.shape
    per_row = gt.shape[1] != 1
    mod_spec = (pl.BlockSpec((1, tm, d), lambda b, t: (b, t, 0)) if per_row
                else pl.BlockSpec((1, 1, d), lambda b, t: (b, 0, 0)))
    row = lambda w: pl.BlockSpec((1, tm, w), lambda b, t: (b, t, 0))
    const = lambda shape: pl.BlockSpec(shape, lambda b, t: tuple(0 for _ in shape))
    lower = (lax.broadcasted_iota(jnp.int32, (tm, tm), 0) > lax.broadcasted_iota(jnp.int32, (tm, tm), 1)).astype(BF16)
    return pl.pallas_call(
        _outproj_kernel,
        grid=(bx, tx // tm),
        in_specs=[row(d), pl.BlockSpec((1, N_HEAD_PAIRS, tm, PAIR), lambda b, t: (b, 0, t, 0)),
                  row(WIDTH), row(WIDTH), row(WIDTH), mod_spec, mod_spec, mod_spec,
                  const((1, WIDTH)), const((1, WIDTH)), const((1, d)), const((2 * WIDTH, d)),
                  const((d, N_EXPERTS)), const((1, N_EXPERTS)), const((WIDTH, WIDTH)), const((tm, tm))],
        out_specs=[row(d), row(d), row(TOP_K), row(TOP_K), row(TOP_K),
                   pl.BlockSpec((1, 1, N_EXPERTS), lambda b, t: (b, 0, 0))],
        out_shape=[jax.ShapeDtypeStruct((bx, tx, d), F32), jax.ShapeDtypeStruct((bx, tx, d), BF16),
                   jax.ShapeDtypeStruct((bx, tx, TOP_K), jnp.int32), jax.ShapeDtypeStruct((bx, tx, TOP_K), F32),
                   jax.ShapeDtypeStruct((bx, tx, TOP_K), jnp.int32), jax.ShapeDtypeStruct((bx, 1, N_EXPERTS), F32)],
        scratch_shapes=[pltpu.VMEM((1, N_EXPERTS), F32)],
        compiler_params=_cparams("parallel", "arbitrary"),
    )(x3, of, y, bonus, g, gt, sc, sh, lnx_w, lnx_b, g_ffn, w_out_b, w_router, b_router, _head_ones(N_HEADS), lower)


MOE_TM = 256


def _moe_kernel(te_ref, tf_ref, x_ref, wg_ref, bg_ref, wu_ref, bu_ref, wd_ref, bd_ref, o_ref, wb_ref):
    del te_ref
    i = pl.program_id(0)

    @pl.when(tf_ref[i] == 2)
    def _():
        wb_ref[0] = wg_ref[0].astype(BF16)
        wb_ref[1] = wu_ref[0].astype(BF16)
        wb_ref[2] = wd_ref[0].astype(BF16)

    @pl.when(tf_ref[i] != 0)
    def _():
        x = x_ref[...]
        glu = jnp.minimum(_dot(x, wb_ref[0]) + bg_ref[0], SWIGLU_LIMIT)
        lin = jnp.clip(_dot(x, wb_ref[1]) + bu_ref[0], -SWIGLU_LIMIT, SWIGLU_LIMIT)
        act = glu * _sigmoid(SWIGLU_ALPHA * glu) * (lin + 1.0)
        o_ref[...] = (_dot(act.astype(BF16), wb_ref[2]) + bd_ref[0]).astype(o_ref.dtype)

    @pl.when(tf_ref[i] == 0)
    def _():
        o_ref[...] = jnp.zeros_like(o_ref)


def _moe_gemm(tile_expert, tile_flag, xg, wg, bg, wu, bu, wd, bd):
    n_slots, d = xg.shape
    n_tiles = n_slots // MOE_TM
    wspec = pl.BlockSpec((1, d, d), lambda i, te, tf: (te[i], 0, 0))
    bspec = pl.BlockSpec((1, 1, d), lambda i, te, tf: (te[i], 0, 0))
    grid_spec = pltpu.PrefetchScalarGridSpec(
        num_scalar_prefetch=2,
        grid=(n_tiles,),
        in_specs=[pl.BlockSpec((MOE_TM, d), lambda i, te, tf: (i, 0)), wspec, bspec, wspec, bspec, wspec, bspec],
        out_specs=pl.BlockSpec((MOE_TM, d), lambda i, te, tf: (i, 0)),
        scratch_shapes=[pltpu.VMEM((3, d, d), BF16)],
    )
    return pl.pallas_call(
        _moe_kernel,
        grid_spec=grid_spec,
        out_shape=jax.ShapeDtypeStruct((n_slots, d), BF16),
        compiler_params=_cparams("arbitrary"),
    )(tile_expert, tile_flag, xg, wg, bg, wu, bu, wd, bd)


def _final_kernel(x1_ref, yg_ref, tw_ref, gt_ref, o_ref):
    tw = tw_ref[0]
    moe = tw[:, 0:1] * yg_ref[0, 0].astype(F32)
    for kk in range(1, TOP_K):
        moe = moe + tw[:, kk:kk + 1] * yg_ref[kk, 0].astype(F32)
    o_ref[0] = x1_ref[0] + gt_ref[0] * moe


def _final(x1, yg, tw, gt, tm):
    bx, tx, d = x1.shape
    per_row = gt.shape[1] != 1
    mod_spec = (pl.BlockSpec((1, tm, d), lambda b, t: (b, t, 0)) if per_row
                else pl.BlockSpec((1, 1, d), lambda b, t: (b, 0, 0)))
    row = pl.BlockSpec((1, tm, d), lambda b, t: (b, t, 0))
    return pl.pallas_call(
        _final_kernel,
        grid=(bx, tx // tm),
        in_specs=[row, pl.BlockSpec((TOP_K, 1, tm, d), lambda b, t: (0, b, t, 0)),
                  pl.BlockSpec((1, tm, TOP_K), lambda b, t: (b, t, 0)), mod_spec],
        out_specs=row,
        out_shape=jax.ShapeDtypeStruct((bx, tx, d), F32),
        compiler_params=_cparams("parallel", "arbitrary"),
    )(x1, yg, tw, gt)


def _dispatch_plan(top_i, rank_in_seg, seg_counts, seg_tokens):
    n_assign = top_i.size
    n_tiles = -(-n_assign // MOE_TM) + N_EXPERTS
    n_slots = n_tiles * MOE_TM
    e_flat = top_i.reshape(-1)
    order = jnp.argsort(e_flat, stable=True).astype(jnp.int32)
    counts = jnp.sum(seg_counts, axis=0)
    cnt_start = jnp.cumsum(counts) - counts
    seg_base = jnp.cumsum(seg_counts, axis=0) - seg_counts
    padded = ((counts + MOE_TM - 1) // MOE_TM) * MOE_TM
    pad_end = jnp.cumsum(padded)
    pad_start = pad_end - padded
    experts = jnp.arange(N_EXPERTS, dtype=jnp.int32)
    tile_start = jnp.arange(n_tiles, dtype=jnp.int32) * MOE_TM
    tile_expert = jnp.minimum(jnp.sum((tile_start[:, None] >= pad_end[None, :]).astype(jnp.int32), axis=1),
                              N_EXPERTS - 1)
    tile_live = tile_start < pad_end[-1]
    is_first = jnp.concatenate([jnp.ones((1,), bool), tile_expert[1:] != tile_expert[:-1]])
    tile_flag = jnp.where(tile_live, jnp.where(is_first, 2, 1), 0).astype(jnp.int32)
    of_tile = lambda tab: jnp.sum(jnp.where(tile_expert[:, None] == experts[None, :], tab[None, :], 0), axis=1)
    rank = (tile_start - of_tile(pad_start))[:, None] + jnp.arange(MOE_TM, dtype=jnp.int32)[None, :]
    live = rank < of_tile(counts)[:, None]
    src = order.at[jnp.clip(of_tile(cnt_start)[:, None] + rank, 0, n_assign - 1)].get(mode="promise_in_bounds")
    token_of_slot = jnp.where(live, src // TOP_K, 0).reshape(n_slots)
    tok_base = jnp.repeat(seg_base + pad_start[None, :], jnp.asarray(seg_tokens), axis=0,
                          total_repeat_length=sum(seg_tokens))
    hit = top_i[:, :, None] == experts[None, None, :]
    slot_of_assign = jnp.sum(jnp.where(hit, tok_base[:, None, :], 0), axis=-1) + rank_in_seg
    return slot_of_assign, token_of_slot, tile_expert, tile_flag


def _pick_tile(n, pref):
    t = min(n, pref)
    assert n % t == 0, (n, t)
    return t


def kernel(x_prompt, x_sample, cache_k, cache_v, cache_logf, state_wkv, state_shift, page_table, c_prompt,
           c_sample, w_mod, b_mod, g_mix, g_ffn, w_in, q_norm, k_norm, b_forget, rw_mu, rw_w0, rw_w2, rw_a0,
           rw_a2, rw_g2, rw_kk, rw_ka, rw_rk, lnx_w, lnx_b, w_out, w_router, b_router, w_gate, b_gate, w_up,
           b_up, w_down, b_down):
    depth = w_mod.shape[0]
    assert depth == 1
    n_p, t_p, d = x_prompt.shape
    n_s, t_s, _ = x_sample.shape
    assert d == D_MODEL and t_s * N_HEADS == QROWS
    l = 0

    w_in_l = w_in[l]
    w_r = jnp.concatenate([w_in_l[:, :3 * WIDTH], w_in_l[:, FOX_COLS:], w_in_l[:, 3 * WIDTH:FOX_COLS],
                           jnp.zeros((d, 128 - N_HEADS), F32)], axis=1).astype(BF16)
    wfz_t = w_in_l[:, 3 * WIDTH:FOX_COLS].T.astype(BF16)
    gq = (jnp.tile(q_norm[l], N_HEADS) * ATTN_SCALE).reshape(1, WIDTH)
    gk = jnp.tile(k_norm[l], N_HEADS).reshape(1, WIDTH)
    zpad = jnp.zeros((DECAY_LORA, WIDTH), F32)
    w2p = jnp.concatenate([rw_w2[l], zpad], axis=0)
    a2p = jnp.concatenate([zpad, rw_a2[l]], axis=0)
    row = lambda a: a.reshape(1, -1)

    n_c = n_p + n_s
    pad_c = (-n_c) % 8
    c_all = jnp.concatenate([c_sample, c_prompt, jnp.zeros((pad_c, d), F32)], axis=0)
    mod = _mod(c_all, w_mod[l], b_mod[l])
    mod_s = mod[:n_s].reshape(n_s, 1, N_MOD, d)
    mod_p = mod[n_s:n_s + n_p].reshape(n_p, 1, N_MOD, d)
    mod_s = jnp.broadcast_to(mod_s, (n_s, t_s, N_MOD, d)).reshape(1, n_s * t_s, N_MOD, d)
    mp = [mod_p[:, :, i] for i in range(N_MOD)]
    msm = [mod_s[:, :, i] for i in range(N_MOD)]

    xs3 = x_sample.reshape(1, n_s * t_s, d)
    tm_p = _pick_tile(t_p, 256)
    tm_s = _pick_tile(n_s * t_s, 256)

    proj_args = (row(g_mix[l]), w_r, wfz_t, gq, gk, b_forget[l])
    qh_p, kh_p, vh_p, k_p, v_p, lf_p, lft_p, rw_p = _inproj(x_prompt, mp[1], mp[0], *proj_args, tm_p)
    qh_s, _, _, k_s, v_s, lf_s, _, rw_s = _inproj(xs3, msm[1], msm[0], *proj_args, tm_s)

    ft = _cumsum_t(lft_p, _pick_tile(t_p, 512))
    tq = _pick_tile(t_p, 1024)
    of_p = _fox_prompt(qh_p, kh_p, vh_p, ft, tq, _pick_tile(tq, 512))

    q_s = jnp.transpose(qh_s[0], (1, 0, 2)).reshape(n_s, QROWS, HEAD_DIM)
    kn_s = k_s.reshape(n_s, QROWS, HEAD_DIM).astype(BF16)
    vn_s = v_s.reshape(n_s, QROWS, HEAD_DIM).astype(BF16)
    cache_kt = jnp.transpose(cache_k[l], (0, 2, 3, 1))
    cache_vt = jnp.transpose(cache_v[l], (0, 2, 3, 1))
    cache_lft = jnp.swapaxes(cache_logf[l], 1, 2)
    o_s = _fox_decode(page_table, q_s, kn_s, vn_s, lf_s.reshape(n_s, QROWS), cache_kt, cache_vt, cache_lft)
    of_s = jnp.transpose(o_s.reshape(n_s * t_s, N_HEAD_PAIRS, PAIR), (1, 0, 2))[None].astype(BF16)

    prep_w = (row(rw_mu[l]), row(rw_w0[l]), w2p, row(rw_a0[l]), a2p, rw_g2[l], row(rw_kk[l]), row(rw_ka[l]),
              row(rw_rk[l]))
    rw_s4 = rw_s.reshape(n_s, t_s, RWKV_COLS)
    prev_s = jnp.concatenate([state_shift[l][:, None, :], rw_s4[:, :-1]], axis=1).reshape(1, n_s * t_s, RWKV_COLS)
    r_p, kk_p, vv_p, w_p, a_p, b_p, g_p, bonus_p = _rwkv_prep(
        rw_p, None, jnp.zeros((n_p, 1, RWKV_COLS), F32), *prep_w, tm_p)
    r_s, kk_s, vv_s, w_s, a_s, b_s, g_s, bonus_s = _rwkv_prep(
        rw_s, prev_s, jnp.zeros((1, 1, RWKV_COLS), F32), *prep_w, tm_s)
    y_p, wkv_p = _wkv_scan(r_p, kk_p, vv_p, w_p, a_p, b_p,
                           jnp.zeros((n_p, N_HEADS, HEAD_DIM, HEAD_DIM), F32), _pick_tile(t_p, 256))
    to_seq = lambda a: a.reshape(n_s, t_s, WIDTH)
    y_s, wkv_s = _wkv_scan(to_seq(r_s), to_seq(kk_s), to_seq(vv_s), to_seq(w_s), to_seq(a_s), to_seq(b_s),
                           state_wkv[l], t_s)
    y_s = y_s.reshape(1, n_s * t_s, WIDTH)

    out_w = (row(lnx_w[l]), row(lnx_b[l]), row(g_ffn[l]), w_out[l].astype(BF16), w_router[l], row(b_router[l]))
    x1_p, h2_p, ti_p, tw_p, rk_p, cnt_p = _outproj(x_prompt, of_p, y_p, bonus_p, g_p, mp[2], mp[4], mp[3],
                                                   *out_w, tm_p)
    x1_s, h2_s, ti_s, tw_s, rk_s, cnt_s = _outproj(xs3, of_s, y_s, bonus_s, g_s, msm[2], msm[4], msm[3],
                                                   *out_w, tm_s)

    experts_w = (w_gate[l], b_gate[l].reshape(N_EXPERTS, 1, d), w_up[l], b_up[l].reshape(N_EXPERTS, 1, d),
                 w_down[l], b_down[l].reshape(N_EXPERTS, 1, d))
    take_rows = lambda a, idx: a.at[idx].get(mode="promise_in_bounds")

    def moe(h2, top_i, rank, counts, seg_tokens):
        n_tok = sum(seg_tokens)
        slot_of_assign, token_of_slot, tile_expert, tile_flag = _dispatch_plan(
            top_i.reshape(n_tok, TOP_K), rank.reshape(n_tok, TOP_K),
            counts.reshape(len(seg_tokens), N_EXPERTS).astype(jnp.int32), seg_tokens)
        ys = _moe_gemm(tile_expert, tile_flag, take_rows(h2.reshape(n_tok, d), token_of_slot), *experts_w)
        return take_rows(ys, slot_of_assign.T)

    yg_p = moe(h2_p, ti_p, rk_p, cnt_p, [t_p] * n_p)
    yg_s = moe(h2_s, ti_s, rk_s, cnt_s, [n_s * t_s])
    y_prompt = _final(x1_p, yg_p.reshape(TOP_K, n_p, t_p, d), tw_p, mp[5], tm_p)
    y_sample = _final(x1_s, yg_s.reshape(TOP_K, 1, n_s * t_s, d), tw_s, msm[5], tm_s)

    heads5 = lambda a, nb, nt: a.reshape(1, nb, nt, N_HEADS, HEAD_DIM)
    return (y_prompt, y_sample.reshape(n_s, t_s, d),
            heads5(k_p, n_p, t_p), heads5(v_p, n_p, t_p), lf_p[None], wkv_p[None], rw_p[:, -1][None],
            heads5(k_s, n_s, t_s), heads5(v_s, n_s, t_s), lf_s.reshape(1, n_s, t_s, N_HEADS), wkv_s[None],
            rw_s4[:, -1][None])
```

```python
import functools

import jax
import jax.numpy as jnp
from jax import lax
from jax.experimental import pallas as pl
from jax.experimental.pallas import tpu as pltpu

F32 = jnp.float32
BF16 = jnp.bfloat16

D_MODEL = 1024
HEAD_DIM = 64
N_HEADS = 8
WIDTH = N_HEADS * HEAD_DIM
PAIR = 2 * HEAD_DIM
N_HEAD_PAIRS = N_HEADS // 2
DECAY_LORA = 64
ICLR_LORA = 64
GATE_LORA = 128
FOX_COLS = 3 * WIDTH + N_HEADS
RWKV_COLS = 3 * WIDTH + DECAY_LORA + ICLR_LORA + GATE_LORA
IN_COLS_PADDED = 3 * WIDTH + RWKV_COLS + 128
FZ_OFF = 3 * WIDTH + RWKV_COLS
ATTN_SCALE = HEAD_DIM ** -0.5
N_EXPERTS = 32
TOP_K = 4
SWIGLU_ALPHA = 1.702
SWIGLU_LIMIT = 7.0
RMS_EPS = 1e-6
GN_EPS = 64e-5
N_MOD = 6
PAGE = 128
NEG = -1e30

VMEM_LIMIT = 56 * 1024 * 1024


def _cparams(*sem):
    return pltpu.CompilerParams(dimension_semantics=sem, vmem_limit_bytes=VMEM_LIMIT)


def _dot(a, b):
    return jnp.dot(a, b, preferred_element_type=F32)


def _dot_nt(a, b):
    return lax.dot_general(a, b, (((1,), (1,)), ((), ())), preferred_element_type=F32)


def _split2(x):
    hi = x.astype(BF16)
    lo = (x - hi.astype(F32)).astype(BF16)
    return hi, lo


def _split3(x):
    hi = x.astype(BF16)
    r = x - hi.astype(F32)
    mid = r.astype(BF16)
    lo = (r - mid.astype(F32)).astype(BF16)
    return hi, mid, lo


def _dot_x3(a, b):
    ah, al = _split2(a)
    bh, bl = _split2(b)
    return _dot(ah, bh) + (_dot(ah, bl) + _dot(al, bh))


def _dot_sel2(a, sel):
    hi, lo = _split2(a)
    return _dot(hi, sel) + _dot(lo, sel)


def _dot_sel3(a, sel):
    hi, mid, lo = _split3(a)
    return _dot(hi, sel) + (_dot(mid, sel) + _dot(lo, sel))


def _sigmoid(x):
    return 1.0 / (1.0 + jnp.exp(-x))


def _softplus(x):
    return jnp.maximum(x, 0.0) + jnp.log(1.0 + jnp.exp(-jnp.abs(x)))


def _head_ones(n_heads):
    w = n_heads * HEAD_DIM
    r = lax.broadcasted_iota(jnp.int32, (w, w), 0) // HEAD_DIM
    c = lax.broadcasted_iota(jnp.int32, (w, w), 1) // HEAD_DIM
    return (r == c).astype(BF16)


def _mod_kernel(c_ref, w_ref, b_ref, o_ref):
    c = c_ref[...]
    s = c * _sigmoid(c)
    o_ref[...] = _dot_x3(s, w_ref[...]) + b_ref[...]


def _mod(c, w_mod, b_mod):
    rows, d = c.shape
    n = w_mod.shape[1]
    tn = 768
    return pl.pallas_call(
        _mod_kernel,
        grid=(n // tn,),
        in_specs=[pl.BlockSpec((rows, d), lambda j: (0, 0)),
                  pl.BlockSpec((d, tn), lambda j: (0, j)),
                  pl.BlockSpec((1, tn), lambda j: (0, j))],
        out_specs=pl.BlockSpec((rows, tn), lambda j: (0, j)),
        out_shape=jax.ShapeDtypeStruct((rows, n), F32),
        compiler_params=_cparams("arbitrary"),
    )(c, w_mod, b_mod.reshape(1, n))


def _inproj_kernel(x_ref, sc_ref, sh_ref, g_ref, w_ref, wfz_ref, gq_ref, gk_ref, bf_ref, bfc_ref, hs_ref,
                   qh_ref, kh_ref, vh_ref, k_ref, v_ref, lf_ref, lft_ref, rw_ref):
    x = x_ref[0]
    ms = jnp.mean(x * x, axis=-1, keepdims=True)
    xn = x * lax.rsqrt(ms + RMS_EPS) * g_ref[...]
    h = (xn * (1.0 + sc_ref[0]) + sh_ref[0]).astype(BF16)
    proj = _dot(h, w_ref[...])
    hs = hs_ref[...]

    def head_norm(t, gain):
        m = _dot_sel2(t * t, hs) * (1.0 / HEAD_DIM)
        return t * lax.rsqrt(m + RMS_EPS) * gain

    q = head_norm(proj[:, 0:WIDTH], gq_ref[...])
    k = head_norm(proj[:, WIDTH:2 * WIDTH], gk_ref[...])
    v = proj[:, 2 * WIDTH:3 * WIDTH]
    k_ref[0] = k
    v_ref[0] = v
    qb = q.astype(BF16)
    kb = k.astype(BF16)
    vb = v.astype(BF16)
    for hp in range(N_HEAD_PAIRS):
        sl = slice(hp * PAIR, (hp + 1) * PAIR)
        qh_ref[0, hp] = qb[:, sl]
        kh_ref[0, hp] = kb[:, sl]
        vh_ref[0, hp] = vb[:, sl]
    rw_ref[0] = proj[:, 3 * WIDTH:3 * WIDTH + RWKV_COLS]
    fz = proj[:, FZ_OFF:FZ_OFF + N_HEADS] + bf_ref[...]
    lf_ref[0] = -_softplus(-fz)
    fzt = _dot_nt(wfz_ref[...], h) + bfc_ref[...]
    lft_ref[0] = -_softplus(-fzt)


def _inproj(x3, sc, sh, g_mix, w_r, wfz_t, gq, gk, b_forget, tm):
    bx, tx, d = x3.shape
    per_row = sc.shape[1] != 1
    mod_spec = (pl.BlockSpec((1, tm, d), lambda b, t: (b, t, 0)) if per_row
                else pl.BlockSpec((1, 1, d), lambda b, t: (b, 0, 0)))
    const = lambda shape: pl.BlockSpec(shape, lambda b, t: tuple(0 for _ in shape))
    head_spec = pl.BlockSpec((1, N_HEAD_PAIRS, tm, PAIR), lambda b, t: (b, 0, t, 0))
    row_spec = lambda w: pl.BlockSpec((1, tm, w), lambda b, t: (b, t, 0))
    head_shape = jax.ShapeDtypeStruct((bx, N_HEAD_PAIRS, tx, PAIR), BF16)
    return pl.pallas_call(
        _inproj_kernel,
        grid=(bx, tx // tm),
        in_specs=[row_spec(d), mod_spec, mod_spec, const((1, d)), const((d, IN_COLS_PADDED)),
                  const((N_HEADS, d)), const((1, WIDTH)), const((1, WIDTH)), const((1, N_HEADS)),
                  const((N_HEADS, 1)), const((WIDTH, WIDTH))],
        out_specs=[head_spec, head_spec, head_spec, row_spec(WIDTH), row_spec(WIDTH), row_spec(N_HEADS),
                   pl.BlockSpec((1, N_HEADS, tm), lambda b, t: (b, 0, t)), row_spec(RWKV_COLS)],
        out_shape=[head_shape, head_shape, head_shape,
                   jax.ShapeDtypeStruct((bx, tx, WIDTH), F32), jax.ShapeDtypeStruct((bx, tx, WIDTH), F32),
                   jax.ShapeDtypeStruct((bx, tx, N_HEADS), F32), jax.ShapeDtypeStruct((bx, N_HEADS, tx), F32),
                   jax.ShapeDtypeStruct((bx, tx, RWKV_COLS), F32)],
        compiler_params=_cparams("parallel", "arbitrary"),
    )(x3, sc, sh, g_mix, w_r, wfz_t, gq, gk, b_forget.reshape(1, N_HEADS), b_forget.reshape(N_HEADS, 1),
      _head_ones(N_HEADS))


def _cumsum_kernel(l_ref, u_ref, o_ref, c_ref):
    tc = l_ref.shape[2]

    @pl.when(pl.program_id(1) == 0)
    def _():
        c_ref[...] = jnp.zeros_like(c_ref)

    f = _dot_sel3(l_ref[0], u_ref[...]) + c_ref[:, 0:1]
    o_ref[0] = f
    c_ref[...] = jnp.broadcast_to(f[:, tc - 1:tc], c_ref.shape)


def _cumsum_t(lft, tc):
    b, h, t = lft.shape
    r = lax.broadcasted_iota(jnp.int32, (tc, tc), 0)
    c = lax.broadcasted_iota(jnp.int32, (tc, tc), 1)
    upper = (r <= c).astype(BF16)
    return pl.pallas_call(
        _cumsum_kernel,
        grid=(b, t // tc),
        in_specs=[pl.BlockSpec((1, h, tc), lambda i, j: (i, 0, j)), pl.BlockSpec((tc, tc), lambda i, j: (0, 0))],
        out_specs=pl.BlockSpec((1, h, tc), lambda i, j: (i, 0, j)),
        out_shape=jax.ShapeDtypeStruct((b, h, t), F32),
        scratch_shapes=[pltpu.VMEM((h, 128), F32)],
        compiler_params=_cparams("parallel", "arbitrary"),
    )(lft, upper)


def _attn_kernel(q_ref, k_ref, v_ref, f_ref, o_ref, m_ref, acc_ref, *, tq, tk):
    i = pl.program_id(2)
    q2 = q_ref[0, 0]
    lane = lax.broadcasted_iota(jnp.int32, q2.shape, 1)
    zero = jnp.zeros_like(q2)
    qs = (jnp.where(lane < HEAD_DIM, q2, zero), jnp.where(lane >= HEAD_DIM, q2, zero))
    q0 = pl.multiple_of(i * tq, tq)
    f_base = [f_ref[0, 0, hh:hh + 1, pl.ds(q0, tq)][:, 0:1] for hh in range(2)]
    m_ref[...] = jnp.full_like(m_ref, NEG)
    acc_ref[...] = jnp.zeros_like(acc_ref)

    def tile(j, masked):
        k0 = pl.multiple_of(j * tk, tk)
        k2 = k_ref[0, 0, pl.ds(k0, tk), :]
        v2 = v_ref[0, 0, pl.ds(k0, tk), :]
        vlane = lax.broadcasted_iota(jnp.int32, v2.shape, 1)
        one = jnp.ones((), v2.dtype)
        vs = (jnp.where(vlane < HEAD_DIM, v2, one), jnp.where(vlane >= HEAD_DIM, v2, one))
        for hh in range(2):
            s = _dot_nt(qs[hh], k2) + (f_base[hh] - f_ref[0, 0, hh:hh + 1, pl.ds(k0, tk)])
            if masked:
                r = lax.broadcasted_iota(jnp.int32, s.shape, 0)
                c = lax.broadcasted_iota(jnp.int32, s.shape, 1)
                s = jnp.where(c + k0 <= r + q0, s, NEG)
            m_old = m_ref[hh]
            m_new = jnp.maximum(m_old, jnp.max(s, axis=1, keepdims=True))
            p = jnp.exp(s - jnp.concatenate([m_new] * (tk // 128), axis=1))
            alpha = jnp.exp(m_old - m_new)
            acc_ref[hh] = alpha * acc_ref[hh] + _dot(p.astype(BF16), vs[hh])
            m_ref[hh] = m_new

    def body(j, carry):
        tile(j, False)
        return carry

    n_sub = tq // tk
    lax.fori_loop(0, i * n_sub, body, 0)
    for dd in range(n_sub):
        tile(i * n_sub + dd, True)
    acc_a = acc_ref[0]
    acc_b = acc_ref[1]
    o_a = acc_a / pltpu.roll(acc_a, HEAD_DIM, 1)
    o_b = acc_b / pltpu.roll(acc_b, HEAD_DIM, 1)
    o_ref[0, 0] = jnp.where(lane < HEAD_DIM, o_a, o_b).astype(o_ref.dtype)


def _fox_prompt(qh, kh, vh, ft, tq, tk):
    b, hp, t, w = qh.shape
    assert tq % tk == 0 and tk % 128 == 0
    ft4 = ft.reshape(b, hp, 2, t)
    seq_spec = pl.BlockSpec((1, 1, t, w), lambda bi, hi, i: (bi, hi, 0, 0))
    tile_spec = pl.BlockSpec((1, 1, tq, w), lambda bi, hi, i: (bi, hi, i, 0))
    return pl.pallas_call(
        functools.partial(_attn_kernel, tq=tq, tk=tk),
        grid=(b, hp, t // tq),
        in_specs=[tile_spec, seq_spec, seq_spec, pl.BlockSpec((1, 1, 2, t), lambda bi, hi, i: (bi, hi, 0, 0))],
        out_specs=tile_spec,
        out_shape=jax.ShapeDtypeStruct((b, hp, t, w), BF16),
        scratch_shapes=[pltpu.VMEM((2, tq, 128), F32), pltpu.VMEM((2, tq, w), F32)],
        compiler_params=_cparams("parallel", "parallel", "arbitrary"),
    )(qh, kh, vh, ft4)


PAGES_PER_STEP = 8
QROWS = 4 * N_HEADS


def _decode_kernel(pt_ref, q_ref, kn_ref, vn_ref, lfn_ref, cnew_ref, ustrict_ref, fold_ref, *refs, n_steps):
    del pt_ref
    npg = PAGES_PER_STEP
    k_refs = refs[0:npg]
    v_refs = refs[npg:2 * npg]
    lf_refs = refs[2 * npg:3 * npg]
    o_ref, m_ref, l_ref, acc_ref, accn_ref, carry_ref = refs[3 * npg:]
    j = pl.program_id(1)
    q = q_ref[0]
    r = lax.broadcasted_iota(jnp.int32, (QROWS, WIDTH), 0)
    c = lax.broadcasted_iota(jnp.int32, (QROWS, WIDTH), 1)
    own_head = (r % N_HEADS) == (c // HEAD_DIM)
    q_bd = jnp.where(own_head, jnp.concatenate([q] * N_HEADS, axis=1), jnp.zeros((), q.dtype))

    def softmax_step(s):
        m_old = m_ref[...]
        m_new = jnp.maximum(m_old, jnp.max(s, axis=1, keepdims=True))
        p = jnp.exp(s - m_new)
        alpha = jnp.exp(m_old - m_new)
        l_ref[...] = alpha * l_ref[...] + jnp.sum(p, axis=1, keepdims=True)
        m_ref[...] = m_new
        return p.astype(BF16), alpha

    @pl.when(j == 0)
    def _():
        m_ref[...] = jnp.full_like(m_ref, NEG)
        l_ref[...] = jnp.zeros_like(l_ref)
        acc_ref[...] = jnp.zeros_like(acc_ref)
        carry_ref[...] = jnp.zeros_like(carry_ref)
        s = _dot_nt(q, kn_ref[0])
        bias = _dot_sel3(lfn_ref[0], cnew_ref[...])[0:1, :]
        rr = lax.broadcasted_iota(jnp.int32, s.shape, 0)
        cc = lax.broadcasted_iota(jnp.int32, s.shape, 1)
        ok = ((rr % N_HEADS) == (cc % N_HEADS)) & ((cc // N_HEADS) <= (rr // N_HEADS))
        p, _ = softmax_step(jnp.where(ok, s + bias, NEG))
        accn_ref[...] = _dot(p, vn_ref[0])

    ustrict = ustrict_ref[...]
    carry = carry_ref[...]
    scores = []
    for u in range(npg):
        kt = k_refs[u][...].reshape(WIDTH, PAGE).astype(BF16)
        lft = lf_refs[u][...]
        a = jnp.broadcast_to(lft[None], (4, N_HEADS, PAGE)).reshape(QROWS, PAGE)
        hi, mid, lo = _split3(a)
        bx = _dot(jnp.concatenate([hi, mid, lo], axis=0), ustrict)
        bias = bx[0:QROWS] + (bx[QROWS:2 * QROWS] + bx[2 * QROWS:3 * QROWS]) + carry
        carry = carry + jnp.sum(a, axis=1, keepdims=True)
        scores.append(_dot(q_bd, kt) + bias)
    carry_ref[...] = carry
    p, alpha = softmax_step(jnp.concatenate(scores, axis=1))
    acc = alpha * acc_ref[...]
    for u in range(npg):
        vt = v_refs[u][...].reshape(WIDTH, PAGE).astype(BF16)
        acc = acc + _dot_nt(p[:, u * PAGE:(u + 1) * PAGE], vt)
    acc_ref[...] = acc
    accn_ref[...] = alpha * accn_ref[...]

    @pl.when(j == n_steps - 1)
    def _():
        own = jnp.where(own_head, acc_ref[...], 0.0)
        o_ref[0] = (_dot_sel3(own, fold_ref[...]) + accn_ref[...]) / l_ref[...]


def _fox_decode(page_table, q, k_new, v_new, lf_new, cache_kt, cache_vt, cache_lft):
    nb, n_pages = page_table.shape
    npg = PAGES_PER_STEP
    n_steps = n_pages // npg
    ustrict = (lax.broadcasted_iota(jnp.int32, (PAGE, PAGE), 0)
               > lax.broadcasted_iota(jnp.int32, (PAGE, PAGE), 1)).astype(BF16)
    ri = lax.broadcasted_iota(jnp.int32, (QROWS, QROWS), 0)
    ci = lax.broadcasted_iota(jnp.int32, (QROWS, QROWS), 1)
    cnew = -(((ri % N_HEADS) == (ci % N_HEADS)) & ((ri // N_HEADS) <= (ci // N_HEADS))).astype(BF16)
    fold = (lax.broadcasted_iota(jnp.int32, (WIDTH, HEAD_DIM), 0) % HEAD_DIM
            == lax.broadcasted_iota(jnp.int32, (WIDTH, HEAD_DIM), 1)).astype(BF16)

    def page_idx(u):
        return lambda b, j, pt: (pt[b, n_pages - 1 - (j * npg + u)], 0, 0, 0)

    def page_idx3(u):
        return lambda b, j, pt: (pt[b, n_pages - 1 - (j * npg + u)], 0, 0)

    per_b = lambda shape: pl.BlockSpec((1,) + shape, lambda b, j, pt: (b, 0, 0))
    const = lambda shape: pl.BlockSpec(shape, lambda b, j, pt: (0, 0))
    in_specs = ([per_b((QROWS, HEAD_DIM)), per_b((QROWS, HEAD_DIM)), per_b((QROWS, HEAD_DIM)),
                 per_b((8, QROWS)), const((QROWS, QROWS)), const((PAGE, PAGE)), const((WIDTH, HEAD_DIM))]
                + [pl.BlockSpec((None, N_HEADS, HEAD_DIM, PAGE), page_idx(u)) for u in range(npg)]
                + [pl.BlockSpec((None, N_HEADS, HEAD_DIM, PAGE), page_idx(u)) for u in range(npg)]
                + [pl.BlockSpec((None, N_HEADS, PAGE), page_idx3(u)) for u in range(npg)])
    grid_spec = pltpu.PrefetchScalarGridSpec(
        num_scalar_prefetch=1,
        grid=(nb, n_steps),
        in_specs=in_specs,
        out_specs=pl.BlockSpec((1, QROWS, HEAD_DIM), lambda b, j, pt: (b, 0, 0)),
        scratch_shapes=[pltpu.VMEM((QROWS, 1), F32), pltpu.VMEM((QROWS, 1), F32),
                        pltpu.VMEM((QROWS, WIDTH), F32), pltpu.VMEM((QROWS, HEAD_DIM), F32),
                        pltpu.VMEM((QROWS, 1), F32)],
    )
    lfn = jnp.broadcast_to(lf_new.reshape(nb, 1, QROWS), (nb, 8, QROWS))
    return pl.pallas_call(
        functools.partial(_decode_kernel, n_steps=n_steps),
        grid_spec=grid_spec,
        out_shape=jax.ShapeDtypeStruct((nb, QROWS, HEAD_DIM), F32),
        compiler_params=_cparams("parallel", "arbitrary"),
    )(page_table, q, k_new, v_new, lfn, cnew, ustrict, fold,
      *([cache_kt] * npg), *([cache_vt] * npg), *([cache_lft] * npg))


def _rwkv_prep_kernel(p_ref, prev_ref, first_ref, mu_ref, w0_ref, w2_ref, a0_ref, a2_ref, g2_ref, kk_ref, ka_ref,
                      rk_ref, hs_ref, r_ref, k_ref, v_ref, w_ref, a_ref, b_ref, g_ref, bonus_ref, *, shift_rows):
    p = p_ref[0]
    if shift_rows:
        carry = jnp.where(pl.program_id(1) == 0, first_ref[0], prev_ref[0, 7:8, :])
        row0 = lax.broadcasted_iota(jnp.int32, p.shape, 0) == 0
        prev = jnp.where(row0, carry, pltpu.roll(p, 1, 0))
    else:
        prev = prev_ref[0]
    z = p + (prev - p) * mu_ref[...]
    r = z[:, 0:WIDTH]
    k = z[:, WIDTH:2 * WIDTH]
    v = z[:, 2 * WIDTH:3 * WIDTH]
    lora = z[:, 3 * WIDTH:3 * WIDTH + DECAY_LORA + ICLR_LORA]
    gd = z[:, 3 * WIDTH + DECAY_LORA + ICLR_LORA:RWKV_COLS]
    hs = hs_ref[...]
    w = -_softplus(-(w0_ref[...] + _dot_x3(jnp.tanh(lora), w2_ref[...]))) - 0.5
    decay = jnp.exp(-jnp.exp(w))
    a = _sigmoid(a0_ref[...] + _dot_x3(lora, a2_ref[...]))
    g = _dot_x3(_sigmoid(gd), g2_ref[...])
    kk = k * kk_ref[...]
    norm = jnp.sqrt(_dot_sel3(kk * kk, hs))
    kk = kk / jnp.maximum(norm, 1e-12)
    k = k * (1.0 + (a - 1.0) * ka_ref[...])
    bonus = _dot_sel3(r * k * rk_ref[...], hs) * v
    r_ref[0] = r
    k_ref[0] = k
    v_ref[0] = v
    w_ref[0] = decay
    a_ref[0] = -kk
    b_ref[0] = kk * a
    g_ref[0] = g
    bonus_ref[0] = bonus


def _rwkv_prep(p_rw, prev, first, mu, w0, w2p, a0, a2p, g2, kk, ka, rk, tm):
    bx, tx, _ = p_rw.shape
    row = lambda w: pl.BlockSpec((1, tm, w), lambda b, t: (b, t, 0))
    const = lambda shape: pl.BlockSpec(shape, lambda b, t: tuple(0 for _ in shape))
    out = jax.ShapeDtypeStruct((bx, tx, WIDTH), F32)
    shift_rows = prev is None
    if shift_rows:
        prev = p_rw
        prev_spec = pl.BlockSpec((1, 8, RWKV_COLS), lambda b, t: (b, jnp.maximum(t * (tm // 8) - 1, 0), 0))
    else:
        prev_spec = row(RWKV_COLS)
    return pl.pallas_call(
        functools.partial(_rwkv_prep_kernel, shift_rows=shift_rows),
        grid=(bx, tx // tm),
        in_specs=[row(RWKV_COLS), prev_spec, pl.BlockSpec((1, 1, RWKV_COLS), lambda b, t: (b, 0, 0)),
                  const((1, RWKV_COLS)), const((1, WIDTH)),
                  const((DECAY_LORA + ICLR_LORA, WIDTH)), const((1, WIDTH)), const((DECAY_LORA + ICLR_LORA, WIDTH)),
                  const((GATE_LORA, WIDTH)), const((1, WIDTH)), const((1, WIDTH)), const((1, WIDTH)),
                  const((WIDTH, WIDTH))],
        out_specs=[row(WIDTH)] * 8,
        out_shape=[out] * 8,
        compiler_params=_cparams("parallel", "arbitrary"),
    )(p_rw, prev, first, mu, w0, w2p, a0, a2p, g2, kk, ka, rk, _head_ones(N_HEADS))


WKV_NB = 2
WKV_HG = 4
WKV_LANES = WKV_HG * HEAD_DIM
WKV_GROUPS = N_HEADS // WKV_HG
WKV_ROWS = WKV_NB * WKV_GROUPS * HEAD_DIM


def _wkv_kernel(r_ref, k_ref, v_ref, w_ref, a_ref, b_ref, s0_ref, ones_ref, y_ref, sT_ref, st_ref, *, tc, n_t):
    ti = pl.program_id(1)
    blocks = [(bi, gg) for bi in range(WKV_NB) for gg in range(WKV_GROUPS)]

    @pl.when(ti == 0)
    def _():
        for n, (bi, gg) in enumerate(blocks):
            st_ref[n * HEAD_DIM:(n + 1) * HEAD_DIM, :] = jnp.concatenate(
                [s0_ref[bi, gg * WKV_HG + q] for q in range(WKV_HG)], axis=1)

    rows = lax.broadcasted_iota(jnp.int32, (WKV_ROWS, WKV_LANES), 0)
    lanes = lax.broadcasted_iota(jnp.int32, (WKV_ROWS, WKV_LANES), 1)
    eye = (rows % HEAD_DIM) == (lanes % HEAD_DIM)
    eye_f = eye.astype(F32)
    ones = ones_ref[...]
    grp = min(tc, 8)

    def bcast(xs, u):
        return jnp.concatenate(
            [jnp.broadcast_to(xs[bi][u:u + 1, gg * WKV_LANES:(gg + 1) * WKV_LANES], (HEAD_DIM, WKV_LANES))
             for bi, gg in blocks], axis=0)

    def group(gi, carry):
        t0 = pl.multiple_of(gi * grp, grp)
        load = lambda ref: [ref[bi, pl.ds(t0, grp), :] for bi in range(WKV_NB)]
        r8, k8, v8, w8, a8, b8 = (load(ref) for ref in (r_ref, k_ref, v_ref, w_ref, a_ref, b_ref))
        vdiag = lambda u: jnp.where(eye, bcast(v8, u), 0.0).astype(BF16)
        s = st_ref[...]
        ys = []

        def y_rows(yb):
            yb = yb * eye_f
            return [jnp.sum(yb[n * HEAD_DIM:(n + 1) * HEAD_DIM], axis=0, keepdims=True) for n in range(len(blocks))]

        vcol = _dot(vdiag(0), ones)
        ybf = None
        for u in range(grp):
            sa = _dot((s * bcast(a8, u)).astype(BF16), ones)
            side = [vdiag(u + 1)] if u + 1 < grp else []
            if ybf is not None:
                side.append(ybf)
            misc = _dot(jnp.concatenate(side, axis=0), ones) if side else None
            s = s * bcast(w8, u) + sa * bcast(b8, u) + vcol * bcast(k8, u)
            if misc is not None:
                off = 0
                if u + 1 < grp:
                    vcol = misc[0:WKV_ROWS]
                    off = WKV_ROWS
                if ybf is not None:
                    ys.append(y_rows(misc[off:off + WKV_ROWS]))
            ybf = (s * bcast(r8, u)).astype(BF16)
        ys.append(y_rows(_dot(ybf, ones)))
        st_ref[...] = s
        for bi in range(WKV_NB):
            y_ref[bi, pl.ds(t0, grp), :] = jnp.concatenate(
                [jnp.concatenate([ys[u][bi * WKV_GROUPS + gg] for gg in range(WKV_GROUPS)], axis=1)
                 for u in range(grp)], axis=0)
        return carry

    lax.fori_loop(0, tc // grp, group, 0)

    @pl.when(ti == n_t - 1)
    def _():
        for n, (bi, gg) in enumerate(blocks):
            for q in range(WKV_HG):
                sT_ref[bi, gg * WKV_HG + q] = st_ref[n * HEAD_DIM:(n + 1) * HEAD_DIM, q * HEAD_DIM:(q + 1) * HEAD_DIM]


def _wkv_scan(r, k, v, w, a, b, s0, tc):
    bx, tx, _ = r.shape
    assert bx % WKV_NB == 0
    n_t = tx // tc
    seq = pl.BlockSpec((WKV_NB, tc, WIDTH), lambda i, t: (i, t, 0))
    st = pl.BlockSpec((WKV_NB, N_HEADS, HEAD_DIM, HEAD_DIM), lambda i, t: (i, 0, 0, 0))
    ri = lax.broadcasted_iota(jnp.int32, (WKV_LANES, WKV_LANES), 0) // HEAD_DIM
    ci = lax.broadcasted_iota(jnp.int32, (WKV_LANES, WKV_LANES), 1) // HEAD_DIM
    ones = (ri == ci).astype(BF16)
    return pl.pallas_call(
        functools.partial(_wkv_kernel, tc=tc, n_t=n_t),
        grid=(bx // WKV_NB, n_t),
        in_specs=[seq] * 6 + [st, pl.BlockSpec((WKV_LANES, WKV_LANES), lambda i, t: (0, 0))],
        out_specs=[seq, st],
        out_shape=[jax.ShapeDtypeStruct((bx, tx, WIDTH), F32),
                   jax.ShapeDtypeStruct((bx, N_HEADS, HEAD_DIM, HEAD_DIM), F32)],
        scratch_shapes=[pltpu.VMEM((WKV_ROWS, WKV_LANES), F32)],
        compiler_params=_cparams("parallel", "arbitrary"),
    )(r, k, v, w, a, b, s0, ones)


def _outproj_kernel(x_ref, of_ref, y_ref, bonus_ref, g_ref, gt_ref, sc_ref, sh_ref, lw_ref, lb_ref, gf_ref,
                    wo_ref, wr_ref, br_ref, hs_ref, lt_ref, x1_ref, h2_ref, ti_ref, tw_ref, rk_ref, cnt_ref, run_ref):
    @pl.when(pl.program_id(1) == 0)
    def _():
        run_ref[...] = jnp.zeros_like(run_ref)

    hs = hs_ref[...]
    y = y_ref[0]
    mu = _dot_sel3(y, hs) * (1.0 / HEAD_DIM)
    yc = y - mu
    var = _dot_sel3(yc * yc, hs) * (1.0 / HEAD_DIM)
    o_rw = ((yc * lax.rsqrt(var + GN_EPS)) * lw_ref[...] + lb_ref[...] + bonus_ref[0]) * g_ref[0]
    mixed = _dot(o_rw.astype(BF16), wo_ref[WIDTH:2 * WIDTH, :])
    for hp in range(N_HEAD_PAIRS):
        mixed = mixed + _dot(of_ref[0, hp], wo_ref[hp * PAIR:(hp + 1) * PAIR, :])
    x1 = x_ref[0] + gt_ref[0] * mixed
    x1_ref[0] = x1
    ms = jnp.mean(x1 * x1, axis=-1, keepdims=True)
    h2 = (x1 * lax.rsqrt(ms + RMS_EPS) * gf_ref[...]) * (1.0 + sc_ref[0]) + sh_ref[0]
    h2_ref[0] = h2.astype(BF16)
    logits = _dot_x3(h2, wr_ref[...]) + br_ref[...]
    lane = lax.broadcasted_iota(jnp.int32, logits.shape, 1).astype(F32)
    vals, idxs = [], []
    for _ in range(TOP_K):
        m = jnp.max(logits, axis=1, keepdims=True)
        idx = jnp.min(jnp.where(logits == m, lane, float(N_EXPERTS)), axis=1, keepdims=True)
        vals.append(m)
        idxs.append(idx)
        logits = jnp.where(lane == idx, -jnp.inf, logits)
    e = [jnp.exp(vv - vals[0]) for vv in vals]
    tot = e[0] + e[1] + e[2] + e[3]
    tw_ref[0] = jnp.concatenate(e, axis=1) / tot
    ti_ref[0] = jnp.concatenate(idxs, axis=1).astype(jnp.int32)
    chosen = [lane == idx for idx in idxs]
    multi_hot = jnp.where(chosen[0] | chosen[1] | chosen[2] | chosen[3], 1.0, 0.0)
    before = _dot(lt_ref[...], multi_hot.astype(BF16)) + run_ref[...]
    ranks = [jnp.sum(jnp.where(c, before, 0.0), axis=1, keepdims=True) for c in chosen]
    rk_ref[0] = jnp.concatenate(ranks, axis=1).astype(jnp.int32)
    run_ref[...] = run_ref[...] + jnp.sum(multi_hot, axis=0, keepdims=True)
    cnt_ref[0] = run_ref[...]


def _outproj(x3, of, y, bonus, g, gt, sc, sh, lnx_w, lnx_b, g_ffn, w_out_b, w_router, b_router, tm):
    bx, tx, d = x3.shape
    per_row = gt.shape[1] != 1
    mod_spec = (pl.BlockSpec((1, tm, d), lambda b, t: (b, t, 0)) if per_row
                else pl.BlockSpec((1, 1, d), lambda b, t: (b, 0, 0)))
    row = lambda w: pl.BlockSpec((1, tm, w), lambda b, t: (b, t, 0))
    const = lambda shape: pl.BlockSpec(shape, lambda b, t: tuple(0 for _ in shape))
    lower = (lax.broadcasted_iota(jnp.int32, (tm, tm), 0) > lax.broadcasted_iota(jnp.int32, (tm, tm), 1)).astype(BF16)
    return pl.pallas_call(
        _outproj_kernel,
        grid=(bx, tx // tm),
        in_specs=[row(d), pl.BlockSpec((1, N_HEAD_PAIRS, tm, PAIR), lambda b, t: (b, 0, t, 0)),
                  row(WIDTH), row(WIDTH), row(WIDTH), mod_spec, mod_spec, mod_spec,
                  const((1, WIDTH)), const((1, WIDTH)), const((1, d)), const((2 * WIDTH, d)),
                  const((d, N_EXPERTS)), const((1, N_EXPERTS)), const((WIDTH, WIDTH)), const((tm, tm))],
        out_specs=[row(d), row(d), row(TOP_K), row(TOP_K), row(TOP_K),
                   pl.BlockSpec((1, 1, N_EXPERTS), lambda b, t: (b, 0, 0))],
        out_shape=[jax.ShapeDtypeStruct((bx, tx, d), F32), jax.ShapeDtypeStruct((bx, tx, d), BF16),
                   jax.ShapeDtypeStruct((bx, tx, TOP_K), jnp.int32), jax.ShapeDtypeStruct((bx, tx, TOP_K), F32),
                   jax.ShapeDtypeStruct((bx, tx, TOP_K), jnp.int32), jax.ShapeDtypeStruct((bx, 1, N_EXPERTS), F32)],
        scratch_shapes=[pltpu.VMEM((1, N_EXPERTS), F32)],
        compiler_params=_cparams("parallel", "arbitrary"),
    )(x3, of, y, bonus, g, gt, sc, sh, lnx_w, lnx_b, g_ffn, w_out_b, w_router, b_router, _head_ones(N_HEADS), lower)


MOE_TM = 256


def _moe_kernel(te_ref, tf_ref, x_ref, wg_ref, bg_ref, wu_ref, bu_ref, wd_ref, bd_ref, o_ref, wb_ref):
    del te_ref
    i = pl.program_id(0)

    @pl.when(tf_ref[i] == 2)
    def _():
        wb_ref[0] = wg_ref[0].astype(BF16)
        wb_ref[1] = wu_ref[0].astype(BF16)
        wb_ref[2] = wd_ref[0].astype(BF16)

    @pl.when(tf_ref[i] != 0)
    def _():
        x = x_ref[...]
        glu = jnp.minimum(_dot(x, wb_ref[0]) + bg_ref[0], SWIGLU_LIMIT)
        lin = jnp.clip(_dot(x, wb_ref[1]) + bu_ref[0], -SWIGLU_LIMIT, SWIGLU_LIMIT)
        act = glu * _sigmoid(SWIGLU_ALPHA * glu) * (lin + 1.0)
        o_ref[...] = (_dot(act.astype(BF16), wb_ref[2]) + bd_ref[0]).astype(o_ref.dtype)

    @pl.when(tf_ref[i] == 0)
    def _():
        o_ref[...] = jnp.zeros_like(o_ref)


def _moe_gemm(tile_expert, tile_flag, xg, wg, bg, wu, bu, wd, bd):
    n_slots, d = xg.shape
    n_tiles = n_slots // MOE_TM
    wspec = pl.BlockSpec((1, d, d), lambda i, te, tf: (te[i], 0, 0))
    bspec = pl.BlockSpec((1, 1, d), lambda i, te, tf: (te[i], 0, 0))
    grid_spec = pltpu.PrefetchScalarGridSpec(
        num_scalar_prefetch=2,
        grid=(n_tiles,),
        in_specs=[pl.BlockSpec((MOE_TM, d), lambda i, te, tf: (i, 0)), wspec, bspec, wspec, bspec, wspec, bspec],
        out_specs=pl.BlockSpec((MOE_TM, d), lambda i, te, tf: (i, 0)),
        scratch_shapes=[pltpu.VMEM((3, d, d), BF16)],
    )
    return pl.pallas_call(
        _moe_kernel,
        grid_spec=grid_spec,
        out_shape=jax.ShapeDtypeStruct((n_slots, d), BF16),
        compiler_params=_cparams("arbitrary"),
    )(tile_expert, tile_flag, xg, wg, bg, wu, bu, wd, bd)


def _final_kernel(x1_ref, yg_ref, tw_ref, gt_ref, o_ref):
    tw = tw_ref[0]
    moe = tw[:, 0:1] * yg_ref[0, 0].astype(F32)
    for kk in range(1, TOP_K):
        moe = moe + tw[:, kk:kk + 1] * yg_ref[kk, 0].astype(F32)
    o_ref[0] = x1_ref[0] + gt_ref[0] * moe


def _final(x1, yg, tw, gt, tm):
    bx, tx, d = x1.shape
    per_row = gt.shape[1] != 1
    mod_spec = (pl.BlockSpec((1, tm, d), lambda b, t: (b, t, 0)) if per_row
                else pl.BlockSpec((1, 1, d), lambda b, t: (b, 0, 0)))
    row = pl.BlockSpec((1, tm, d), lambda b, t: (b, t, 0))
    return pl.pallas_call(
        _final_kernel,
        grid=(bx, tx // tm),
        in_specs=[row, pl.BlockSpec((TOP_K, 1, tm, d), lambda b, t: (0, b, t, 0)),
                  pl.BlockSpec((1, tm, TOP_K), lambda b, t: (b, t, 0)), mod_spec],
        out_specs=row,
        out_shape=jax.ShapeDtypeStruct((bx, tx, d), F32),
        compiler_params=_cparams("parallel", "arbitrary"),
    )(x1, yg, tw, gt)


def _dispatch_plan(top_i, rank_in_seg, seg_counts, seg_tokens):
    n_assign = top_i.size
    n_tiles = -(-n_assign // MOE_TM) + N_EXPERTS
    n_slots = n_tiles * MOE_TM
    e_flat = top_i.reshape(-1)
    order = jnp.argsort(e_flat, stable=True).astype(jnp.int32)
    counts = jnp.sum(seg_counts, axis=0)
    cnt_start = jnp.cumsum(counts) - counts
    seg_base = jnp.cumsum(seg_counts, axis=0) - seg_counts
    padded = ((counts + MOE_TM - 1) // MOE_TM) * MOE_TM
    pad_end = jnp.cumsum(padded)
    pad_start = pad_end - padded
    experts = jnp.arange(N_EXPERTS, dtype=jnp.int32)
    tile_start = jnp.arange(n_tiles, dtype=jnp.int32) * MOE_TM
    tile_expert = jnp.minimum(jnp.sum((tile_start[:, None] >= pad_end[None, :]).astype(jnp.int32), axis=1),
                              N_EXPERTS - 1)
    tile_live = tile_start < pad_end[-1]
    is_first = jnp.concatenate([jnp.ones((1,), bool), tile_expert[1:] != tile_expert[:-1]])
    tile_flag = jnp.where(tile_live, jnp.where(is_first, 2, 1), 0).astype(jnp.int32)
    of_tile = lambda tab: jnp.sum(jnp.where(tile_expert[:, None] == experts[None, :], tab[None, :], 0), axis=1)
    rank = (tile_start - of_tile(pad_start))[:, None] + jnp.arange(MOE_TM, dtype=jnp.int32)[None, :]
    live = rank < of_tile(counts)[:, None]
    src = order.at[jnp.clip(of_tile(cnt_start)[:, None] + rank, 0, n_assign - 1)].get(mode="promise_in_bounds")
    token_of_slot = jnp.where(live, src // TOP_K, 0).reshape(n_slots)
    tok_base = jnp.repeat(seg_base + pad_start[None, :], jnp.asarray(seg_tokens), axis=0,
                          total_repeat_length=sum(seg_tokens))
    hit = top_i[:, :, None] == experts[None, None, :]
    slot_of_assign = jnp.sum(jnp.where(hit, tok_base[:, None, :], 0), axis=-1) + rank_in_seg
    return slot_of_assign, token_of_slot, tile_expert, tile_flag


def _pick_tile(n, pref):
    t = min(n, pref)
    assert n % t == 0, (n, t)
    return t


def kernel(x_prompt, x_sample, cache_k, cache_v, cache_logf, state_wkv, state_shift, page_table, c_prompt,
           c_sample, w_mod, b_mod, g_mix, g_ffn, w_in, q_norm, k_norm, b_forget, rw_mu, rw_w0, rw_w2, rw_a0,
           rw_a2, rw_g2, rw_kk, rw_ka, rw_rk, lnx_w, lnx_b, w_out, w_router, b_router, w_gate, b_gate, w_up,
           b_up, w_down, b_down):
    depth = w_mod.shape[0]
    assert depth == 1
    n_p, t_p, d = x_prompt.shape
    n_s, t_s, _ = x_sample.shape
    assert d == D_MODEL and t_s * N_HEADS == QROWS
    l = 0

    w_in_l = w_in[l]
    w_r = jnp.concatenate([w_in_l[:, :3 * WIDTH], w_in_l[:, FOX_COLS:], w_in_l[:, 3 * WIDTH:FOX_COLS],
                           jnp.zeros((d, 128 - N_HEADS), F32)], axis=1).astype(BF16)
    wfz_t = w_in_l[:, 3 * WIDTH:FOX_COLS].T.astype(BF16)
    gq = (jnp.tile(q_norm[l], N_HEADS) * ATTN_SCALE).reshape(1, WIDTH)
    gk = jnp.tile(k_norm[l], N_HEADS).reshape(1, WIDTH)
    zpad = jnp.zeros((DECAY_LORA, WIDTH), F32)
    w2p = jnp.concatenate([rw_w2[l], zpad], axis=0)
    a2p = jnp.concatenate([zpad, rw_a2[l]], axis=0)
    row = lambda a: a.reshape(1, -1)

    n_c = n_p + n_s
    pad_c = (-n_c) % 8
    c_all = jnp.concatenate([c_sample, c_prompt, jnp.zeros((pad_c, d), F32)], axis=0)
    mod = _mod(c_all, w_mod[l], b_mod[l])
    mod_s = mod[:n_s].reshape(n_s, 1, N_MOD, d)
    mod_p = mod[n_s:n_s + n_p].reshape(n_p, 1, N_MOD, d)
    mod_s = jnp.broadcast_to(mod_s, (n_s, t_s, N_MOD, d)).reshape(1, n_s * t_s, N_MOD, d)
    mp = [mod_p[:, :, i] for i in range(N_MOD)]
    msm = [mod_s[:, :, i] for i in range(N_MOD)]

    xs3 = x_sample.reshape(1, n_s * t_s, d)
    tm_p = _pick_tile(t_p, 256)
    tm_s = _pick_tile(n_s * t_s, 256)

    proj_args = (row(g_mix[l]), w_r, wfz_t, gq, gk, b_forget[l])
    qh_p, kh_p, vh_p, k_p, v_p, lf_p, lft_p, rw_p = _inproj(x_prompt, mp[1], mp[0], *proj_args, tm_p)
    qh_s, _, _, k_s, v_s, lf_s, _, rw_s = _inproj(xs3, msm[1], msm[0], *proj_args, tm_s)

    ft = _cumsum_t(lft_p, _pick_tile(t_p, 512))
    tq = _pick_tile(t_p, 1024)
    of_p = _fox_prompt(qh_p, kh_p, vh_p, ft, tq, _pick_tile(tq, 512))

    q_s = jnp.transpose(qh_s[0], (1, 0, 2)).reshape(n_s, QROWS, HEAD_DIM)
    kn_s = k_s.reshape(n_s, QROWS, HEAD_DIM).astype(BF16)
    vn_s = v_s.reshape(n_s, QROWS, HEAD_DIM).astype(BF16)
    cache_kt = jnp.transpose(cache_k[l], (0, 2, 3, 1))
    cache_vt = jnp.transpose(cache_v[l], (0, 2, 3, 1))
    cache_lft = jnp.swapaxes(cache_logf[l], 1, 2)
    lfn_s = lf_s.reshape(n_s, QROWS)
    halves = [slice(0, n_s // 2), slice(n_s // 2, n_s)] if n_s % 2 == 0 else [slice(0, n_s)]
    o_halves = [_fox_decode(page_table[h], q_s[h], kn_s[h], vn_s[h], lfn_s[h], cache_kt, cache_vt, cache_lft)
                for h in halves]

    prep_w = (row(rw_mu[l]), row(rw_w0[l]), w2p, row(rw_a0[l]), a2p, rw_g2[l], row(rw_kk[l]), row(rw_ka[l]),
              row(rw_rk[l]))
    rw_s4 = rw_s.reshape(n_s, t_s, RWKV_COLS)
    prev_s = jnp.concatenate([state_shift[l][:, None, :], rw_s4[:, :-1]], axis=1).reshape(1, n_s * t_s, RWKV_COLS)
    r_p, kk_p, vv_p, w_p, a_p, b_p, g_p, bonus_p = _rwkv_prep(
        rw_p, None, jnp.zeros((n_p, 1, RWKV_COLS), F32), *prep_w, tm_p)
    r_s, kk_s, vv_s, w_s, a_s, b_s, g_s, bonus_s = _rwkv_prep(
        rw_s, prev_s, jnp.zeros((1, 1, RWKV_COLS), F32), *prep_w, tm_s)
    y_p, wkv_p = _wkv_scan(r_p, kk_p, vv_p, w_p, a_p, b_p,
                           jnp.zeros((n_p, N_HEADS, HEAD_DIM, HEAD_DIM), F32), _pick_tile(t_p, 256))
    to_seq = lambda a: a.reshape(n_s, t_s, WIDTH)
    y_s, wkv_s = _wkv_scan(to_seq(r_s), to_seq(kk_s), to_seq(vv_s), to_seq(w_s), to_seq(a_s), to_seq(b_s),
                           state_wkv[l], t_s)
    y_s = y_s.reshape(1, n_s * t_s, WIDTH)

    out_w = (row(lnx_w[l]), row(lnx_b[l]), row(g_ffn[l]), w_out[l].astype(BF16), w_router[l], row(b_router[l]))
    x1_p, h2_p, ti_p, tw_p, rk_p, cnt_p = _outproj(x_prompt, of_p, y_p, bonus_p, g_p, mp[2], mp[4], mp[3],
                                                   *out_w, tm_p)

    experts_w = (w_gate[l], b_gate[l].reshape(N_EXPERTS, 1, d), w_up[l], b_up[l].reshape(N_EXPERTS, 1, d),
                 w_down[l], b_down[l].reshape(N_EXPERTS, 1, d))
    take_rows = lambda a, idx: a.at[idx].get(mode="promise_in_bounds")

    def moe(h2, top_i, rank, counts, seg_tokens, tie=None):
        n_tok = sum(seg_tokens)
        slot_of_assign, token_of_slot, tile_expert, tile_flag = _dispatch_plan(
            top_i.reshape(n_tok, TOP_K), rank.reshape(n_tok, TOP_K),
            counts.reshape(len(seg_tokens), N_EXPERTS).astype(jnp.int32), seg_tokens)
        xg = take_rows(h2.reshape(n_tok, d), token_of_slot)
        if tie is not None:
            tile_flag, tie = lax.optimization_barrier((tile_flag, tie))
        ys = _moe_gemm(tile_expert, tile_flag, xg, *experts_w)
        return take_rows(ys, slot_of_assign.T), tie

    yg_p, o_first = moe(h2_p, ti_p, rk_p, cnt_p, [t_p] * n_p, tie=o_halves[0])
    o_s = jnp.concatenate([o_first] + o_halves[1:], axis=0)
    of_s = jnp.transpose(o_s.reshape(n_s * t_s, N_HEAD_PAIRS, PAIR), (1, 0, 2))[None].astype(BF16)
    x1_s, h2_s, ti_s, tw_s, rk_s, cnt_s = _outproj(xs3, of_s, y_s, bonus_s, g_s, msm[2], msm[4], msm[3],
                                                   *out_w, tm_s)
    yg_s, _ = moe(h2_s, ti_s, rk_s, cnt_s, [n_s * t_s])
    y_prompt = _final(x1_p, yg_p.reshape(TOP_K, n_p, t_p, d), tw_p, mp[5], tm_p)
    y_sample = _final(x1_s, yg_s.reshape(TOP_K, 1, n_s * t_s, d), tw_s, msm[5], tm_s)

    heads5 = lambda a, nb, nt: a.reshape(1, nb, nt, N_HEADS, HEAD_DIM)
    return (y_prompt, y_sample.reshape(n_s, t_s, d),
            heads5(k_p, n_p, t_p), heads5(v_p, n_p, t_p), lf_p[None], wkv_p[None], rw_p[:, -1][None],
            heads5(k_s, n_s, t_s), heads5(v_s, n_s, t_s), lf_s.reshape(1, n_s, t_s, N_HEADS), wkv_s[None],
            rw_s4[:, -1][None])
```

```python
import functools

import jax
import jax.numpy as jnp
from jax import lax
from jax.experimental import pallas as pl
from jax.experimental.pallas import tpu as pltpu

F32 = jnp.float32
BF16 = jnp.bfloat16

D_MODEL = 1024
HEAD_DIM = 64
N_HEADS = 8
WIDTH = N_HEADS * HEAD_DIM
PAIR = 2 * HEAD_DIM
N_HEAD_PAIRS = N_HEADS // 2
DECAY_LORA = 64
ICLR_LORA = 64
GATE_LORA = 128
FOX_COLS = 3 * WIDTH + N_HEADS
RWKV_COLS = 3 * WIDTH + DECAY_LORA + ICLR_LORA + GATE_LORA
IN_COLS_PADDED = 3 * WIDTH + RWKV_COLS + 128
FZ_OFF = 3 * WIDTH + RWKV_COLS
ATTN_SCALE = HEAD_DIM ** -0.5
N_EXPERTS = 32
TOP_K = 4
SWIGLU_ALPHA = 1.702
SWIGLU_LIMIT = 7.0
RMS_EPS = 1e-6
GN_EPS = 64e-5
N_MOD = 6
PAGE = 128
NEG = -1e30

VMEM_LIMIT = 56 * 1024 * 1024


def _cparams(*sem):
    return pltpu.CompilerParams(dimension_semantics=sem, vmem_limit_bytes=VMEM_LIMIT)


def _dot(a, b):
    return jnp.dot(a, b, preferred_element_type=F32)


def _dot_nt(a, b):
    return lax.dot_general(a, b, (((1,), (1,)), ((), ())), preferred_element_type=F32)


def _split2(x):
    hi = x.astype(BF16)
    lo = (x - hi.astype(F32)).astype(BF16)
    return hi, lo


def _split3(x):
    hi = x.astype(BF16)
    r = x - hi.astype(F32)
    mid = r.astype(BF16)
    lo = (r - mid.astype(F32)).astype(BF16)
    return hi, mid, lo


def _dot_x3(a, b):
    ah, al = _split2(a)
    bh, bl = _split2(b)
    return _dot(ah, bh) + (_dot(ah, bl) + _dot(al, bh))


def _dot_sel2(a, sel):
    hi, lo = _split2(a)
    return _dot(hi, sel) + _dot(lo, sel)


def _dot_sel3(a, sel):
    hi, mid, lo = _split3(a)
    return _dot(hi, sel) + (_dot(mid, sel) + _dot(lo, sel))


def _sigmoid(x):
    return 1.0 / (1.0 + jnp.exp(-x))


def _softplus(x):
    return jnp.maximum(x, 0.0) + jnp.log(1.0 + jnp.exp(-jnp.abs(x)))


def _head_ones(n_heads):
    w = n_heads * HEAD_DIM
    r = lax.broadcasted_iota(jnp.int32, (w, w), 0) // HEAD_DIM
    c = lax.broadcasted_iota(jnp.int32, (w, w), 1) // HEAD_DIM
    return (r == c).astype(BF16)


def _mod_kernel(c_ref, w_ref, b_ref, o_ref):
    c = c_ref[...]
    s = c * _sigmoid(c)
    o_ref[...] = _dot_x3(s, w_ref[...]) + b_ref[...]


def _mod(c, w_mod, b_mod):
    rows, d = c.shape
    n = w_mod.shape[1]
    tn = 768
    return pl.pallas_call(
        _mod_kernel,
        grid=(n // tn,),
        in_specs=[pl.BlockSpec((rows, d), lambda j: (0, 0)),
                  pl.BlockSpec((d, tn), lambda j: (0, j)),
                  pl.BlockSpec((1, tn), lambda j: (0, j))],
        out_specs=pl.BlockSpec((rows, tn), lambda j: (0, j)),
        out_shape=jax.ShapeDtypeStruct((rows, n), F32),
        compiler_params=_cparams("arbitrary"),
    )(c, w_mod, b_mod.reshape(1, n))


def _inproj_kernel(x_ref, sc_ref, sh_ref, g_ref, w_ref, wfz_ref, gq_ref, gk_ref, bf_ref, bfc_ref, hs_ref,
                   qh_ref, kh_ref, vh_ref, k_ref, v_ref, lf_ref, lft_ref, rw_ref):
    x = x_ref[0]
    ms = jnp.mean(x * x, axis=-1, keepdims=True)
    xn = x * lax.rsqrt(ms + RMS_EPS) * g_ref[...]
    h = (xn * (1.0 + sc_ref[0]) + sh_ref[0]).astype(BF16)
    proj = _dot(h, w_ref[...])
    hs = hs_ref[...]

    def head_norm(t, gain):
        m = _dot_sel2(t * t, hs) * (1.0 / HEAD_DIM)
        return t * lax.rsqrt(m + RMS_EPS) * gain

    q = head_norm(proj[:, 0:WIDTH], gq_ref[...])
    k = head_norm(proj[:, WIDTH:2 * WIDTH], gk_ref[...])
    v = proj[:, 2 * WIDTH:3 * WIDTH]
    k_ref[0] = k
    v_ref[0] = v
    qb = q.astype(BF16)
    kb = k.astype(BF16)
    vb = v.astype(BF16)
    for hp in range(N_HEAD_PAIRS):
        sl = slice(hp * PAIR, (hp + 1) * PAIR)
        qh_ref[0, hp] = qb[:, sl]
        kh_ref[0, hp] = kb[:, sl]
        vh_ref[0, hp] = vb[:, sl]
    rw_ref[0] = proj[:, 3 * WIDTH:3 * WIDTH + RWKV_COLS]
    fz = proj[:, FZ_OFF:FZ_OFF + N_HEADS] + bf_ref[...]
    lf_ref[0] = -_softplus(-fz)
    fzt = _dot_nt(wfz_ref[...], h) + bfc_ref[...]
    lft_ref[0] = -_softplus(-fzt)


def _inproj(x3, sc, sh, g_mix, w_r, wfz_t, gq, gk, b_forget, tm):
    bx, tx, d = x3.shape
    per_row = sc.shape[1] != 1
    mod_spec = (pl.BlockSpec((1, tm, d), lambda b, t: (b, t, 0)) if per_row
                else pl.BlockSpec((1, 1, d), lambda b, t: (b, 0, 0)))
    const = lambda shape: pl.BlockSpec(shape, lambda b, t: tuple(0 for _ in shape))
    head_spec = pl.BlockSpec((1, N_HEAD_PAIRS, tm, PAIR), lambda b, t: (b, 0, t, 0))
    row_spec = lambda w: pl.BlockSpec((1, tm, w), lambda b, t: (b, t, 0))
    head_shape = jax.ShapeDtypeStruct((bx, N_HEAD_PAIRS, tx, PAIR), BF16)
    return pl.pallas_call(
        _inproj_kernel,
        grid=(bx, tx // tm),
        in_specs=[row_spec(d), mod_spec, mod_spec, const((1, d)), const((d, IN_COLS_PADDED)),
                  const((N_HEADS, d)), const((1, WIDTH)), const((1, WIDTH)), const((1, N_HEADS)),
                  const((N_HEADS, 1)), const((WIDTH, WIDTH))],
        out_specs=[head_spec, head_spec, head_spec, row_spec(WIDTH), row_spec(WIDTH), row_spec(N_HEADS),
                   pl.BlockSpec((1, N_HEADS, tm), lambda b, t: (b, 0, t)), row_spec(RWKV_COLS)],
        out_shape=[head_shape, head_shape, head_shape,
                   jax.ShapeDtypeStruct((bx, tx, WIDTH), F32), jax.ShapeDtypeStruct((bx, tx, WIDTH), F32),
                   jax.ShapeDtypeStruct((bx, tx, N_HEADS), F32), jax.ShapeDtypeStruct((bx, N_HEADS, tx), F32),
                   jax.ShapeDtypeStruct((bx, tx, RWKV_COLS), F32)],
        compiler_params=_cparams("parallel", "arbitrary"),
    )(x3, sc, sh, g_mix, w_r, wfz_t, gq, gk, b_forget.reshape(1, N_HEADS), b_forget.reshape(N_HEADS, 1),
      _head_ones(N_HEADS))


def _cumsum_kernel(l_ref, u_ref, o_ref, c_ref):
    tc = l_ref.shape[2]

    @pl.when(pl.program_id(1) == 0)
    def _():
        c_ref[...] = jnp.zeros_like(c_ref)

    f = _dot_sel3(l_ref[0], u_ref[...]) + c_ref[:, 0:1]
    o_ref[0] = f
    c_ref[...] = jnp.broadcast_to(f[:, tc - 1:tc], c_ref.shape)


def _cumsum_t(lft, tc):
    b, h, t = lft.shape
    r = lax.broadcasted_iota(jnp.int32, (tc, tc), 0)
    c = lax.broadcasted_iota(jnp.int32, (tc, tc), 1)
    upper = (r <= c).astype(BF16)
    return pl.pallas_call(
        _cumsum_kernel,
        grid=(b, t // tc),
        in_specs=[pl.BlockSpec((1, h, tc), lambda i, j: (i, 0, j)), pl.BlockSpec((tc, tc), lambda i, j: (0, 0))],
        out_specs=pl.BlockSpec((1, h, tc), lambda i, j: (i, 0, j)),
        out_shape=jax.ShapeDtypeStruct((b, h, t), F32),
        scratch_shapes=[pltpu.VMEM((h, 128), F32)],
        compiler_params=_cparams("parallel", "arbitrary"),
    )(lft, upper)


def _attn_kernel(q_ref, k_ref, v_ref, f_ref, o_ref, m_ref, acc_ref, *, tq, tk):
    i = pl.program_id(2)
    q2 = q_ref[0, 0]
    lane = lax.broadcasted_iota(jnp.int32, q2.shape, 1)
    zero = jnp.zeros_like(q2)
    qs = (jnp.where(lane < HEAD_DIM, q2, zero), jnp.where(lane >= HEAD_DIM, q2, zero))
    q0 = pl.multiple_of(i * tq, tq)
    f_base = [f_ref[0, 0, hh:hh + 1, pl.ds(q0, tq)][:, 0:1] for hh in range(2)]
    m_ref[...] = jnp.full_like(m_ref, NEG)
    acc_ref[...] = jnp.zeros_like(acc_ref)

    def tile(j, masked):
        k0 = pl.multiple_of(j * tk, tk)
        k2 = k_ref[0, 0, pl.ds(k0, tk), :]
        v2 = v_ref[0, 0, pl.ds(k0, tk), :]
        vlane = lax.broadcasted_iota(jnp.int32, v2.shape, 1)
        one = jnp.ones((), v2.dtype)
        vs = (jnp.where(vlane < HEAD_DIM, v2, one), jnp.where(vlane >= HEAD_DIM, v2, one))
        for hh in range(2):
            s = _dot_nt(qs[hh], k2) + (f_base[hh] - f_ref[0, 0, hh:hh + 1, pl.ds(k0, tk)])
            if masked:
                r = lax.broadcasted_iota(jnp.int32, s.shape, 0)
                c = lax.broadcasted_iota(jnp.int32, s.shape, 1)
                s = jnp.where(c + k0 <= r + q0, s, NEG)
            m_old = m_ref[hh]
            m_new = jnp.maximum(m_old, jnp.max(s, axis=1, keepdims=True))
            p = jnp.exp(s - jnp.concatenate([m_new] * (tk // 128), axis=1))
            alpha = jnp.exp(m_old - m_new)
            acc_ref[hh] = alpha * acc_ref[hh] + _dot(p.astype(BF16), vs[hh])
            m_ref[hh] = m_new

    def body(j, carry):
        tile(j, False)
        return carry

    n_sub = tq // tk
    lax.fori_loop(0, i * n_sub, body, 0)
    for dd in range(n_sub):
        tile(i * n_sub + dd, True)
    acc_a = acc_ref[0]
    acc_b = acc_ref[1]
    o_a = acc_a / pltpu.roll(acc_a, HEAD_DIM, 1)
    o_b = acc_b / pltpu.roll(acc_b, HEAD_DIM, 1)
    o_ref[0, 0] = jnp.where(lane < HEAD_DIM, o_a, o_b).astype(o_ref.dtype)


def _fox_prompt(qh, kh, vh, ft, tq, tk):
    b, hp, t, w = qh.shape
    assert tq % tk == 0 and tk % 128 == 0
    ft4 = ft.reshape(b, hp, 2, t)
    seq_spec = pl.BlockSpec((1, 1, t, w), lambda bi, hi, i: (bi, hi, 0, 0))
    tile_spec = pl.BlockSpec((1, 1, tq, w), lambda bi, hi, i: (bi, hi, i, 0))
    return pl.pallas_call(
        functools.partial(_attn_kernel, tq=tq, tk=tk),
        grid=(b, hp, t // tq),
        in_specs=[tile_spec, seq_spec, seq_spec, pl.BlockSpec((1, 1, 2, t), lambda bi, hi, i: (bi, hi, 0, 0))],
        out_specs=tile_spec,
        out_shape=jax.ShapeDtypeStruct((b, hp, t, w), BF16),
        scratch_shapes=[pltpu.VMEM((2, tq, 128), F32), pltpu.VMEM((2, tq, w), F32)],
        compiler_params=_cparams("parallel", "parallel", "arbitrary"),
    )(qh, kh, vh, ft4)


PAGES_PER_STEP = 8
QROWS = 4 * N_HEADS


def _decode_kernel(pt_ref, q_ref, kn_ref, vn_ref, lfn_ref, cnew_ref, ustrict_ref, fold_ref, *refs, n_steps):
    del pt_ref
    npg = PAGES_PER_STEP
    k_refs = refs[0:npg]
    v_refs = refs[npg:2 * npg]
    lf_refs = refs[2 * npg:3 * npg]
    o_ref, m_ref, l_ref, acc_ref, accn_ref, carry_ref = refs[3 * npg:]
    j = pl.program_id(1)
    q = q_ref[0]
    r = lax.broadcasted_iota(jnp.int32, (QROWS, WIDTH), 0)
    c = lax.broadcasted_iota(jnp.int32, (QROWS, WIDTH), 1)
    own_head = (r % N_HEADS) == (c // HEAD_DIM)
    q_bd = jnp.where(own_head, jnp.concatenate([q] * N_HEADS, axis=1), jnp.zeros((), q.dtype))

    def softmax_step(s):
        m_old = m_ref[...]
        m_new = jnp.maximum(m_old, jnp.max(s, axis=1, keepdims=True))
        p = jnp.exp(s - m_new)
        alpha = jnp.exp(m_old - m_new)
        l_ref[...] = alpha * l_ref[...] + jnp.sum(p, axis=1, keepdims=True)
        m_ref[...] = m_new
        return p.astype(BF16), alpha

    @pl.when(j == 0)
    def _():
        m_ref[...] = jnp.full_like(m_ref, NEG)
        l_ref[...] = jnp.zeros_like(l_ref)
        acc_ref[...] = jnp.zeros_like(acc_ref)
        carry_ref[...] = jnp.zeros_like(carry_ref)
        s = _dot_nt(q, kn_ref[0])
        bias = _dot_sel3(lfn_ref[0], cnew_ref[...])[0:1, :]
        rr = lax.broadcasted_iota(jnp.int32, s.shape, 0)
        cc = lax.broadcasted_iota(jnp.int32, s.shape, 1)
        ok = ((rr % N_HEADS) == (cc % N_HEADS)) & ((cc // N_HEADS) <= (rr // N_HEADS))
        p, _ = softmax_step(jnp.where(ok, s + bias, NEG))
        accn_ref[...] = _dot(p, vn_ref[0])

    ustrict = ustrict_ref[...]
    carry = carry_ref[...]
    scores = []
    for u in range(npg):
        kt = k_refs[u][...].reshape(WIDTH, PAGE).astype(BF16)
        lft = lf_refs[u][...]
        a = jnp.broadcast_to(lft[None], (4, N_HEADS, PAGE)).reshape(QROWS, PAGE)
        hi, mid, lo = _split3(a)
        bx = _dot(jnp.concatenate([hi, mid, lo], axis=0), ustrict)
        bias = bx[0:QROWS] + (bx[QROWS:2 * QROWS] + bx[2 * QROWS:3 * QROWS]) + carry
        carry = carry + jnp.sum(a, axis=1, keepdims=True)
        scores.append(_dot(q_bd, kt) + bias)
    carry_ref[...] = carry
    p, alpha = softmax_step(jnp.concatenate(scores, axis=1))
    acc = alpha * acc_ref[...]
    for u in range(npg):
        vt = v_refs[u][...].reshape(WIDTH, PAGE).astype(BF16)
        acc = acc + _dot_nt(p[:, u * PAGE:(u + 1) * PAGE], vt)
    acc_ref[...] = acc
    accn_ref[...] = alpha * accn_ref[...]

    @pl.when(j == n_steps - 1)
    def _():
        own = jnp.where(own_head, acc_ref[...], 0.0)
        o_ref[0] = (_dot_sel3(own, fold_ref[...]) + accn_ref[...]) / l_ref[...]


def _fox_decode(page_table, q, k_new, v_new, lf_new, cache_kt, cache_vt, cache_lft):
    nb, n_pages = page_table.shape
    npg = PAGES_PER_STEP
    n_steps = n_pages // npg
    ustrict = (lax.broadcasted_iota(jnp.int32, (PAGE, PAGE), 0)
               > lax.broadcasted_iota(jnp.int32, (PAGE, PAGE), 1)).astype(BF16)
    ri = lax.broadcasted_iota(jnp.int32, (QROWS, QROWS), 0)
    ci = lax.broadcasted_iota(jnp.int32, (QROWS, QROWS), 1)
    cnew = -(((ri % N_HEADS) == (ci % N_HEADS)) & ((ri // N_HEADS) <= (ci // N_HEADS))).astype(BF16)
    fold = (lax.broadcasted_iota(jnp.int32, (WIDTH, HEAD_DIM), 0) % HEAD_DIM
            == lax.broadcasted_iota(jnp.int32, (WIDTH, HEAD_DIM), 1)).astype(BF16)

    def page_idx(u):
        return lambda b, j, pt: (pt[b, n_pages - 1 - (j * npg + u)], 0, 0, 0)

    def page_idx3(u):
        return lambda b, j, pt: (pt[b, n_pages - 1 - (j * npg + u)], 0, 0)

    per_b = lambda shape: pl.BlockSpec((1,) + shape, lambda b, j, pt: (b, 0, 0))
    const = lambda shape: pl.BlockSpec(shape, lambda b, j, pt: (0, 0))
    in_specs = ([per_b((QROWS, HEAD_DIM)), per_b((QROWS, HEAD_DIM)), per_b((QROWS, HEAD_DIM)),
                 per_b((8, QROWS)), const((QROWS, QROWS)), const((PAGE, PAGE)), const((WIDTH, HEAD_DIM))]
                + [pl.BlockSpec((None, N_HEADS, HEAD_DIM, PAGE), page_idx(u)) for u in range(npg)]
                + [pl.BlockSpec((None, N_HEADS, HEAD_DIM, PAGE), page_idx(u)) for u in range(npg)]
                + [pl.BlockSpec((None, N_HEADS, PAGE), page_idx3(u)) for u in range(npg)])
    grid_spec = pltpu.PrefetchScalarGridSpec(
        num_scalar_prefetch=1,
        grid=(nb, n_steps),
        in_specs=in_specs,
        out_specs=pl.BlockSpec((1, QROWS, HEAD_DIM), lambda b, j, pt: (b, 0, 0)),
        scratch_shapes=[pltpu.VMEM((QROWS, 1), F32), pltpu.VMEM((QROWS, 1), F32),
                        pltpu.VMEM((QROWS, WIDTH), F32), pltpu.VMEM((QROWS, HEAD_DIM), F32),
                        pltpu.VMEM((QROWS, 1), F32)],
    )
    lfn = jnp.broadcast_to(lf_new.reshape(nb, 1, QROWS), (nb, 8, QROWS))
    return pl.pallas_call(
        functools.partial(_decode_kernel, n_steps=n_steps),
        grid_spec=grid_spec,
        out_shape=jax.ShapeDtypeStruct((nb, QROWS, HEAD_DIM), F32),
        compiler_params=_cparams("parallel", "arbitrary"),
    )(page_table, q, k_new, v_new, lfn, cnew, ustrict, fold,
      *([cache_kt] * npg), *([cache_vt] * npg), *([cache_lft] * npg))


def _rwkv_prep_kernel(p_ref, prev_ref, first_ref, mu_ref, w0_ref, w2_ref, a0_ref, a2_ref, g2_ref, kk_ref, ka_ref,
                      rk_ref, hs_ref, r_ref, k_ref, v_ref, w_ref, a_ref, b_ref, g_ref, bonus_ref, *, shift_rows):
    p = p_ref[0]
    if shift_rows:
        carry = jnp.where(pl.program_id(1) == 0, first_ref[0], prev_ref[0, 7:8, :])
        row0 = lax.broadcasted_iota(jnp.int32, p.shape, 0) == 0
        prev = jnp.where(row0, carry, pltpu.roll(p, 1, 0))
    else:
        prev = prev_ref[0]
    z = p + (prev - p) * mu_ref[...]
    r = z[:, 0:WIDTH]
    k = z[:, WIDTH:2 * WIDTH]
    v = z[:, 2 * WIDTH:3 * WIDTH]
    lora = z[:, 3 * WIDTH:3 * WIDTH + DECAY_LORA + ICLR_LORA]
    gd = z[:, 3 * WIDTH + DECAY_LORA + ICLR_LORA:RWKV_COLS]
    hs = hs_ref[...]
    w = -_softplus(-(w0_ref[...] + _dot_x3(jnp.tanh(lora), w2_ref[...]))) - 0.5
    decay = jnp.exp(-jnp.exp(w))
    a = _sigmoid(a0_ref[...] + _dot_x3(lora, a2_ref[...]))
    g = _dot_x3(_sigmoid(gd), g2_ref[...])
    kk = k * kk_ref[...]
    norm = jnp.sqrt(_dot_sel3(kk * kk, hs))
    kk = kk / jnp.maximum(norm, 1e-12)
    k = k * (1.0 + (a - 1.0) * ka_ref[...])
    bonus = _dot_sel3(r * k * rk_ref[...], hs) * v
    r_ref[0] = r
    k_ref[0] = k
    v_ref[0] = v
    w_ref[0] = decay
    a_ref[0] = -kk
    b_ref[0] = kk * a
    g_ref[0] = g
    bonus_ref[0] = bonus


def _rwkv_prep(p_rw, prev, first, mu, w0, w2p, a0, a2p, g2, kk, ka, rk, tm):
    bx, tx, _ = p_rw.shape
    row = lambda w: pl.BlockSpec((1, tm, w), lambda b, t: (b, t, 0))
    const = lambda shape: pl.BlockSpec(shape, lambda b, t: tuple(0 for _ in shape))
    out = jax.ShapeDtypeStruct((bx, tx, WIDTH), F32)
    shift_rows = prev is None
    if shift_rows:
        prev = p_rw
        prev_spec = pl.BlockSpec((1, 8, RWKV_COLS), lambda b, t: (b, jnp.maximum(t * (tm // 8) - 1, 0), 0))
    else:
        prev_spec = row(RWKV_COLS)
    return pl.pallas_call(
        functools.partial(_rwkv_prep_kernel, shift_rows=shift_rows),
        grid=(bx, tx // tm),
        in_specs=[row(RWKV_COLS), prev_spec, pl.BlockSpec((1, 1, RWKV_COLS), lambda b, t: (b, 0, 0)),
                  const((1, RWKV_COLS)), const((1, WIDTH)),
                  const((DECAY_LORA + ICLR_LORA, WIDTH)), const((1, WIDTH)), const((DECAY_LORA + ICLR_LORA, WIDTH)),
                  const((GATE_LORA, WIDTH)), const((1, WIDTH)), const((1, WIDTH)), const((1, WIDTH)),
                  const((WIDTH, WIDTH))],
        out_specs=[row(WIDTH)] * 8,
        out_shape=[out] * 8,
        compiler_params=_cparams("parallel", "arbitrary"),
    )(p_rw, prev, first, mu, w0, w2p, a0, a2p, g2, kk, ka, rk, _head_ones(N_HEADS))


WKV_NB = 2
WKV_HG = 4
WKV_LANES = WKV_HG * HEAD_DIM
WKV_GROUPS = N_HEADS // WKV_HG
WKV_ROWS = WKV_NB * WKV_GROUPS * HEAD_DIM


def _wkv_kernel(r_ref, k_ref, v_ref, w_ref, a_ref, b_ref, s0_ref, ones_ref, y_ref, sT_ref, st_ref, *, tc, n_t):
    ti = pl.program_id(1)
    blocks = [(bi, gg) for bi in range(WKV_NB) for gg in range(WKV_GROUPS)]

    @pl.when(ti == 0)
    def _():
        for n, (bi, gg) in enumerate(blocks):
            st_ref[n * HEAD_DIM:(n + 1) * HEAD_DIM, :] = jnp.concatenate(
                [s0_ref[bi, gg * WKV_HG + q] for q in range(WKV_HG)], axis=1)

    rows = lax.broadcasted_iota(jnp.int32, (WKV_ROWS, WKV_LANES), 0)
    lanes = lax.broadcasted_iota(jnp.int32, (WKV_ROWS, WKV_LANES), 1)
    eye = (rows % HEAD_DIM) == (lanes % HEAD_DIM)
    eye_f = eye.astype(F32)
    ones = ones_ref[...]
    grp = min(tc, 8)

    def bcast(xs, u):
        return jnp.concatenate(
            [jnp.broadcast_to(xs[bi][u:u + 1, gg * WKV_LANES:(gg + 1) * WKV_LANES], (HEAD_DIM, WKV_LANES))
             for bi, gg in blocks], axis=0)

    def group(gi, carry):
        t0 = pl.multiple_of(gi * grp, grp)
        load = lambda ref: [ref[bi, pl.ds(t0, grp), :] for bi in range(WKV_NB)]
        r8, k8, v8, w8, a8, b8 = (load(ref) for ref in (r_ref, k_ref, v_ref, w_ref, a_ref, b_ref))
        vdiag = lambda u: jnp.where(eye, bcast(v8, u), 0.0).astype(BF16)
        s = st_ref[...]
        ys = []

        def y_rows(yb):
            yb = yb * eye_f
            return [jnp.sum(yb[n * HEAD_DIM:(n + 1) * HEAD_DIM], axis=0, keepdims=True) for n in range(len(blocks))]

        vcol = _dot(vdiag(0), ones)
        ybf = None
        for u in range(grp):
            sa = _dot((s * bcast(a8, u)).astype(BF16), ones)
            side = [vdiag(u + 1)] if u + 1 < grp else []
            if ybf is not None:
                side.append(ybf)
            misc = _dot(jnp.concatenate(side, axis=0), ones) if side else None
            s = s * bcast(w8, u) + sa * bcast(b8, u) + vcol * bcast(k8, u)
            if misc is not None:
                off = 0
                if u + 1 < grp:
                    vcol = misc[0:WKV_ROWS]
                    off = WKV_ROWS
                if ybf is not None:
                    ys.append(y_rows(misc[off:off + WKV_ROWS]))
            ybf = (s * bcast(r8, u)).astype(BF16)
        ys.append(y_rows(_dot(ybf, ones)))
        st_ref[...] = s
        for bi in range(WKV_NB):
            y_ref[bi, pl.ds(t0, grp), :] = jnp.concatenate(
                [jnp.concatenate([ys[u][bi * WKV_GROUPS + gg] for gg in range(WKV_GROUPS)], axis=1)
                 for u in range(grp)], axis=0)
        return carry

    lax.fori_loop(0, tc // grp, group, 0)

    @pl.when(ti == n_t - 1)
    def _():
        for n, (bi, gg) in enumerate(blocks):
            for q in range(WKV_HG):
                sT_ref[bi, gg * WKV_HG + q] = st_ref[n * HEAD_DIM:(n + 1) * HEAD_DIM, q * HEAD_DIM:(q + 1) * HEAD_DIM]


def _wkv_scan(r, k, v, w, a, b, s0, tc):
    bx, tx, _ = r.shape
    assert bx % WKV_NB == 0
    n_t = tx // tc
    seq = pl.BlockSpec((WKV_NB, tc, WIDTH), lambda i, t: (i, t, 0))
    st = pl.BlockSpec((WKV_NB, N_HEADS, HEAD_DIM, HEAD_DIM), lambda i, t: (i, 0, 0, 0))
    ri = lax.broadcasted_iota(jnp.int32, (WKV_LANES, WKV_LANES), 0) // HEAD_DIM
    ci = lax.broadcasted_iota(jnp.int32, (WKV_LANES, WKV_LANES), 1) // HEAD_DIM
    ones = (ri == ci).astype(BF16)
    return pl.pallas_call(
        functools.partial(_wkv_kernel, tc=tc, n_t=n_t),
        grid=(bx // WKV_NB, n_t),
        in_specs=[seq] * 6 + [st, pl.BlockSpec((WKV_LANES, WKV_LANES), lambda i, t: (0, 0))],
        out_specs=[seq, st],
        out_shape=[jax.ShapeDtypeStruct((bx, tx, WIDTH), F32),
                   jax.ShapeDtypeStruct((bx, N_HEADS, HEAD_DIM, HEAD_DIM), F32)],
        scratch_shapes=[pltpu.VMEM((WKV_ROWS, WKV_LANES), F32)],
        compiler_params=_cparams("parallel", "arbitrary"),
    )(r, k, v, w, a, b, s0, ones)


def _outproj_kernel(x_ref, of_ref, y_ref, bonus_ref, g_ref, gt_ref, sc_ref, sh_ref, lw_ref, lb_ref, gf_ref,
                    wo_ref, wr_ref, br_ref, hs_ref, lt_ref, x1_ref, h2_ref, ti_ref, tw_ref, rk_ref, cnt_ref, run_ref):
    @pl.when(pl.program_id(1) == 0)
    def _():
        run_ref[...] = jnp.zeros_like(run_ref)

    hs = hs_ref[...]
    y = y_ref[0]
    mu = _dot_sel2(y, hs) * (1.0 / HEAD_DIM)
    yc = y - mu
    var = _dot_sel2(yc * yc, hs) * (1.0 / HEAD_DIM)
    o_rw = ((yc * lax.rsqrt(var + GN_EPS)) * lw_ref[...] + lb_ref[...] + bonus_ref[0]) * g_ref[0]
    mixed = _dot(o_rw.astype(BF16), wo_ref[WIDTH:2 * WIDTH, :])
    for hp in range(N_HEAD_PAIRS):
        mixed = mixed + _dot(of_ref[0, hp], wo_ref[hp * PAIR:(hp + 1) * PAIR, :])
    x1 = x_ref[0] + gt_ref[0] * mixed
    x1_ref[0] = x1
    ms = jnp.mean(x1 * x1, axis=-1, keepdims=True)
    h2 = (x1 * lax.rsqrt(ms + RMS_EPS) * gf_ref[...]) * (1.0 + sc_ref[0]) + sh_ref[0]
    h2_ref[0] = h2.astype(BF16)
    logits = _dot_x3(h2, wr_ref[...]) + br_ref[...]
    lane = lax.broadcasted_iota(jnp.int32, logits.shape, 1).astype(F32)
    vals, idxs = [], []
    for _ in range(TOP_K):
        m = jnp.max(logits, axis=1, keepdims=True)
        idx = jnp.min(jnp.where(logits == m, lane, float(N_EXPERTS)), axis=1, keepdims=True)
        vals.append(m)
        idxs.append(idx)
        logits = jnp.where(lane == idx, -jnp.inf, logits)
    e = [jnp.exp(vv - vals[0]) for vv in vals]
    tot = e[0] + e[1] + e[2] + e[3]
    tw_ref[0] = jnp.concatenate(e, axis=1) / tot
    ti_ref[0] = jnp.concatenate(idxs, axis=1).astype(jnp.int32)
    chosen = [lane == idx for idx in idxs]
    multi_hot = jnp.where(chosen[0] | chosen[1] | chosen[2] | chosen[3], 1.0, 0.0)
    before = _dot(lt_ref[...], multi_hot.astype(BF16)) + run_ref[...]
    ranks = [jnp.sum(jnp.where(c, before, 0.0), axis=1, keepdims=True) for c in chosen]
    rk_ref[0] = jnp.concatenate(ranks, axis=1).astype(jnp.int32)
    run_ref[...] = run_ref[...] + jnp.sum(multi_hot, axis=0, keepdims=True)
    cnt_ref[0] = run_ref[...]


def _outproj(x3, of, y, bonus, g, gt, sc, sh, lnx_w, lnx_b, g_ffn, w_out_b, w_router, b_router, tm):
    bx, tx, d = x3.shape
    per_row = gt.shape[1] != 1
    mod_spec = (pl.BlockSpec((1, tm, d), lambda b, t: (b, t, 0)) if per_row
                else pl.BlockSpec((1, 1, d), lambda b, t: (b, 0, 0)))
    row = lambda w: pl.BlockSpec((1, tm, w), lambda b, t: (b, t, 0))
    const = lambda shape: pl.BlockSpec(shape, lambda b, t: tuple(0 for _ in shape))
    lower = (lax.broadcasted_iota(jnp.int32, (tm, tm), 0) > lax.broadcasted_iota(jnp.int32, (tm, tm), 1)).astype(BF16)
    return pl.pallas_call(
        _outproj_kernel,
        grid=(bx, tx // tm),
        in_specs=[row(d), pl.BlockSpec((1, N_HEAD_PAIRS, tm, PAIR), lambda b, t: (b, 0, t, 0)),
                  row(WIDTH), row(WIDTH), row(WIDTH), mod_spec, mod_spec, mod_spec,
                  const((1, WIDTH)), const((1, WIDTH)), const((1, d)), const((2 * WIDTH, d)),
                  const((d, N_EXPERTS)), const((1, N_EXPERTS)), const((WIDTH, WIDTH)), const((tm, tm))],
        out_specs=[row(d), row(d), row(TOP_K), row(TOP_K), row(TOP_K),
                   pl.BlockSpec((1, 1, N_EXPERTS), lambda b, t: (b, 0, 0))],
        out_shape=[jax.ShapeDtypeStruct((bx, tx, d), F32), jax.ShapeDtypeStruct((bx, tx, d), BF16),
                   jax.ShapeDtypeStruct((bx, tx, TOP_K), jnp.int32), jax.ShapeDtypeStruct((bx, tx, TOP_K), F32),
                   jax.ShapeDtypeStruct((bx, tx, TOP_K), jnp.int32), jax.ShapeDtypeStruct((bx, 1, N_EXPERTS), F32)],
        scratch_shapes=[pltpu.VMEM((1, N_EXPERTS), F32)],
        compiler_params=_cparams("parallel", "arbitrary"),
    )(x3, of, y, bonus, g, gt, sc, sh, lnx_w, lnx_b, g_ffn, w_out_b, w_router, b_router, _head_ones(N_HEADS), lower)


MOE_TM = 512


def _moe_kernel(te_ref, tf_ref, x_ref, wg_ref, bg_ref, wu_ref, bu_ref, wd_ref, bd_ref, o_ref, wb_ref):
    del te_ref
    i = pl.program_id(0)

    @pl.when(tf_ref[i] == 2)
    def _():
        wb_ref[0] = wg_ref[0].astype(BF16)
        wb_ref[1] = wu_ref[0].astype(BF16)
        wb_ref[2] = wd_ref[0].astype(BF16)

    @pl.when(tf_ref[i] != 0)
    def _():
        x = x_ref[...]
        glu = jnp.minimum(_dot(x, wb_ref[0]) + bg_ref[0], SWIGLU_LIMIT)
        lin = jnp.clip(_dot(x, wb_ref[1]) + bu_ref[0], -SWIGLU_LIMIT, SWIGLU_LIMIT)
        act = glu * _sigmoid(SWIGLU_ALPHA * glu) * (lin + 1.0)
        o_ref[...] = (_dot(act.astype(BF16), wb_ref[2]) + bd_ref[0]).astype(o_ref.dtype)

    @pl.when(tf_ref[i] == 0)
    def _():
        o_ref[...] = jnp.zeros_like(o_ref)


def _moe_gemm(tile_expert, tile_flag, xg, wg, bg, wu, bu, wd, bd):
    n_slots, d = xg.shape
    n_tiles = n_slots // MOE_TM
    wspec = pl.BlockSpec((1, d, d), lambda i, te, tf: (te[i], 0, 0))
    bspec = pl.BlockSpec((1, 1, d), lambda i, te, tf: (te[i], 0, 0))
    grid_spec = pltpu.PrefetchScalarGridSpec(
        num_scalar_prefetch=2,
        grid=(n_tiles,),
        in_specs=[pl.BlockSpec((MOE_TM, d), lambda i, te, tf: (i, 0)), wspec, bspec, wspec, bspec, wspec, bspec],
        out_specs=pl.BlockSpec((MOE_TM, d), lambda i, te, tf: (i, 0)),
        scratch_shapes=[pltpu.VMEM((3, d, d), BF16)],
    )
    return pl.pallas_call(
        _moe_kernel,
        grid_spec=grid_spec,
        out_shape=jax.ShapeDtypeStruct((n_slots, d), BF16),
        compiler_params=_cparams("arbitrary"),
    )(tile_expert, tile_flag, xg, wg, bg, wu, bu, wd, bd)


def _final_kernel(x1_ref, yg_ref, tw_ref, gt_ref, o_ref):
    tw = tw_ref[0]
    moe = tw[:, 0:1] * yg_ref[0, 0].astype(F32)
    for kk in range(1, TOP_K):
        moe = moe + tw[:, kk:kk + 1] * yg_ref[kk, 0].astype(F32)
    o_ref[0] = x1_ref[0] + gt_ref[0] * moe


def _final(x1, yg, tw, gt, tm):
    bx, tx, d = x1.shape
    per_row = gt.shape[1] != 1
    mod_spec = (pl.BlockSpec((1, tm, d), lambda b, t: (b, t, 0)) if per_row
                else pl.BlockSpec((1, 1, d), lambda b, t: (b, 0, 0)))
    row = pl.BlockSpec((1, tm, d), lambda b, t: (b, t, 0))
    return pl.pallas_call(
        _final_kernel,
        grid=(bx, tx // tm),
        in_specs=[row, pl.BlockSpec((TOP_K, 1, tm, d), lambda b, t: (0, b, t, 0)),
                  pl.BlockSpec((1, tm, TOP_K), lambda b, t: (b, t, 0)), mod_spec],
        out_specs=row,
        out_shape=jax.ShapeDtypeStruct((bx, tx, d), F32),
        compiler_params=_cparams("parallel", "arbitrary"),
    )(x1, yg, tw, gt)


def _dispatch_plan(top_i, rank_in_seg, seg_counts, seg_tokens):
    n_assign = top_i.size
    n_tiles = -(-n_assign // MOE_TM) + N_EXPERTS
    n_slots = n_tiles * MOE_TM
    e_flat = top_i.reshape(-1)
    order = jnp.argsort(e_flat, stable=True).astype(jnp.int32)
    counts = jnp.sum(seg_counts, axis=0)
    cnt_start = jnp.cumsum(counts) - counts
    seg_base = jnp.cumsum(seg_counts, axis=0) - seg_counts
    padded = ((counts + MOE_TM - 1) // MOE_TM) * MOE_TM
    pad_end = jnp.cumsum(padded)
    pad_start = pad_end - padded
    experts = jnp.arange(N_EXPERTS, dtype=jnp.int32)
    tile_start = jnp.arange(n_tiles, dtype=jnp.int32) * MOE_TM
    tile_expert = jnp.minimum(jnp.sum((tile_start[:, None] >= pad_end[None, :]).astype(jnp.int32), axis=1),
                              N_EXPERTS - 1)
    tile_live = tile_start < pad_end[-1]
    is_first = jnp.concatenate([jnp.ones((1,), bool), tile_expert[1:] != tile_expert[:-1]])
    tile_flag = jnp.where(tile_live, jnp.where(is_first, 2, 1), 0).astype(jnp.int32)
    of_tile = lambda tab: jnp.sum(jnp.where(tile_expert[:, None] == experts[None, :], tab[None, :], 0), axis=1)
    rank = (tile_start - of_tile(pad_start))[:, None] + jnp.arange(MOE_TM, dtype=jnp.int32)[None, :]
    live = rank < of_tile(counts)[:, None]
    src = order.at[jnp.clip(of_tile(cnt_start)[:, None] + rank, 0, n_assign - 1)].get(mode="promise_in_bounds")
    token_of_slot = jnp.where(live, src // TOP_K, 0).reshape(n_slots)
    tok_base = jnp.repeat(seg_base + pad_start[None, :], jnp.asarray(seg_tokens), axis=0,
                          total_repeat_length=sum(seg_tokens))
    hit = top_i[:, :, None] == experts[None, None, :]
    slot_of_assign = jnp.sum(jnp.where(hit, tok_base[:, None, :], 0), axis=-1) + rank_in_seg
    return slot_of_assign, token_of_slot, tile_expert, tile_flag


def _pick_tile(n, pref):
    t = min(n, pref)
    assert n % t == 0, (n, t)
    return t


def kernel(x_prompt, x_sample, cache_k, cache_v, cache_logf, state_wkv, state_shift, page_table, c_prompt,
           c_sample, w_mod, b_mod, g_mix, g_ffn, w_in, q_norm, k_norm, b_forget, rw_mu, rw_w0, rw_w2, rw_a0,
           rw_a2, rw_g2, rw_kk, rw_ka, rw_rk, lnx_w, lnx_b, w_out, w_router, b_router, w_gate, b_gate, w_up,
           b_up, w_down, b_down):
    depth = w_mod.shape[0]
    assert depth == 1
    n_p, t_p, d = x_prompt.shape
    n_s, t_s, _ = x_sample.shape
    assert d == D_MODEL and t_s * N_HEADS == QROWS
    l = 0

    w_in_l = w_in[l]
    w_r = jnp.concatenate([w_in_l[:, :3 * WIDTH], w_in_l[:, FOX_COLS:], w_in_l[:, 3 * WIDTH:FOX_COLS],
                           jnp.zeros((d, 128 - N_HEADS), F32)], axis=1).astype(BF16)
    wfz_t = w_in_l[:, 3 * WIDTH:FOX_COLS].T.astype(BF16)
    gq = (jnp.tile(q_norm[l], N_HEADS) * ATTN_SCALE).reshape(1, WIDTH)
    gk = jnp.tile(k_norm[l], N_HEADS).reshape(1, WIDTH)
    zpad = jnp.zeros((DECAY_LORA, WIDTH), F32)
    w2p = jnp.concatenate([rw_w2[l], zpad], axis=0)
    a2p = jnp.concatenate([zpad, rw_a2[l]], axis=0)
    row = lambda a: a.reshape(1, -1)

    n_c = n_p + n_s
    pad_c = (-n_c) % 8
    c_all = jnp.concatenate([c_sample, c_prompt, jnp.zeros((pad_c, d), F32)], axis=0)
    mod = _mod(c_all, w_mod[l], b_mod[l])
    mod_s = mod[:n_s].reshape(n_s, 1, N_MOD, d)
    mod_p = mod[n_s:n_s + n_p].reshape(n_p, 1, N_MOD, d)
    mod_s = jnp.broadcast_to(mod_s, (n_s, t_s, N_MOD, d)).reshape(1, n_s * t_s, N_MOD, d)
    mp = [mod_p[:, :, i] for i in range(N_MOD)]
    msm = [mod_s[:, :, i] for i in range(N_MOD)]

    xs3 = x_sample.reshape(1, n_s * t_s, d)
    tm_p = _pick_tile(t_p, 256)
    tm_s = _pick_tile(n_s * t_s, 256)

    proj_args = (row(g_mix[l]), w_r, wfz_t, gq, gk, b_forget[l])
    qh_p, kh_p, vh_p, k_p, v_p, lf_p, lft_p, rw_p = _inproj(x_prompt, mp[1], mp[0], *proj_args, tm_p)
    qh_s, _, _, k_s, v_s, lf_s, _, rw_s = _inproj(xs3, msm[1], msm[0], *proj_args, tm_s)

    ft = _cumsum_t(lft_p, _pick_tile(t_p, 512))
    tq = _pick_tile(t_p, 1024)
    of_p = _fox_prompt(qh_p, kh_p, vh_p, ft, tq, tq)

    q_s = jnp.transpose(qh_s[0], (1, 0, 2)).reshape(n_s, QROWS, HEAD_DIM)
    kn_s = k_s.reshape(n_s, QROWS, HEAD_DIM).astype(BF16)
    vn_s = v_s.reshape(n_s, QROWS, HEAD_DIM).astype(BF16)
    cache_kt = jnp.transpose(cache_k[l], (0, 2, 3, 1))
    cache_vt = jnp.transpose(cache_v[l], (0, 2, 3, 1))
    cache_lft = jnp.swapaxes(cache_logf[l], 1, 2)
    lfn_s = lf_s.reshape(n_s, QROWS)
    halves = [slice(0, n_s // 2), slice(n_s // 2, n_s)] if n_s % 2 == 0 else [slice(0, n_s)]
    o_halves = [_fox_decode(page_table[h], q_s[h], kn_s[h], vn_s[h], lfn_s[h], cache_kt, cache_vt, cache_lft)
                for h in halves]

    prep_w = (row(rw_mu[l]), row(rw_w0[l]), w2p, row(rw_a0[l]), a2p, rw_g2[l], row(rw_kk[l]), row(rw_ka[l]),
              row(rw_rk[l]))
    rw_s4 = rw_s.reshape(n_s, t_s, RWKV_COLS)
    prev_s = jnp.concatenate([state_shift[l][:, None, :], rw_s4[:, :-1]], axis=1).reshape(1, n_s * t_s, RWKV_COLS)
    r_p, kk_p, vv_p, w_p, a_p, b_p, g_p, bonus_p = _rwkv_prep(
        rw_p, None, jnp.zeros((n_p, 1, RWKV_COLS), F32), *prep_w, tm_p)
    r_s, kk_s, vv_s, w_s, a_s, b_s, g_s, bonus_s = _rwkv_prep(
        rw_s, prev_s, jnp.zeros((1, 1, RWKV_COLS), F32), *prep_w, tm_s)
    y_p, wkv_p = _wkv_scan(r_p, kk_p, vv_p, w_p, a_p, b_p,
                           jnp.zeros((n_p, N_HEADS, HEAD_DIM, HEAD_DIM), F32), _pick_tile(t_p, 256))
    to_seq = lambda a: a.reshape(n_s, t_s, WIDTH)
    y_s, wkv_s = _wkv_scan(to_seq(r_s), to_seq(kk_s), to_seq(vv_s), to_seq(w_s), to_seq(a_s), to_seq(b_s),
                           state_wkv[l], t_s)
    y_s = y_s.reshape(1, n_s * t_s, WIDTH)

    out_w = (row(lnx_w[l]), row(lnx_b[l]), row(g_ffn[l]), w_out[l].astype(BF16), w_router[l], row(b_router[l]))
    x1_p, h2_p, ti_p, tw_p, rk_p, cnt_p = _outproj(x_prompt, of_p, y_p, bonus_p, g_p, mp[2], mp[4], mp[3],
                                                   *out_w, tm_p)

    experts_w = (w_gate[l], b_gate[l].reshape(N_EXPERTS, 1, d), w_up[l], b_up[l].reshape(N_EXPERTS, 1, d),
                 w_down[l], b_down[l].reshape(N_EXPERTS, 1, d))
    take_rows = lambda a, idx: a.at[idx].get(mode="promise_in_bounds")

    def moe(h2, top_i, rank, counts, seg_tokens, tie=None):
        n_tok = sum(seg_tokens)
        slot_of_assign, token_of_slot, tile_expert, tile_flag = _dispatch_plan(
            top_i.reshape(n_tok, TOP_K), rank.reshape(n_tok, TOP_K),
            counts.reshape(len(seg_tokens), N_EXPERTS).astype(jnp.int32), seg_tokens)
        xg = take_rows(h2.reshape(n_tok, d), token_of_slot)
        if tie is not None:
            tile_flag, tie = lax.optimization_barrier((tile_flag, tie))
        ys = _moe_gemm(tile_expert, tile_flag, xg, *experts_w)
        return take_rows(ys, slot_of_assign.T), tie

    yg_p, o_first = moe(h2_p, ti_p, rk_p, cnt_p, [t_p] * n_p, tie=o_halves[0])
    o_s = jnp.concatenate([o_first] + o_halves[1:], axis=0)
    of_s = jnp.transpose(o_s.reshape(n_s * t_s, N_HEAD_PAIRS, PAIR), (1, 0, 2))[None].astype(BF16)
    x1_s, h2_s, ti_s, tw_s, rk_s, cnt_s = _outproj(xs3, of_s, y_s, bonus_s, g_s, msm[2], msm[4], msm[3],
                                                   *out_w, tm_s)
    yg_s, _ = moe(h2_s, ti_s, rk_s, cnt_s, [n_s * t_s])
    y_prompt = _final(x1_p, yg_p.reshape(TOP_K, n_p, t_p, d), tw_p, mp[5], tm_p)
    y_sample = _final(x1_s, yg_s.reshape(TOP_K, 1, n_s * t_s, d), tw_s, msm[5], tm_s)

    heads5 = lambda a, nb, nt: a.reshape(1, nb, nt, N_HEADS, HEAD_DIM)
    return (y_prompt, y_sample.reshape(n_s, t_s, d),
            heads5(k_p, n_p, t_p), heads5(v_p, n_p, t_p), lf_p[None], wkv_p[None], rw_p[:, -1][None],
            heads5(k_s, n_s, t_s), heads5(v_s, n_s, t_s), lf_s.reshape(1, n_s, t_s, N_HEADS), wkv_s[None],
            rw_s4[:, -1][None])
```

```python
import functools

import jax
import jax.numpy as jnp
from jax import lax
from jax.experimental import pallas as pl
from jax.experimental.pallas import tpu as pltpu

F32 = jnp.float32
BF16 = jnp.bfloat16

D_MODEL = 1024
HEAD_DIM = 64
N_HEADS = 8
WIDTH = N_HEADS * HEAD_DIM
PAIR = 2 * HEAD_DIM
N_HEAD_PAIRS = N_HEADS // 2
DECAY_LORA = 64
ICLR_LORA = 64
GATE_LORA = 128
FOX_COLS = 3 * WIDTH + N_HEADS
RWKV_COLS = 3 * WIDTH + DECAY_LORA + ICLR_LORA + GATE_LORA
IN_COLS_PADDED = 3 * WIDTH + RWKV_COLS + 128
FZ_OFF = 3 * WIDTH + RWKV_COLS
ATTN_SCALE = HEAD_DIM ** -0.5
N_EXPERTS = 32
TOP_K = 4
SWIGLU_ALPHA = 1.702
SWIGLU_LIMIT = 7.0
RMS_EPS = 1e-6
GN_EPS = 64e-5
N_MOD = 6
PAGE = 128
NEG = -1e30
LOG2E = 1.4426950408889634

VMEM_LIMIT = 56 * 1024 * 1024


def _cparams(*sem):
    return pltpu.CompilerParams(dimension_semantics=sem, vmem_limit_bytes=VMEM_LIMIT)


def _dot(a, b):
    return jnp.dot(a, b, preferred_element_type=F32)


def _dot_nt(a, b):
    return lax.dot_general(a, b, (((1,), (1,)), ((), ())), preferred_element_type=F32)


def _split2(x):
    hi = x.astype(BF16)
    lo = (x - hi.astype(F32)).astype(BF16)
    return hi, lo


def _split3(x):
    hi = x.astype(BF16)
    r = x - hi.astype(F32)
    mid = r.astype(BF16)
    lo = (r - mid.astype(F32)).astype(BF16)
    return hi, mid, lo


def _dot_x3(a, b):
    ah, al = _split2(a)
    bh, bl = _split2(b)
    return _dot(ah, bh) + (_dot(ah, bl) + _dot(al, bh))


def _dot_sel2(a, sel):
    hi, lo = _split2(a)
    return _dot(hi, sel) + _dot(lo, sel)


def _dot_sel3(a, sel):
    hi, mid, lo = _split3(a)
    return _dot(hi, sel) + (_dot(mid, sel) + _dot(lo, sel))


def _sigmoid(x):
    return 1.0 / (1.0 + jnp.exp(-x))


def _softplus(x):
    return jnp.maximum(x, 0.0) + jnp.log(1.0 + jnp.exp(-jnp.abs(x)))


def _head_ones(n_heads):
    w = n_heads * HEAD_DIM
    r = lax.broadcasted_iota(jnp.int32, (w, w), 0) // HEAD_DIM
    c = lax.broadcasted_iota(jnp.int32, (w, w), 1) // HEAD_DIM
    return (r == c).astype(BF16)


def _mod_kernel(c_ref, w_ref, b_ref, o_ref):
    c = c_ref[...]
    s = c * _sigmoid(c)
    o_ref[...] = _dot_x3(s, w_ref[...]) + b_ref[...]


def _mod(c, w_mod, b_mod):
    rows, d = c.shape
    n = w_mod.shape[1]
    tn = 768
    return pl.pallas_call(
        _mod_kernel,
        grid=(n // tn,),
        in_specs=[pl.BlockSpec((rows, d), lambda j: (0, 0)),
                  pl.BlockSpec((d, tn), lambda j: (0, j)),
                  pl.BlockSpec((1, tn), lambda j: (0, j))],
        out_specs=pl.BlockSpec((rows, tn), lambda j: (0, j)),
        out_shape=jax.ShapeDtypeStruct((rows, n), F32),
        compiler_params=_cparams("arbitrary"),
    )(c, w_mod, b_mod.reshape(1, n))


def _inproj_kernel(x_ref, sc_ref, sh_ref, g_ref, w_ref, wfz_ref, gq_ref, gk_ref, bf_ref, bfc_ref, hs_ref,
                   qh_ref, kh_ref, vh_ref, k_ref, v_ref, lf_ref, lft_ref, rw_ref):
    x = x_ref[0]
    ms = jnp.mean(x * x, axis=-1, keepdims=True)
    xn = x * lax.rsqrt(ms + RMS_EPS) * g_ref[...]
    h = (xn * (1.0 + sc_ref[0]) + sh_ref[0]).astype(BF16)
    proj = _dot(h, w_ref[...])
    hs = hs_ref[...]

    def head_norm(t, gain):
        m = _dot_sel2(t * t, hs) * (1.0 / HEAD_DIM)
        return t * lax.rsqrt(m + RMS_EPS) * gain

    q = head_norm(proj[:, 0:WIDTH], gq_ref[...])
    k = head_norm(proj[:, WIDTH:2 * WIDTH], gk_ref[...])
    v = proj[:, 2 * WIDTH:3 * WIDTH]
    k_ref[0] = k
    v_ref[0] = v
    qb = q.astype(BF16)
    kb = k.astype(BF16)
    vb = v.astype(BF16)
    for hp in range(N_HEAD_PAIRS):
        sl = slice(hp * PAIR, (hp + 1) * PAIR)
        qh_ref[0, hp] = qb[:, sl]
        kh_ref[0, hp] = kb[:, sl]
        vh_ref[0, hp] = vb[:, sl]
    rw_ref[0] = proj[:, 3 * WIDTH:3 * WIDTH + RWKV_COLS]
    fz = proj[:, FZ_OFF:FZ_OFF + N_HEADS] + bf_ref[...]
    lf_ref[0] = -_softplus(-fz)
    fzt = _dot_nt(wfz_ref[...], h) + bfc_ref[...]
    lft_ref[0] = -_softplus(-fzt)


def _inproj(x3, sc, sh, g_mix, w_r, wfz_t, gq, gk, b_forget, tm):
    bx, tx, d = x3.shape
    per_row = sc.shape[1] != 1
    mod_spec = (pl.BlockSpec((1, tm, d), lambda b, t: (b, t, 0)) if per_row
                else pl.BlockSpec((1, 1, d), lambda b, t: (b, 0, 0)))
    const = lambda shape: pl.BlockSpec(shape, lambda b, t: tuple(0 for _ in shape))
    head_spec = pl.BlockSpec((1, N_HEAD_PAIRS, tm, PAIR), lambda b, t: (b, 0, t, 0))
    row_spec = lambda w: pl.BlockSpec((1, tm, w), lambda b, t: (b, t, 0))
    head_shape = jax.ShapeDtypeStruct((bx, N_HEAD_PAIRS, tx, PAIR), BF16)
    return pl.pallas_call(
        _inproj_kernel,
        grid=(bx, tx // tm),
        in_specs=[row_spec(d), mod_spec, mod_spec, const((1, d)), const((d, IN_COLS_PADDED)),
                  const((N_HEADS, d)), const((1, WIDTH)), const((1, WIDTH)), const((1, N_HEADS)),
                  const((N_HEADS, 1)), const((WIDTH, WIDTH))],
        out_specs=[head_spec, head_spec, head_spec, row_spec(WIDTH), row_spec(WIDTH), row_spec(N_HEADS),
                   pl.BlockSpec((1, N_HEADS, tm), lambda b, t: (b, 0, t)), row_spec(RWKV_COLS)],
        out_shape=[head_shape, head_shape, head_shape,
                   jax.ShapeDtypeStruct((bx, tx, WIDTH), F32), jax.ShapeDtypeStruct((bx, tx, WIDTH), F32),
                   jax.ShapeDtypeStruct((bx, tx, N_HEADS), F32), jax.ShapeDtypeStruct((bx, N_HEADS, tx), F32),
                   jax.ShapeDtypeStruct((bx, tx, RWKV_COLS), F32)],
        compiler_params=_cparams("parallel", "arbitrary"),
    )(x3, sc, sh, g_mix, w_r, wfz_t, gq, gk, b_forget.reshape(1, N_HEADS), b_forget.reshape(N_HEADS, 1),
      _head_ones(N_HEADS))


def _cumsum_kernel(l_ref, u_ref, o_ref, c_ref, *, scale):
    tc = l_ref.shape[2]

    @pl.when(pl.program_id(1) == 0)
    def _():
        c_ref[...] = jnp.zeros_like(c_ref)

    f = _dot_sel3(l_ref[0], u_ref[...]) + c_ref[:, 0:1]
    o_ref[0] = f * scale
    c_ref[...] = jnp.broadcast_to(f[:, tc - 1:tc], c_ref.shape)


def _cumsum_t(lft, tc, scale):
    b, h, t = lft.shape
    r = lax.broadcasted_iota(jnp.int32, (tc, tc), 0)
    c = lax.broadcasted_iota(jnp.int32, (tc, tc), 1)
    upper = (r <= c).astype(BF16)
    return pl.pallas_call(
        functools.partial(_cumsum_kernel, scale=scale),
        grid=(b, t // tc),
        in_specs=[pl.BlockSpec((1, h, tc), lambda i, j: (i, 0, j)), pl.BlockSpec((tc, tc), lambda i, j: (0, 0))],
        out_specs=pl.BlockSpec((1, h, tc), lambda i, j: (i, 0, j)),
        out_shape=jax.ShapeDtypeStruct((b, h, t), F32),
        scratch_shapes=[pltpu.VMEM((h, 128), F32)],
        compiler_params=_cparams("parallel", "arbitrary"),
    )(lft, upper)


def _attn_kernel(q_ref, k_ref, v_ref, f_ref, o_ref, m_ref, acc_ref, *, tq, tk):
    i = pl.program_id(2)
    q2 = q_ref[0, 0]
    lane = lax.broadcasted_iota(jnp.int32, q2.shape, 1)
    zero = jnp.zeros_like(q2)
    qs = (jnp.where(lane < HEAD_DIM, q2, zero), jnp.where(lane >= HEAD_DIM, q2, zero))
    q0 = pl.multiple_of(i * tq, tq)
    f_base = [f_ref[0, 0, hh:hh + 1, pl.ds(q0, tq)][:, 0:1] for hh in range(2)]
    m_ref[...] = jnp.full_like(m_ref, NEG)
    acc_ref[...] = jnp.zeros_like(acc_ref)

    def tile(j, masked):
        k0 = pl.multiple_of(j * tk, tk)
        k2 = k_ref[0, 0, pl.ds(k0, tk), :]
        v2 = v_ref[0, 0, pl.ds(k0, tk), :]
        vlane = lax.broadcasted_iota(jnp.int32, v2.shape, 1)
        one = jnp.ones((), v2.dtype)
        vs = (jnp.where(vlane < HEAD_DIM, v2, one), jnp.where(vlane >= HEAD_DIM, v2, one))
        for hh in range(2):
            s = _dot_nt(qs[hh], k2) + (f_base[hh] - f_ref[0, 0, hh:hh + 1, pl.ds(k0, tk)])
            if masked:
                r = lax.broadcasted_iota(jnp.int32, s.shape, 0)
                c = lax.broadcasted_iota(jnp.int32, s.shape, 1)
                s = jnp.where(c + k0 <= r + q0, s, NEG)
            m_old = m_ref[hh]
            m_new = jnp.maximum(m_old, jnp.max(s, axis=1, keepdims=True))
            p = jnp.exp2(s - jnp.concatenate([m_new] * (tk // 128), axis=1))
            alpha = jnp.exp2(m_old - m_new)
            acc_ref[hh] = alpha * acc_ref[hh] + _dot(p.astype(BF16), vs[hh])
            m_ref[hh] = m_new

    def body(j, carry):
        tile(j, False)
        return carry

    n_sub = tq // tk
    lax.fori_loop(0, i * n_sub, body, 0)
    for dd in range(n_sub):
        tile(i * n_sub + dd, True)
    acc_a = acc_ref[0]
    acc_b = acc_ref[1]
    o_a = acc_a / pltpu.roll(acc_a, HEAD_DIM, 1)
    o_b = acc_b / pltpu.roll(acc_b, HEAD_DIM, 1)
    o_ref[0, 0] = jnp.where(lane < HEAD_DIM, o_a, o_b).astype(o_ref.dtype)


def _fox_prompt(qh, kh, vh, ft, tq, tk):
    b, hp, t, w = qh.shape
    assert tq % tk == 0 and tk % 128 == 0
    ft4 = ft.reshape(b, hp, 2, t)
    seq_spec = pl.BlockSpec((1, 1, t, w), lambda bi, hi, i: (bi, hi, 0, 0))
    tile_spec = pl.BlockSpec((1, 1, tq, w), lambda bi, hi, i: (bi, hi, i, 0))
    return pl.pallas_call(
        functools.partial(_attn_kernel, tq=tq, tk=tk),
        grid=(b, hp, t // tq),
        in_specs=[tile_spec, seq_spec, seq_spec, pl.BlockSpec((1, 1, 2, t), lambda bi, hi, i: (bi, hi, 0, 0))],
        out_specs=tile_spec,
        out_shape=jax.ShapeDtypeStruct((b, hp, t, w), BF16),
        scratch_shapes=[pltpu.VMEM((2, tq, 128), F32), pltpu.VMEM((2, tq, w), F32)],
        compiler_params=_cparams("parallel", "parallel", "arbitrary"),
    )(qh, kh, vh, ft4)


PAGES_PER_STEP = 8
QROWS = 4 * N_HEADS


def _decode_kernel(pt_ref, q_ref, kn_ref, vn_ref, lfn_ref, cnew_ref, ustrict_ref, fold_ref, *refs, n_steps):
    del pt_ref
    npg = PAGES_PER_STEP
    k_refs = refs[0:npg]
    v_refs = refs[npg:2 * npg]
    lf_refs = refs[2 * npg:3 * npg]
    o_ref, m_ref, l_ref, acc_ref, accn_ref, carry_ref = refs[3 * npg:]
    j = pl.program_id(1)
    q = q_ref[0]
    r = lax.broadcasted_iota(jnp.int32, (QROWS, WIDTH), 0)
    c = lax.broadcasted_iota(jnp.int32, (QROWS, WIDTH), 1)
    own_head = (r % N_HEADS) == (c // HEAD_DIM)
    q_bd = jnp.where(own_head, jnp.concatenate([q] * N_HEADS, axis=1), jnp.zeros((), q.dtype))

    def softmax_step(s):
        m_old = m_ref[...]
        m_new = jnp.maximum(m_old, jnp.max(s, axis=1, keepdims=True))
        p = jnp.exp(s - m_new)
        alpha = jnp.exp(m_old - m_new)
        l_ref[...] = alpha * l_ref[...] + jnp.sum(p, axis=1, keepdims=True)
        m_ref[...] = m_new
        return p.astype(BF16), alpha

    @pl.when(j == 0)
    def _():
        m_ref[...] = jnp.full_like(m_ref, NEG)
        l_ref[...] = jnp.zeros_like(l_ref)
        acc_ref[...] = jnp.zeros_like(acc_ref)
        carry_ref[...] = jnp.zeros_like(carry_ref)
        s = _dot_nt(q, kn_ref[0])
        bias = _dot_sel3(lfn_ref[0], cnew_ref[...])[0:1, :]
        rr = lax.broadcasted_iota(jnp.int32, s.shape, 0)
        cc = lax.broadcasted_iota(jnp.int32, s.shape, 1)
        ok = ((rr % N_HEADS) == (cc % N_HEADS)) & ((cc // N_HEADS) <= (rr // N_HEADS))
        p, _ = softmax_step(jnp.where(ok, s + bias, NEG))
        accn_ref[...] = _dot(p, vn_ref[0])

    ustrict = ustrict_ref[...]
    carry = carry_ref[...]
    scores = []
    for u in range(npg):
        kt = k_refs[u][...].reshape(WIDTH, PAGE).astype(BF16)
        lft = lf_refs[u][...]
        a = jnp.broadcast_to(lft[None], (4, N_HEADS, PAGE)).reshape(QROWS, PAGE)
        hi, mid, lo = _split3(a)
        bx = _dot(jnp.concatenate([hi, mid, lo], axis=0), ustrict)
        bias = bx[0:QROWS] + (bx[QROWS:2 * QROWS] + bx[2 * QROWS:3 * QROWS]) + carry
        carry = carry + jnp.sum(a, axis=1, keepdims=True)
        scores.append(_dot(q_bd, kt) + bias)
    carry_ref[...] = carry
    p, alpha = softmax_step(jnp.concatenate(scores, axis=1))
    acc = alpha * acc_ref[...]
    for u in range(npg):
        vt = v_refs[u][...].reshape(WIDTH, PAGE).astype(BF16)
        acc = acc + _dot_nt(p[:, u * PAGE:(u + 1) * PAGE], vt)
    acc_ref[...] = acc
    accn_ref[...] = alpha * accn_ref[...]

    @pl.when(j == n_steps - 1)
    def _():
        own = jnp.where(own_head, acc_ref[...], 0.0)
        o_ref[0] = (_dot_sel3(own, fold_ref[...]) + accn_ref[...]) / l_ref[...]


def _fox_decode(page_table, q, k_new, v_new, lf_new, cache_kt, cache_vt, cache_lft):
    nb, n_pages = page_table.shape
    npg = PAGES_PER_STEP
    n_steps = n_pages // npg
    ustrict = (lax.broadcasted_iota(jnp.int32, (PAGE, PAGE), 0)
               > lax.broadcasted_iota(jnp.int32, (PAGE, PAGE), 1)).astype(BF16)
    ri = lax.broadcasted_iota(jnp.int32, (QROWS, QROWS), 0)
    ci = lax.broadcasted_iota(jnp.int32, (QROWS, QROWS), 1)
    cnew = -(((ri % N_HEADS) == (ci % N_HEADS)) & ((ri // N_HEADS) <= (ci // N_HEADS))).astype(BF16)
    fold = (lax.broadcasted_iota(jnp.int32, (WIDTH, HEAD_DIM), 0) % HEAD_DIM
            == lax.broadcasted_iota(jnp.int32, (WIDTH, HEAD_DIM), 1)).astype(BF16)

    def page_idx(u):
        return lambda b, j, pt: (pt[b, n_pages - 1 - (j * npg + u)], 0, 0, 0)

    def page_idx3(u):
        return lambda b, j, pt: (pt[b, n_pages - 1 - (j * npg + u)], 0, 0)

    per_b = lambda shape: pl.BlockSpec((1,) + shape, lambda b, j, pt: (b, 0, 0))
    const = lambda shape: pl.BlockSpec(shape, lambda b, j, pt: (0, 0))
    in_specs = ([per_b((QROWS, HEAD_DIM)), per_b((QROWS, HEAD_DIM)), per_b((QROWS, HEAD_DIM)),
                 per_b((8, QROWS)), const((QROWS, QROWS)), const((PAGE, PAGE)), const((WIDTH, HEAD_DIM))]
                + [pl.BlockSpec((None, N_HEADS, HEAD_DIM, PAGE), page_idx(u)) for u in range(npg)]
                + [pl.BlockSpec((None, N_HEADS, HEAD_DIM, PAGE), page_idx(u)) for u in range(npg)]
                + [pl.BlockSpec((None, N_HEADS, PAGE), page_idx3(u)) for u in range(npg)])
    grid_spec = pltpu.PrefetchScalarGridSpec(
        num_scalar_prefetch=1,
        grid=(nb, n_steps),
        in_specs=in_specs,
        out_specs=pl.BlockSpec((1, QROWS, HEAD_DIM), lambda b, j, pt: (b, 0, 0)),
        scratch_shapes=[pltpu.VMEM((QROWS, 1), F32), pltpu.VMEM((QROWS, 1), F32),
                        pltpu.VMEM((QROWS, WIDTH), F32), pltpu.VMEM((QROWS, HEAD_DIM), F32),
                        pltpu.VMEM((QROWS, 1), F32)],
    )
    lfn = jnp.broadcast_to(lf_new.reshape(nb, 1, QROWS), (nb, 8, QROWS))
    return pl.pallas_call(
        functools.partial(_decode_kernel, n_steps=n_steps),
        grid_spec=grid_spec,
        out_shape=jax.ShapeDtypeStruct((nb, QROWS, HEAD_DIM), F32),
        compiler_params=_cparams("parallel", "arbitrary"),
    )(page_table, q, k_new, v_new, lfn, cnew, ustrict, fold,
      *([cache_kt] * npg), *([cache_vt] * npg), *([cache_lft] * npg))


def _rwkv_prep_kernel(p_ref, prev_ref, first_ref, mu_ref, w0_ref, w2_ref, a0_ref, a2_ref, g2_ref, kk_ref, ka_ref,
                      rk_ref, hs_ref, r_ref, k_ref, v_ref, w_ref, a_ref, b_ref, g_ref, bonus_ref, *, shift_rows):
    p = p_ref[0]
    if shift_rows:
        carry = jnp.where(pl.program_id(1) == 0, first_ref[0], prev_ref[0, 7:8, :])
        row0 = lax.broadcasted_iota(jnp.int32, p.shape, 0) == 0
        prev = jnp.where(row0, carry, pltpu.roll(p, 1, 0))
    else:
        prev = prev_ref[0]
    z = p + (prev - p) * mu_ref[...]
    r = z[:, 0:WIDTH]
    k = z[:, WIDTH:2 * WIDTH]
    v = z[:, 2 * WIDTH:3 * WIDTH]
    lora = z[:, 3 * WIDTH:3 * WIDTH + DECAY_LORA + ICLR_LORA]
    gd = z[:, 3 * WIDTH + DECAY_LORA + ICLR_LORA:RWKV_COLS]
    hs = hs_ref[...]
    w = -_softplus(-(w0_ref[...] + _dot_x3(jnp.tanh(lora), w2_ref[...]))) - 0.5
    decay = jnp.exp(-jnp.exp(w))
    a = _sigmoid(a0_ref[...] + _dot_x3(lora, a2_ref[...]))
    g = _dot_x3(_sigmoid(gd), g2_ref[...])
    kk = k * kk_ref[...]
    norm = jnp.sqrt(_dot_sel3(kk * kk, hs))
    kk = kk / jnp.maximum(norm, 1e-12)
    k = k * (1.0 + (a - 1.0) * ka_ref[...])
    bonus = _dot_sel3(r * k * rk_ref[...], hs) * v
    r_ref[0] = r
    k_ref[0] = k
    v_ref[0] = v
    w_ref[0] = decay
    a_ref[0] = -kk
    b_ref[0] = kk * a
    g_ref[0] = g
    bonus_ref[0] = bonus


def _rwkv_prep(p_rw, prev, first, mu, w0, w2p, a0, a2p, g2, kk, ka, rk, tm):
    bx, tx, _ = p_rw.shape
    row = lambda w: pl.BlockSpec((1, tm, w), lambda b, t: (b, t, 0))
    const = lambda shape: pl.BlockSpec(shape, lambda b, t: tuple(0 for _ in shape))
    out = jax.ShapeDtypeStruct((bx, tx, WIDTH), F32)
    shift_rows = prev is None
    if shift_rows:
        prev = p_rw
        prev_spec = pl.BlockSpec((1, 8, RWKV_COLS), lambda b, t: (b, jnp.maximum(t * (tm // 8) - 1, 0), 0))
    else:
        prev_spec = row(RWKV_COLS)
    return pl.pallas_call(
        functools.partial(_rwkv_prep_kernel, shift_rows=shift_rows),
        grid=(bx, tx // tm),
        in_specs=[row(RWKV_COLS), prev_spec, pl.BlockSpec((1, 1, RWKV_COLS), lambda b, t: (b, 0, 0)),
                  const((1, RWKV_COLS)), const((1, WIDTH)),
                  const((DECAY_LORA + ICLR_LORA, WIDTH)), const((1, WIDTH)), const((DECAY_LORA + ICLR_LORA, WIDTH)),
                  const((GATE_LORA, WIDTH)), const((1, WIDTH)), const((1, WIDTH)), const((1, WIDTH)),
                  const((WIDTH, WIDTH))],
        out_specs=[row(WIDTH)] * 8,
        out_shape=[out] * 8,
        compiler_params=_cparams("parallel", "arbitrary"),
    )(p_rw, prev, first, mu, w0, w2p, a0, a2p, g2, kk, ka, rk, _head_ones(N_HEADS))


WKV_NB = 2
WKV_HG = 4
WKV_LANES = WKV_HG * HEAD_DIM
WKV_GROUPS = N_HEADS // WKV_HG
WKV_ROWS = WKV_NB * WKV_GROUPS * HEAD_DIM


def _wkv_kernel(r_ref, k_ref, v_ref, w_ref, a_ref, b_ref, s0_ref, ones_ref, y_ref, sT_ref, st_ref, *, tc, n_t):
    ti = pl.program_id(1)
    blocks = [(bi, gg) for bi in range(WKV_NB) for gg in range(WKV_GROUPS)]

    @pl.when(ti == 0)
    def _():
        for n, (bi, gg) in enumerate(blocks):
            st_ref[n * HEAD_DIM:(n + 1) * HEAD_DIM, :] = jnp.concatenate(
                [s0_ref[bi, gg * WKV_HG + q] for q in range(WKV_HG)], axis=1)

    rows = lax.broadcasted_iota(jnp.int32, (WKV_ROWS, WKV_LANES), 0)
    lanes = lax.broadcasted_iota(jnp.int32, (WKV_ROWS, WKV_LANES), 1)
    eye = (rows % HEAD_DIM) == (lanes % HEAD_DIM)
    eye_f = eye.astype(F32)
    ones = ones_ref[...]
    grp = min(tc, 8)

    def bcast(xs, u):
        return jnp.concatenate(
            [jnp.broadcast_to(xs[bi][u:u + 1, gg * WKV_LANES:(gg + 1) * WKV_LANES], (HEAD_DIM, WKV_LANES))
             for bi, gg in blocks], axis=0)

    def group(gi, carry):
        t0 = pl.multiple_of(gi * grp, grp)
        load = lambda ref: [ref[bi, pl.ds(t0, grp), :] for bi in range(WKV_NB)]
        r8, k8, v8, w8, a8, b8 = (load(ref) for ref in (r_ref, k_ref, v_ref, w_ref, a_ref, b_ref))
        vdiag = lambda u: jnp.where(eye, bcast(v8, u), 0.0).astype(BF16)
        s = st_ref[...]
        ys = []

        def y_rows(yb):
            yb = yb * eye_f
            return [jnp.sum(yb[n * HEAD_DIM:(n + 1) * HEAD_DIM], axis=0, keepdims=True) for n in range(len(blocks))]

        vcol = _dot(vdiag(0), ones)
        ybf = None
        for u in range(grp):
            sa = _dot((s * bcast(a8, u)).astype(BF16), ones)
            side = [vdiag(u + 1)] if u + 1 < grp else []
            if ybf is not None:
                side.append(ybf)
            misc = _dot(jnp.concatenate(side, axis=0), ones) if side else None
            s = s * bcast(w8, u) + sa * bcast(b8, u) + vcol * bcast(k8, u)
            if misc is not None:
                off = 0
                if u + 1 < grp:
                    vcol = misc[0:WKV_ROWS]
                    off = WKV_ROWS
                if ybf is not None:
                    ys.append(y_rows(misc[off:off + WKV_ROWS]))
            ybf = (s * bcast(r8, u)).astype(BF16)
        ys.append(y_rows(_dot(ybf, ones)))
        st_ref[...] = s
        for bi in range(WKV_NB):
            y_ref[bi, pl.ds(t0, grp), :] = jnp.concatenate(
                [jnp.concatenate([ys[u][bi * WKV_GROUPS + gg] for gg in range(WKV_GROUPS)], axis=1)
                 for u in range(grp)], axis=0)
        return carry

    lax.fori_loop(0, tc // grp, group, 0)

    @pl.when(ti == n_t - 1)
    def _():
        for n, (bi, gg) in enumerate(blocks):
            for q in range(WKV_HG):
                sT_ref[bi, gg * WKV_HG + q] = st_ref[n * HEAD_DIM:(n + 1) * HEAD_DIM, q * HEAD_DIM:(q + 1) * HEAD_DIM]


def _wkv_scan(r, k, v, w, a, b, s0, tc):
    bx, tx, _ = r.shape
    assert bx % WKV_NB == 0
    n_t = tx // tc
    seq = pl.BlockSpec((WKV_NB, tc, WIDTH), lambda i, t: (i, t, 0))
    st = pl.BlockSpec((WKV_NB, N_HEADS, HEAD_DIM, HEAD_DIM), lambda i, t: (i, 0, 0, 0))
    ri = lax.broadcasted_iota(jnp.int32, (WKV_LANES, WKV_LANES), 0) // HEAD_DIM
    ci = lax.broadcasted_iota(jnp.int32, (WKV_LANES, WKV_LANES), 1) // HEAD_DIM
    ones = (ri == ci).astype(BF16)
    return pl.pallas_call(
        functools.partial(_wkv_kernel, tc=tc, n_t=n_t),
        grid=(bx // WKV_NB, n_t),
        in_specs=[seq] * 6 + [st, pl.BlockSpec((WKV_LANES, WKV_LANES), lambda i, t: (0, 0))],
        out_specs=[seq, st],
        out_shape=[jax.ShapeDtypeStruct((bx, tx, WIDTH), F32),
                   jax.ShapeDtypeStruct((bx, N_HEADS, HEAD_DIM, HEAD_DIM), F32)],
        scratch_shapes=[pltpu.VMEM((WKV_ROWS, WKV_LANES), F32)],
        compiler_params=_cparams("parallel", "arbitrary"),
    )(r, k, v, w, a, b, s0, ones)


def _outproj_kernel(x_ref, of_ref, y_ref, bonus_ref, g_ref, gt_ref, sc_ref, sh_ref, lw_ref, lb_ref, gf_ref,
                    wo_ref, wr_ref, br_ref, hs_ref, lt_ref, x1_ref, h2_ref, ti_ref, tw_ref, rk_ref, cnt_ref, run_ref):
    @pl.when(pl.program_id(1) == 0)
    def _():
        run_ref[...] = jnp.zeros_like(run_ref)

    hs = hs_ref[...]
    y = y_ref[0]
    mu = _dot_sel2(y, hs) * (1.0 / HEAD_DIM)
    yc = y - mu
    var = _dot_sel2(yc * yc, hs) * (1.0 / HEAD_DIM)
    o_rw = ((yc * lax.rsqrt(var + GN_EPS)) * lw_ref[...] + lb_ref[...] + bonus_ref[0]) * g_ref[0]
    mixed = _dot(o_rw.astype(BF16), wo_ref[WIDTH:2 * WIDTH, :])
    for hp in range(N_HEAD_PAIRS):
        mixed = mixed + _dot(of_ref[0, hp], wo_ref[hp * PAIR:(hp + 1) * PAIR, :])
    x1 = x_ref[0] + gt_ref[0] * mixed
    x1_ref[0] = x1
    ms = jnp.mean(x1 * x1, axis=-1, keepdims=True)
    h2 = (x1 * lax.rsqrt(ms + RMS_EPS) * gf_ref[...]) * (1.0 + sc_ref[0]) + sh_ref[0]
    h2_ref[0] = h2.astype(BF16)
    logits = _dot_x3(h2, wr_ref[...]) + br_ref[...]
    lane = lax.broadcasted_iota(jnp.int32, logits.shape, 1).astype(F32)
    vals, idxs = [], []
    for _ in range(TOP_K):
        m = jnp.max(logits, axis=1, keepdims=True)
        idx = jnp.min(jnp.where(logits == m, lane, float(N_EXPERTS)), axis=1, keepdims=True)
        vals.append(m)
        idxs.append(idx)
        logits = jnp.where(lane == idx, -jnp.inf, logits)
    e = [jnp.exp(vv - vals[0]) for vv in vals]
    tot = e[0] + e[1] + e[2] + e[3]
    tw_ref[0] = jnp.concatenate(e, axis=1) / tot
    ti_ref[0] = jnp.concatenate(idxs, axis=1).astype(jnp.int32)
    chosen = [lane == idx for idx in idxs]
    multi_hot = jnp.where(chosen[0] | chosen[1] | chosen[2] | chosen[3], 1.0, 0.0)
    before = _dot(lt_ref[...], multi_hot.astype(BF16)) + run_ref[...]
    ranks = [jnp.sum(jnp.where(c, before, 0.0), axis=1, keepdims=True) for c in chosen]
    rk_ref[0] = jnp.concatenate(ranks, axis=1).astype(jnp.int32)
    run_ref[...] = run_ref[...] + jnp.sum(multi_hot, axis=0, keepdims=True)
    cnt_ref[0] = run_ref[...]


def _outproj(x3, of, y, bonus, g, gt, sc, sh, lnx_w, lnx_b, g_ffn, w_out_b, w_router, b_router, tm):
    bx, tx, d = x3.shape
    per_row = gt.shape[1] != 1
    mod_spec = (pl.BlockSpec((1, tm, d), lambda b, t: (b, t, 0)) if per_row
                else pl.BlockSpec((1, 1, d), lambda b, t: (b, 0, 0)))
    row = lambda w: pl.BlockSpec((1, tm, w), lambda b, t: (b, t, 0))
    const = lambda shape: pl.BlockSpec(shape, lambda b, t: tuple(0 for _ in shape))
    lower = (lax.broadcasted_iota(jnp.int32, (tm, tm), 0) > lax.broadcasted_iota(jnp.int32, (tm, tm), 1)).astype(BF16)
    return pl.pallas_call(
        _outproj_kernel,
        grid=(bx, tx // tm),
        in_specs=[row(d), pl.BlockSpec((1, N_HEAD_PAIRS, tm, PAIR), lambda b, t: (b, 0, t, 0)),
                  row(WIDTH), row(WIDTH), row(WIDTH), mod_spec, mod_spec, mod_spec,
                  const((1, WIDTH)), const((1, WIDTH)), const((1, d)), const((2 * WIDTH, d)),
                  const((d, N_EXPERTS)), const((1, N_EXPERTS)), const((WIDTH, WIDTH)), const((tm, tm))],
        out_specs=[row(d), row(d), row(TOP_K), row(TOP_K), row(TOP_K),
                   pl.BlockSpec((1, 1, N_EXPERTS), lambda b, t: (b, 0, 0))],
        out_shape=[jax.ShapeDtypeStruct((bx, tx, d), F32), jax.ShapeDtypeStruct((bx, tx, d), BF16),
                   jax.ShapeDtypeStruct((bx, tx, TOP_K), jnp.int32), jax.ShapeDtypeStruct((bx, tx, TOP_K), F32),
                   jax.ShapeDtypeStruct((bx, tx, TOP_K), jnp.int32), jax.ShapeDtypeStruct((bx, 1, N_EXPERTS), F32)],
        scratch_shapes=[pltpu.VMEM((1, N_EXPERTS), F32)],
        compiler_params=_cparams("parallel", "arbitrary"),
    )(x3, of, y, bonus, g, gt, sc, sh, lnx_w, lnx_b, g_ffn, w_out_b, w_router, b_router, _head_ones(N_HEADS), lower)


MOE_TM = 512


def _moe_kernel(te_ref, tf_ref, x_ref, wg_ref, bg_ref, wu_ref, bu_ref, wd_ref, bd_ref, o_ref, wb_ref):
    del te_ref
    i = pl.program_id(0)

    @pl.when(tf_ref[i] == 2)
    def _():
        wb_ref[0] = wg_ref[0].astype(BF16)
        wb_ref[1] = wu_ref[0].astype(BF16)
        wb_ref[2] = wd_ref[0].astype(BF16)

    @pl.when(tf_ref[i] != 0)
    def _():
        x = x_ref[...]
        glu = jnp.minimum(_dot(x, wb_ref[0]) + bg_ref[0], SWIGLU_LIMIT)
        lin = jnp.clip(_dot(x, wb_ref[1]) + bu_ref[0], -SWIGLU_LIMIT, SWIGLU_LIMIT)
        act = glu * _sigmoid(SWIGLU_ALPHA * glu) * (lin + 1.0)
        o_ref[...] = (_dot(act.astype(BF16), wb_ref[2]) + bd_ref[0]).astype(o_ref.dtype)

    @pl.when(tf_ref[i] == 0)
    def _():
        o_ref[...] = jnp.zeros_like(o_ref)


def _moe_gemm(tile_expert, tile_flag, xg, wg, bg, wu, bu, wd, bd):
    n_slots, d = xg.shape
    n_tiles = n_slots // MOE_TM
    wspec = pl.BlockSpec((1, d, d), lambda i, te, tf: (te[i], 0, 0))
    bspec = pl.BlockSpec((1, 1, d), lambda i, te, tf: (te[i], 0, 0))
    grid_spec = pltpu.PrefetchScalarGridSpec(
        num_scalar_prefetch=2,
        grid=(n_tiles,),
        in_specs=[pl.BlockSpec((MOE_TM, d), lambda i, te, tf: (i, 0)), wspec, bspec, wspec, bspec, wspec, bspec],
        out_specs=pl.BlockSpec((MOE_TM, d), lambda i, te, tf: (i, 0)),
        scratch_shapes=[pltpu.VMEM((3, d, d), BF16)],
    )
    return pl.pallas_call(
        _moe_kernel,
        grid_spec=grid_spec,
        out_shape=jax.ShapeDtypeStruct((n_slots, d), BF16),
        compiler_params=_cparams("arbitrary"),
    )(tile_expert, tile_flag, xg, wg, bg, wu, bu, wd, bd)


def _final_kernel(x1_ref, yg_ref, tw_ref, gt_ref, o_ref):
    tw = tw_ref[0]
    moe = tw[:, 0:1] * yg_ref[0, 0].astype(F32)
    for kk in range(1, TOP_K):
        moe = moe + tw[:, kk:kk + 1] * yg_ref[kk, 0].astype(F32)
    o_ref[0] = x1_ref[0] + gt_ref[0] * moe


def _final(x1, yg, tw, gt, tm):
    bx, tx, d = x1.shape
    per_row = gt.shape[1] != 1
    mod_spec = (pl.BlockSpec((1, tm, d), lambda b, t: (b, t, 0)) if per_row
                else pl.BlockSpec((1, 1, d), lambda b, t: (b, 0, 0)))
    row = pl.BlockSpec((1, tm, d), lambda b, t: (b, t, 0))
    return pl.pallas_call(
        _final_kernel,
        grid=(bx, tx // tm),
        in_specs=[row, pl.BlockSpec((TOP_K, 1, tm, d), lambda b, t: (0, b, t, 0)),
                  pl.BlockSpec((1, tm, TOP_K), lambda b, t: (b, t, 0)), mod_spec],
        out_specs=row,
        out_shape=jax.ShapeDtypeStruct((bx, tx, d), F32),
        compiler_params=_cparams("parallel", "arbitrary"),
    )(x1, yg, tw, gt)


def _dispatch_plan(top_i, rank_in_seg, seg_counts, seg_tokens):
    n_assign = top_i.size
    n_tiles = -(-n_assign // MOE_TM) + N_EXPERTS
    n_slots = n_tiles * MOE_TM
    e_flat = top_i.reshape(-1)
    order = jnp.argsort(e_flat, stable=True).astype(jnp.int32)
    counts = jnp.sum(seg_counts, axis=0)
    cnt_start = jnp.cumsum(counts) - counts
    seg_base = jnp.cumsum(seg_counts, axis=0) - seg_counts
    padded = ((counts + MOE_TM - 1) // MOE_TM) * MOE_TM
    pad_end = jnp.cumsum(padded)
    pad_start = pad_end - padded
    experts = jnp.arange(N_EXPERTS, dtype=jnp.int32)
    tile_start = jnp.arange(n_tiles, dtype=jnp.int32) * MOE_TM
    tile_expert = jnp.minimum(jnp.sum((tile_start[:, None] >= pad_end[None, :]).astype(jnp.int32), axis=1),
                              N_EXPERTS - 1)
    tile_live = tile_start < pad_end[-1]
    is_first = jnp.concatenate([jnp.ones((1,), bool), tile_expert[1:] != tile_expert[:-1]])
    tile_flag = jnp.where(tile_live, jnp.where(is_first, 2, 1), 0).astype(jnp.int32)
    of_tile = lambda tab: jnp.sum(jnp.where(tile_expert[:, None] == experts[None, :], tab[None, :], 0), axis=1)
    rank = (tile_start - of_tile(pad_start))[:, None] + jnp.arange(MOE_TM, dtype=jnp.int32)[None, :]
    live = rank < of_tile(counts)[:, None]
    src = order.at[jnp.clip(of_tile(cnt_start)[:, None] + rank, 0, n_assign - 1)].get(mode="promise_in_bounds")
    token_of_slot = jnp.where(live, src // TOP_K, 0).reshape(n_slots)
    tok_base = jnp.repeat(seg_base + pad_start[None, :], jnp.asarray(seg_tokens), axis=0,
                          total_repeat_length=sum(seg_tokens))
    hit = top_i[:, :, None] == experts[None, None, :]
    slot_of_assign = jnp.sum(jnp.where(hit, tok_base[:, None, :], 0), axis=-1) + rank_in_seg
    return slot_of_assign, token_of_slot, tile_expert, tile_flag


def _pick_tile(n, pref):
    t = min(n, pref)
    assert n % t == 0, (n, t)
    return t


def kernel(x_prompt, x_sample, cache_k, cache_v, cache_logf, state_wkv, state_shift, page_table, c_prompt,
           c_sample, w_mod, b_mod, g_mix, g_ffn, w_in, q_norm, k_norm, b_forget, rw_mu, rw_w0, rw_w2, rw_a0,
           rw_a2, rw_g2, rw_kk, rw_ka, rw_rk, lnx_w, lnx_b, w_out, w_router, b_router, w_gate, b_gate, w_up,
           b_up, w_down, b_down):
    depth = w_mod.shape[0]
    assert depth == 1
    n_p, t_p, d = x_prompt.shape
    n_s, t_s, _ = x_sample.shape
    assert d == D_MODEL and t_s * N_HEADS == QROWS
    l = 0

    w_in_l = w_in[l]
    w_r = jnp.concatenate([w_in_l[:, :3 * WIDTH], w_in_l[:, FOX_COLS:], w_in_l[:, 3 * WIDTH:FOX_COLS],
                           jnp.zeros((d, 128 - N_HEADS), F32)], axis=1).astype(BF16)
    wfz_t = w_in_l[:, 3 * WIDTH:FOX_COLS].T.astype(BF16)
    gq = (jnp.tile(q_norm[l], N_HEADS) * ATTN_SCALE).reshape(1, WIDTH)
    gk = jnp.tile(k_norm[l], N_HEADS).reshape(1, WIDTH)
    zpad = jnp.zeros((DECAY_LORA, WIDTH), F32)
    w2p = jnp.concatenate([rw_w2[l], zpad], axis=0)
    a2p = jnp.concatenate([zpad, rw_a2[l]], axis=0)
    row = lambda a: a.reshape(1, -1)

    n_c = n_p + n_s
    pad_c = (-n_c) % 8
    c_all = jnp.concatenate([c_sample, c_prompt, jnp.zeros((pad_c, d), F32)], axis=0)
    mod = _mod(c_all, w_mod[l], b_mod[l])
    mod_s = mod[:n_s].reshape(n_s, 1, N_MOD, d)
    mod_p = mod[n_s:n_s + n_p].reshape(n_p, 1, N_MOD, d)
    mod_s = jnp.broadcast_to(mod_s, (n_s, t_s, N_MOD, d)).reshape(1, n_s * t_s, N_MOD, d)
    mp = [mod_p[:, :, i] for i in range(N_MOD)]
    msm = [mod_s[:, :, i] for i in range(N_MOD)]

    xs3 = x_sample.reshape(1, n_s * t_s, d)
    tm_p = _pick_tile(t_p, 256)
    tm_s = _pick_tile(n_s * t_s, 256)

    proj_args = (row(g_mix[l]), w_r, wfz_t, gq, gk, b_forget[l])
    proj_args_p = (row(g_mix[l]), w_r, wfz_t, gq * LOG2E, gk, b_forget[l])
    qh_p, kh_p, vh_p, k_p, v_p, lf_p, lft_p, rw_p = _inproj(x_prompt, mp[1], mp[0], *proj_args_p, tm_p)
    qh_s, _, _, k_s, v_s, lf_s, _, rw_s = _inproj(xs3, msm[1], msm[0], *proj_args, tm_s)

    ft = _cumsum_t(lft_p, _pick_tile(t_p, 512), LOG2E)
    tq = _pick_tile(t_p, 1024)
    of_p = _fox_prompt(qh_p, kh_p, vh_p, ft, tq, tq)

    q_s = jnp.transpose(qh_s[0], (1, 0, 2)).reshape(n_s, QROWS, HEAD_DIM)
    kn_s = k_s.reshape(n_s, QROWS, HEAD_DIM).astype(BF16)
    vn_s = v_s.reshape(n_s, QROWS, HEAD_DIM).astype(BF16)
    cache_kt = jnp.transpose(cache_k[l], (0, 2, 3, 1))
    cache_vt = jnp.transpose(cache_v[l], (0, 2, 3, 1))
    cache_lft = jnp.swapaxes(cache_logf[l], 1, 2)
    o_s = _fox_decode(page_table, q_s, kn_s, vn_s, lf_s.reshape(n_s, QROWS), cache_kt, cache_vt, cache_lft)
    of_s = jnp.transpose(o_s.reshape(n_s * t_s, N_HEAD_PAIRS, PAIR), (1, 0, 2))[None].astype(BF16)

    prep_w = (row(rw_mu[l]), row(rw_w0[l]), w2p, row(rw_a0[l]), a2p, rw_g2[l], row(rw_kk[l]), row(rw_ka[l]),
              row(rw_rk[l]))
    rw_s4 = rw_s.reshape(n_s, t_s, RWKV_COLS)
    prev_s = jnp.concatenate([state_shift[l][:, None, :], rw_s4[:, :-1]], axis=1).reshape(1, n_s * t_s, RWKV_COLS)
    r_p, kk_p, vv_p, w_p, a_p, b_p, g_p, bonus_p = _rwkv_prep(
        rw_p, None, jnp.zeros((n_p, 1, RWKV_COLS), F32), *prep_w, tm_p)
    r_s, kk_s, vv_s, w_s, a_s, b_s, g_s, bonus_s = _rwkv_prep(
        rw_s, prev_s, jnp.zeros((1, 1, RWKV_COLS), F32), *prep_w, tm_s)
    y_p, wkv_p = _wkv_scan(r_p, kk_p, vv_p, w_p, a_p, b_p,
                           jnp.zeros((n_p, N_HEADS, HEAD_DIM, HEAD_DIM), F32), _pick_tile(t_p, 256))
    to_seq = lambda a: a.reshape(n_s, t_s, WIDTH)
    y_s, wkv_s = _wkv_scan(to_seq(r_s), to_seq(kk_s), to_seq(vv_s), to_seq(w_s), to_seq(a_s), to_seq(b_s),
                           state_wkv[l], t_s)
    y_s = y_s.reshape(1, n_s * t_s, WIDTH)

    out_w = (row(lnx_w[l]), row(lnx_b[l]), row(g_ffn[l]), w_out[l].astype(BF16), w_router[l], row(b_router[l]))
    x1_p, h2_p, ti_p, tw_p, rk_p, cnt_p = _outproj(x_prompt, of_p, y_p, bonus_p, g_p, mp[2], mp[4], mp[3],
                                                   *out_w, tm_p)
    x1_s, h2_s, ti_s, tw_s, rk_s, cnt_s = _outproj(xs3, of_s, y_s, bonus_s, g_s, msm[2], msm[4], msm[3],
                                                   *out_w, tm_s)

    experts_w = (w_gate[l], b_gate[l].reshape(N_EXPERTS, 1, d), w_up[l], b_up[l].reshape(N_EXPERTS, 1, d),
                 w_down[l], b_down[l].reshape(N_EXPERTS, 1, d))
    take_rows = lambda a, idx: a.at[idx].get(mode="promise_in_bounds")

    def moe(h2, top_i, rank, counts, seg_tokens):
        n_tok = sum(seg_tokens)
        slot_of_assign, token_of_slot, tile_expert, tile_flag = _dispatch_plan(
            top_i.reshape(n_tok, TOP_K), rank.reshape(n_tok, TOP_K),
            counts.reshape(len(seg_tokens), N_EXPERTS).astype(jnp.int32), seg_tokens)
        ys = _moe_gemm(tile_expert, tile_flag, take_rows(h2.reshape(n_tok, d), token_of_slot), *experts_w)
        return take_rows(ys, slot_of_assign.T)

    yg_p = moe(h2_p, ti_p, rk_p, cnt_p, [t_p] * n_p)
    yg_s = moe(h2_s, ti_s, rk_s, cnt_s, [n_s * t_s])
    y_prompt = _final(x1_p, yg_p.reshape(TOP_K, n_p, t_p, d), tw_p, mp[5], tm_p)
    y_sample = _final(x1_s, yg_s.reshape(TOP_K, 1, n_s * t_s, d), tw_s, msm[5], tm_s)

    heads5 = lambda a, nb, nt: a.reshape(1, nb, nt, N_HEADS, HEAD_DIM)
    return (y_prompt, y_sample.reshape(n_s, t_s, d),
            heads5(k_p, n_p, t_p), heads5(v_p, n_p, t_p), lf_p[None], wkv_p[None], rw_p[:, -1][None],
            heads5(k_s, n_s, t_s), heads5(v_s, n_s, t_s), lf_s.reshape(1, n_s, t_s, N_HEADS), wkv_s[None],
            rw_s4[:, -1][None])
```

```python
import functools

import jax
import jax.numpy as jnp
from jax import lax
from jax.experimental import pallas as pl
from jax.experimental.pallas import tpu as pltpu

F32 = jnp.float32
BF16 = jnp.bfloat16

D_MODEL = 1024
HEAD_DIM = 64
N_HEADS = 8
WIDTH = N_HEADS * HEAD_DIM
PAIR = 2 * HEAD_DIM
N_HEAD_PAIRS = N_HEADS // 2
DECAY_LORA = 64
ICLR_LORA = 64
GATE_LORA = 128
FOX_COLS = 3 * WIDTH + N_HEADS
RWKV_COLS = 3 * WIDTH + DECAY_LORA + ICLR_LORA + GATE_LORA
IN_COLS_PADDED = 3 * WIDTH + RWKV_COLS + 128
FZ_OFF = 3 * WIDTH + RWKV_COLS
ATTN_SCALE = HEAD_DIM ** -0.5
N_EXPERTS = 32
TOP_K = 4
SWIGLU_ALPHA = 1.702
SWIGLU_LIMIT = 7.0
RMS_EPS = 1e-6
GN_EPS = 64e-5
N_MOD = 6
PAGE = 128
NEG = -1e30
LOG2E = 1.4426950408889634

VMEM_LIMIT = 56 * 1024 * 1024


def _cparams(*sem):
    return pltpu.CompilerParams(dimension_semantics=sem, vmem_limit_bytes=VMEM_LIMIT)


def _dot(a, b):
    return jnp.dot(a, b, preferred_element_type=F32)


def _dot_nt(a, b):
    return lax.dot_general(a, b, (((1,), (1,)), ((), ())), preferred_element_type=F32)


def _split2(x):
    hi = x.astype(BF16)
    lo = (x - hi.astype(F32)).astype(BF16)
    return hi, lo


def _split3(x):
    hi = x.astype(BF16)
    r = x - hi.astype(F32)
    mid = r.astype(BF16)
    lo = (r - mid.astype(F32)).astype(BF16)
    return hi, mid, lo


def _dot_x3(a, b):
    ah, al = _split2(a)
    bh, bl = _split2(b)
    return _dot(ah, bh) + (_dot(ah, bl) + _dot(al, bh))


def _dot_sel2(a, sel):
    hi, lo = _split2(a)
    return _dot(hi, sel) + _dot(lo, sel)


def _dot_sel3(a, sel):
    hi, mid, lo = _split3(a)
    return _dot(hi, sel) + (_dot(mid, sel) + _dot(lo, sel))


def _sigmoid(x):
    return 1.0 / (1.0 + jnp.exp(-x))


def _softplus(x):
    return jnp.maximum(x, 0.0) + jnp.log(1.0 + jnp.exp(-jnp.abs(x)))


def _head_ones(n_heads):
    w = n_heads * HEAD_DIM
    r = lax.broadcasted_iota(jnp.int32, (w, w), 0) // HEAD_DIM
    c = lax.broadcasted_iota(jnp.int32, (w, w), 1) // HEAD_DIM
    return (r == c).astype(BF16)


def _mod_kernel(c_ref, w_ref, b_ref, o_ref):
    c = c_ref[...]
    s = c * _sigmoid(c)
    o_ref[...] = _dot_x3(s, w_ref[...]) + b_ref[...]


def _mod(c, w_mod, b_mod):
    rows, d = c.shape
    n = w_mod.shape[1]
    tn = 768
    return pl.pallas_call(
        _mod_kernel,
        grid=(n // tn,),
        in_specs=[pl.BlockSpec((rows, d), lambda j: (0, 0)),
                  pl.BlockSpec((d, tn), lambda j: (0, j)),
                  pl.BlockSpec((1, tn), lambda j: (0, j))],
        out_specs=pl.BlockSpec((rows, tn), lambda j: (0, j)),
        out_shape=jax.ShapeDtypeStruct((rows, n), F32),
        compiler_params=_cparams("arbitrary"),
    )(c, w_mod, b_mod.reshape(1, n))


def _inproj_kernel(x_ref, sc_ref, sh_ref, g_ref, w_ref, wfz_ref, gq_ref, gk_ref, bf_ref, bfc_ref, hs_ref,
                   qh_ref, kh_ref, vh_ref, k_ref, v_ref, lf_ref, lft_ref, rw_ref):
    x = x_ref[0]
    ms = jnp.mean(x * x, axis=-1, keepdims=True)
    xn = x * lax.rsqrt(ms + RMS_EPS) * g_ref[...]
    h = (xn * (1.0 + sc_ref[0]) + sh_ref[0]).astype(BF16)
    proj = _dot(h, w_ref[...])
    hs = hs_ref[...]

    def head_norm(t, gain):
        m = _dot_sel2(t * t, hs) * (1.0 / HEAD_DIM)
        return t * lax.rsqrt(m + RMS_EPS) * gain

    q = head_norm(proj[:, 0:WIDTH], gq_ref[...])
    k = head_norm(proj[:, WIDTH:2 * WIDTH], gk_ref[...])
    v = proj[:, 2 * WIDTH:3 * WIDTH]
    k_ref[0] = k
    v_ref[0] = v
    qb = q.astype(BF16)
    kb = k.astype(BF16)
    vb = v.astype(BF16)
    for hp in range(N_HEAD_PAIRS):
        sl = slice(hp * PAIR, (hp + 1) * PAIR)
        qh_ref[0, hp] = qb[:, sl]
        kh_ref[0, hp] = kb[:, sl]
        vh_ref[0, hp] = vb[:, sl]
    rw_ref[0] = proj[:, 3 * WIDTH:3 * WIDTH + RWKV_COLS]
    fz = proj[:, FZ_OFF:FZ_OFF + N_HEADS] + bf_ref[...]
    lf_ref[0] = -_softplus(-fz)
    fzt = _dot_nt(wfz_ref[...], h) + bfc_ref[...]
    lft_ref[0] = -_softplus(-fzt)


def _inproj(x3, sc, sh, g_mix, w_r, wfz_t, gq, gk, b_forget, tm):
    bx, tx, d = x3.shape
    per_row = sc.shape[1] != 1
    mod_spec = (pl.BlockSpec((1, tm, d), lambda b, t: (b, t, 0)) if per_row
                else pl.BlockSpec((1, 1, d), lambda b, t: (b, 0, 0)))
    const = lambda shape: pl.BlockSpec(shape, lambda b, t: tuple(0 for _ in shape))
    head_spec = pl.BlockSpec((1, N_HEAD_PAIRS, tm, PAIR), lambda b, t: (b, 0, t, 0))
    row_spec = lambda w: pl.BlockSpec((1, tm, w), lambda b, t: (b, t, 0))
    head_shape = jax.ShapeDtypeStruct((bx, N_HEAD_PAIRS, tx, PAIR), BF16)
    return pl.pallas_call(
        _inproj_kernel,
        grid=(bx, tx // tm),
        in_specs=[row_spec(d), mod_spec, mod_spec, const((1, d)), const((d, IN_COLS_PADDED)),
                  const((N_HEADS, d)), const((1, WIDTH)), const((1, WIDTH)), const((1, N_HEADS)),
                  const((N_HEADS, 1)), const((WIDTH, WIDTH))],
        out_specs=[head_spec, head_spec, head_spec, row_spec(WIDTH), row_spec(WIDTH), row_spec(N_HEADS),
                   pl.BlockSpec((1, N_HEADS, tm), lambda b, t: (b, 0, t)), row_spec(RWKV_COLS)],
        out_shape=[head_shape, head_shape, head_shape,
                   jax.ShapeDtypeStruct((bx, tx, WIDTH), F32), jax.ShapeDtypeStruct((bx, tx, WIDTH), F32),
                   jax.ShapeDtypeStruct((bx, tx, N_HEADS), F32), jax.ShapeDtypeStruct((bx, N_HEADS, tx), F32),
                   jax.ShapeDtypeStruct((bx, tx, RWKV_COLS), F32)],
        compiler_params=_cparams("parallel", "arbitrary"),
    )(x3, sc, sh, g_mix, w_r, wfz_t, gq, gk, b_forget.reshape(1, N_HEADS), b_forget.reshape(N_HEADS, 1),
      _head_ones(N_HEADS))


def _cumsum_kernel(l_ref, u_ref, o_ref, c_ref, *, scale):
    tc = l_ref.shape[2]

    @pl.when(pl.program_id(1) == 0)
    def _():
        c_ref[...] = jnp.zeros_like(c_ref)

    f = _dot_sel3(l_ref[0], u_ref[...]) + c_ref[:, 0:1]
    o_ref[0] = f * scale
    c_ref[...] = jnp.broadcast_to(f[:, tc - 1:tc], c_ref.shape)


def _cumsum_t(lft, tc, scale):
    b, h, t = lft.shape
    r = lax.broadcasted_iota(jnp.int32, (tc, tc), 0)
    c = lax.broadcasted_iota(jnp.int32, (tc, tc), 1)
    upper = (r <= c).astype(BF16)
    return pl.pallas_call(
        functools.partial(_cumsum_kernel, scale=scale),
        grid=(b, t // tc),
        in_specs=[pl.BlockSpec((1, h, tc), lambda i, j: (i, 0, j)), pl.BlockSpec((tc, tc), lambda i, j: (0, 0))],
        out_specs=pl.BlockSpec((1, h, tc), lambda i, j: (i, 0, j)),
        out_shape=jax.ShapeDtypeStruct((b, h, t), F32),
        scratch_shapes=[pltpu.VMEM((h, 128), F32)],
        compiler_params=_cparams("parallel", "arbitrary"),
    )(lft, upper)


def _attn_kernel(q_ref, k_ref, v_ref, f_ref, o_ref, m_ref, acc_ref, *, tq, tk):
    i = pl.program_id(2)
    q2 = q_ref[0, 0]
    lane = lax.broadcasted_iota(jnp.int32, q2.shape, 1)
    zero = jnp.zeros_like(q2)
    qs = (jnp.where(lane < HEAD_DIM, q2, zero), jnp.where(lane >= HEAD_DIM, q2, zero))
    q0 = pl.multiple_of(i * tq, tq)
    f_base = [f_ref[0, 0, hh:hh + 1, pl.ds(q0, tq)][:, 0:1] for hh in range(2)]
    m_ref[...] = jnp.full_like(m_ref, NEG)
    acc_ref[...] = jnp.zeros_like(acc_ref)

    def tile(j, masked):
        k0 = pl.multiple_of(j * tk, tk)
        k2 = k_ref[0, 0, pl.ds(k0, tk), :]
        v2 = v_ref[0, 0, pl.ds(k0, tk), :]
        vlane = lax.broadcasted_iota(jnp.int32, v2.shape, 1)
        one = jnp.ones((), v2.dtype)
        vs = (jnp.where(vlane < HEAD_DIM, v2, one), jnp.where(vlane >= HEAD_DIM, v2, one))
        for hh in range(2):
            s = _dot_nt(qs[hh], k2) + (f_base[hh] - f_ref[0, 0, hh:hh + 1, pl.ds(k0, tk)])
            if masked:
                r = lax.broadcasted_iota(jnp.int32, s.shape, 0)
                c = lax.broadcasted_iota(jnp.int32, s.shape, 1)
                s = jnp.where(c + k0 <= r + q0, s, NEG)
            m_old = m_ref[hh]
            m_new = jnp.maximum(m_old, jnp.max(s, axis=1, keepdims=True))
            p = jnp.exp2(s - jnp.concatenate([m_new] * (tk // 128), axis=1))
            alpha = jnp.exp2(m_old - m_new)
            acc_ref[hh] = alpha * acc_ref[hh] + _dot(p.astype(BF16), vs[hh])
            m_ref[hh] = m_new

    def body(j, carry):
        tile(j, False)
        return carry

    n_sub = tq // tk
    lax.fori_loop(0, i * n_sub, body, 0)
    for dd in range(n_sub):
        tile(i * n_sub + dd, True)
    acc_a = acc_ref[0]
    acc_b = acc_ref[1]
    o_a = acc_a / pltpu.roll(acc_a, HEAD_DIM, 1)
    o_b = acc_b / pltpu.roll(acc_b, HEAD_DIM, 1)
    o_ref[0, 0] = jnp.where(lane < HEAD_DIM, o_a, o_b).astype(o_ref.dtype)


def _fox_prompt(qh, kh, vh, ft, tq, tk):
    b, hp, t, w = qh.shape
    assert tq % tk == 0 and tk % 128 == 0
    ft4 = ft.reshape(b, hp, 2, t)
    seq_spec = pl.BlockSpec((1, 1, t, w), lambda bi, hi, i: (bi, hi, 0, 0))
    tile_spec = pl.BlockSpec((1, 1, tq, w), lambda bi, hi, i: (bi, hi, i, 0))
    return pl.pallas_call(
        functools.partial(_attn_kernel, tq=tq, tk=tk),
        grid=(b, hp, t // tq),
        in_specs=[tile_spec, seq_spec, seq_spec, pl.BlockSpec((1, 1, 2, t), lambda bi, hi, i: (bi, hi, 0, 0))],
        out_specs=tile_spec,
        out_shape=jax.ShapeDtypeStruct((b, hp, t, w), BF16),
        scratch_shapes=[pltpu.VMEM((2, tq, 128), F32), pltpu.VMEM((2, tq, w), F32)],
        compiler_params=_cparams("parallel", "parallel", "arbitrary"),
    )(qh, kh, vh, ft4)


PAGES_PER_STEP = 8
QROWS = 4 * N_HEADS


def _decode_kernel(pt_ref, q_ref, kn_ref, vn_ref, lfn_ref, cnew_ref, ustrict_ref, fold_ref, *refs, n_steps):
    del pt_ref
    npg = PAGES_PER_STEP
    k_refs = refs[0:npg]
    v_refs = refs[npg:2 * npg]
    lf_refs = refs[2 * npg:3 * npg]
    o_ref, m_ref, l_ref, acc_ref, accn_ref, carry_ref = refs[3 * npg:]
    j = pl.program_id(1)
    q = q_ref[0]
    r = lax.broadcasted_iota(jnp.int32, (QROWS, WIDTH), 0)
    c = lax.broadcasted_iota(jnp.int32, (QROWS, WIDTH), 1)
    own_head = (r % N_HEADS) == (c // HEAD_DIM)
    q_bd = jnp.where(own_head, jnp.concatenate([q] * N_HEADS, axis=1), jnp.zeros((), q.dtype))

    def softmax_step(s):
        m_old = m_ref[...]
        m_new = jnp.maximum(m_old, jnp.max(s, axis=1, keepdims=True))
        p = jnp.exp(s - m_new)
        alpha = jnp.exp(m_old - m_new)
        l_ref[...] = alpha * l_ref[...] + jnp.sum(p, axis=1, keepdims=True)
        m_ref[...] = m_new
        return p.astype(BF16), alpha

    @pl.when(j == 0)
    def _():
        m_ref[...] = jnp.full_like(m_ref, NEG)
        l_ref[...] = jnp.zeros_like(l_ref)
        acc_ref[...] = jnp.zeros_like(acc_ref)
        carry_ref[...] = jnp.zeros_like(carry_ref)
        s = _dot_nt(q, kn_ref[0])
        bias = _dot_sel3(lfn_ref[0], cnew_ref[...])[0:1, :]
        rr = lax.broadcasted_iota(jnp.int32, s.shape, 0)
        cc = lax.broadcasted_iota(jnp.int32, s.shape, 1)
        ok = ((rr % N_HEADS) == (cc % N_HEADS)) & ((cc // N_HEADS) <= (rr // N_HEADS))
        p, _ = softmax_step(jnp.where(ok, s + bias, NEG))
        accn_ref[...] = _dot(p, vn_ref[0])

    ustrict = ustrict_ref[...]
    carry = carry_ref[...]
    scores = []
    for u in range(npg):
        kt = k_refs[u][...].reshape(WIDTH, PAGE).astype(BF16)
        lft = lf_refs[u][...]
        a = jnp.broadcast_to(lft[None], (4, N_HEADS, PAGE)).reshape(QROWS, PAGE)
        hi, mid, lo = _split3(a)
        bx = _dot(jnp.concatenate([hi, mid, lo], axis=0), ustrict)
        bias = bx[0:QROWS] + (bx[QROWS:2 * QROWS] + bx[2 * QROWS:3 * QROWS]) + carry
        carry = carry + jnp.sum(a, axis=1, keepdims=True)
        scores.append(_dot(q_bd, kt) + bias)
    carry_ref[...] = carry
    p, alpha = softmax_step(jnp.concatenate(scores, axis=1))
    acc = alpha * acc_ref[...]
    for u in range(npg):
        vt = v_refs[u][...].reshape(WIDTH, PAGE).astype(BF16)
        acc = acc + _dot_nt(p[:, u * PAGE:(u + 1) * PAGE], vt)
    acc_ref[...] = acc
    accn_ref[...] = alpha * accn_ref[...]

    @pl.when(j == n_steps - 1)
    def _():
        own = jnp.where(own_head, acc_ref[...], 0.0)
        o_ref[0] = (_dot_sel3(own, fold_ref[...]) + accn_ref[...]) / l_ref[...]


def _fox_decode(page_table, q, k_new, v_new, lf_new, cache_kt, cache_vt, cache_lft):
    nb, n_pages = page_table.shape
    npg = PAGES_PER_STEP
    n_steps = n_pages // npg
    ustrict = (lax.broadcasted_iota(jnp.int32, (PAGE, PAGE), 0)
               > lax.broadcasted_iota(jnp.int32, (PAGE, PAGE), 1)).astype(BF16)
    ri = lax.broadcasted_iota(jnp.int32, (QROWS, QROWS), 0)
    ci = lax.broadcasted_iota(jnp.int32, (QROWS, QROWS), 1)
    cnew = -(((ri % N_HEADS) == (ci % N_HEADS)) & ((ri // N_HEADS) <= (ci // N_HEADS))).astype(BF16)
    fold = (lax.broadcasted_iota(jnp.int32, (WIDTH, HEAD_DIM), 0) % HEAD_DIM
            == lax.broadcasted_iota(jnp.int32, (WIDTH, HEAD_DIM), 1)).astype(BF16)

    def page_idx(u):
        return lambda b, j, pt: (pt[b, n_pages - 1 - (j * npg + u)], 0, 0, 0)

    def page_idx3(u):
        return lambda b, j, pt: (pt[b, n_pages - 1 - (j * npg + u)], 0, 0)

    per_b = lambda shape: pl.BlockSpec((1,) + shape, lambda b, j, pt: (b, 0, 0))
    const = lambda shape: pl.BlockSpec(shape, lambda b, j, pt: (0, 0))
    in_specs = ([per_b((QROWS, HEAD_DIM)), per_b((QROWS, HEAD_DIM)), per_b((QROWS, HEAD_DIM)),
                 per_b((8, QROWS)), const((QROWS, QROWS)), const((PAGE, PAGE)), const((WIDTH, HEAD_DIM))]
                + [pl.BlockSpec((None, N_HEADS, HEAD_DIM, PAGE), page_idx(u)) for u in range(npg)]
                + [pl.BlockSpec((None, N_HEADS, HEAD_DIM, PAGE), page_idx(u)) for u in range(npg)]
                + [pl.BlockSpec((None, N_HEADS, PAGE), page_idx3(u)) for u in range(npg)])
    grid_spec = pltpu.PrefetchScalarGridSpec(
        num_scalar_prefetch=1,
        grid=(nb, n_steps),
        in_specs=in_specs,
        out_specs=pl.BlockSpec((1, QROWS, HEAD_DIM), lambda b, j, pt: (b, 0, 0)),
        scratch_shapes=[pltpu.VMEM((QROWS, 1), F32), pltpu.VMEM((QROWS, 1), F32),
                        pltpu.VMEM((QROWS, WIDTH), F32), pltpu.VMEM((QROWS, HEAD_DIM), F32),
                        pltpu.VMEM((QROWS, 1), F32)],
    )
    lfn = jnp.broadcast_to(lf_new.reshape(nb, 1, QROWS), (nb, 8, QROWS))
    return pl.pallas_call(
        functools.partial(_decode_kernel, n_steps=n_steps),
        grid_spec=grid_spec,
        out_shape=jax.ShapeDtypeStruct((nb, QROWS, HEAD_DIM), F32),
        compiler_params=_cparams("parallel", "arbitrary"),
    )(page_table, q, k_new, v_new, lfn, cnew, ustrict, fold,
      *([cache_kt] * npg), *([cache_vt] * npg), *([cache_lft] * npg))


def _rwkv_prep_kernel(p_ref, prev_ref, first_ref, mu_ref, w0_ref, w2_ref, a0_ref, a2_ref, g2_ref, kk_ref, ka_ref,
                      rk_ref, hs_ref, r_ref, k_ref, v_ref, w_ref, a_ref, b_ref, g_ref, bonus_ref, *, shift_rows):
    p = p_ref[0]
    if shift_rows:
        carry = jnp.where(pl.program_id(1) == 0, first_ref[0], prev_ref[0, 7:8, :])
        row0 = lax.broadcasted_iota(jnp.int32, p.shape, 0) == 0
        prev = jnp.where(row0, carry, pltpu.roll(p, 1, 0))
    else:
        prev = prev_ref[0]
    z = p + (prev - p) * mu_ref[...]
    r = z[:, 0:WIDTH]
    k = z[:, WIDTH:2 * WIDTH]
    v = z[:, 2 * WIDTH:3 * WIDTH]
    lora = z[:, 3 * WIDTH:3 * WIDTH + DECAY_LORA + ICLR_LORA]
    gd = z[:, 3 * WIDTH + DECAY_LORA + ICLR_LORA:RWKV_COLS]
    hs = hs_ref[...]
    w = -_softplus(-(w0_ref[...] + _dot_x3(jnp.tanh(lora), w2_ref[...]))) - 0.5
    decay = jnp.exp(-jnp.exp(w))
    a = _sigmoid(a0_ref[...] + _dot_x3(lora, a2_ref[...]))
    g = _dot_x3(_sigmoid(gd), g2_ref[...])
    kk = k * kk_ref[...]
    norm = jnp.sqrt(_dot_sel3(kk * kk, hs))
    kk = kk / jnp.maximum(norm, 1e-12)
    k = k * (1.0 + (a - 1.0) * ka_ref[...])
    bonus = _dot_sel3(r * k * rk_ref[...], hs) * v
    r_ref[0] = r
    k_ref[0] = k
    v_ref[0] = v
    w_ref[0] = decay
    a_ref[0] = -kk
    b_ref[0] = kk * a
    g_ref[0] = g
    bonus_ref[0] = bonus


def _rwkv_prep(p_rw, prev, first, mu, w0, w2p, a0, a2p, g2, kk, ka, rk, tm):
    bx, tx, _ = p_rw.shape
    row = lambda w: pl.BlockSpec((1, tm, w), lambda b, t: (b, t, 0))
    const = lambda shape: pl.BlockSpec(shape, lambda b, t: tuple(0 for _ in shape))
    out = jax.ShapeDtypeStruct((bx, tx, WIDTH), F32)
    shift_rows = prev is None
    if shift_rows:
        prev = p_rw
        prev_spec = pl.BlockSpec((1, 8, RWKV_COLS), lambda b, t: (b, jnp.maximum(t * (tm // 8) - 1, 0), 0))
    else:
        prev_spec = row(RWKV_COLS)
    return pl.pallas_call(
        functools.partial(_rwkv_prep_kernel, shift_rows=shift_rows),
        grid=(bx, tx // tm),
        in_specs=[row(RWKV_COLS), prev_spec, pl.BlockSpec((1, 1, RWKV_COLS), lambda b, t: (b, 0, 0)),
                  const((1, RWKV_COLS)), const((1, WIDTH)),
                  const((DECAY_LORA + ICLR_LORA, WIDTH)), const((1, WIDTH)), const((DECAY_LORA + ICLR_LORA, WIDTH)),
                  const((GATE_LORA, WIDTH)), const((1, WIDTH)), const((1, WIDTH)), const((1, WIDTH)),
                  const((WIDTH, WIDTH))],
        out_specs=[row(WIDTH)] * 8,
        out_shape=[out] * 8,
        compiler_params=_cparams("parallel", "arbitrary"),
    )(p_rw, prev, first, mu, w0, w2p, a0, a2p, g2, kk, ka, rk, _head_ones(N_HEADS))


WKV_NB = 2
WKV_HG = 4
WKV_LANES = WKV_HG * HEAD_DIM
WKV_GROUPS = N_HEADS // WKV_HG
WKV_ROWS = WKV_NB * WKV_GROUPS * HEAD_DIM


def _wkv_kernel(r_ref, k_ref, v_ref, w_ref, a_ref, b_ref, s0_ref, ones_ref, y_ref, sT_ref, st_ref, *, tc, n_t):
    ti = pl.program_id(1)
    blocks = [(bi, gg) for bi in range(WKV_NB) for gg in range(WKV_GROUPS)]

    @pl.when(ti == 0)
    def _():
        for n, (bi, gg) in enumerate(blocks):
            st_ref[n * HEAD_DIM:(n + 1) * HEAD_DIM, :] = jnp.concatenate(
                [s0_ref[bi, gg * WKV_HG + q] for q in range(WKV_HG)], axis=1)

    rows = lax.broadcasted_iota(jnp.int32, (WKV_ROWS, WKV_LANES), 0)
    lanes = lax.broadcasted_iota(jnp.int32, (WKV_ROWS, WKV_LANES), 1)
    eye = (rows % HEAD_DIM) == (lanes % HEAD_DIM)
    eye_f = eye.astype(F32)
    ones = ones_ref[...]
    grp = min(tc, 8)

    def bcast(xs, u):
        return jnp.concatenate(
            [jnp.broadcast_to(xs[bi][u:u + 1, gg * WKV_LANES:(gg + 1) * WKV_LANES], (HEAD_DIM, WKV_LANES))
             for bi, gg in blocks], axis=0)

    def group(gi, carry):
        t0 = pl.multiple_of(gi * grp, grp)
        load = lambda ref: [ref[bi, pl.ds(t0, grp), :] for bi in range(WKV_NB)]
        r8, k8, v8, w8, a8, b8 = (load(ref) for ref in (r_ref, k_ref, v_ref, w_ref, a_ref, b_ref))
        vdiag = lambda u: jnp.where(eye, bcast(v8, u), 0.0).astype(BF16)
        s = st_ref[...]
        ys = []

        def y_rows(yb):
            yb = yb * eye_f
            return [jnp.sum(yb[n * HEAD_DIM:(n + 1) * HEAD_DIM], axis=0, keepdims=True) for n in range(len(blocks))]

        vcol = _dot(vdiag(0), ones)
        ybf = None
        for u in range(grp):
            sa = _dot((s * bcast(a8, u)).astype(BF16), ones)
            side = [vdiag(u + 1)] if u + 1 < grp else []
            if ybf is not None:
                side.append(ybf)
            misc = _dot(jnp.concatenate(side, axis=0), ones) if side else None
            s = s * bcast(w8, u) + sa * bcast(b8, u) + vcol * bcast(k8, u)
            if misc is not None:
                off = 0
                if u + 1 < grp:
                    vcol = misc[0:WKV_ROWS]
                    off = WKV_ROWS
                if ybf is not None:
                    ys.append(y_rows(misc[off:off + WKV_ROWS]))
            ybf = (s * bcast(r8, u)).astype(BF16)
        ys.append(y_rows(_dot(ybf, ones)))
        st_ref[...] = s
        for bi in range(WKV_NB):
            y_ref[bi, pl.ds(t0, grp), :] = jnp.concatenate(
                [jnp.concatenate([ys[u][bi * WKV_GROUPS + gg] for gg in range(WKV_GROUPS)], axis=1)
                 for u in range(grp)], axis=0)
        return carry

    lax.fori_loop(0, tc // grp, group, 0)

    @pl.when(ti == n_t - 1)
    def _():
        for n, (bi, gg) in enumerate(blocks):
            for q in range(WKV_HG):
                sT_ref[bi, gg * WKV_HG + q] = st_ref[n * HEAD_DIM:(n + 1) * HEAD_DIM, q * HEAD_DIM:(q + 1) * HEAD_DIM]


def _wkv_scan(r, k, v, w, a, b, s0, tc):
    bx, tx, _ = r.shape
    assert bx % WKV_NB == 0
    n_t = tx // tc
    seq = pl.BlockSpec((WKV_NB, tc, WIDTH), lambda i, t: (i, t, 0))
    st = pl.BlockSpec((WKV_NB, N_HEADS, HEAD_DIM, HEAD_DIM), lambda i, t: (i, 0, 0, 0))
    ri = lax.broadcasted_iota(jnp.int32, (WKV_LANES, WKV_LANES), 0) // HEAD_DIM
    ci = lax.broadcasted_iota(jnp.int32, (WKV_LANES, WKV_LANES), 1) // HEAD_DIM
    ones = (ri == ci).astype(BF16)
    return pl.pallas_call(
        functools.partial(_wkv_kernel, tc=tc, n_t=n_t),
        grid=(bx // WKV_NB, n_t),
        in_specs=[seq] * 6 + [st, pl.BlockSpec((WKV_LANES, WKV_LANES), lambda i, t: (0, 0))],
        out_specs=[seq, st],
        out_shape=[jax.ShapeDtypeStruct((bx, tx, WIDTH), F32),
                   jax.ShapeDtypeStruct((bx, N_HEADS, HEAD_DIM, HEAD_DIM), F32)],
        scratch_shapes=[pltpu.VMEM((WKV_ROWS, WKV_LANES), F32)],
        compiler_params=_cparams("parallel", "arbitrary"),
    )(r, k, v, w, a, b, s0, ones)


def _outproj_kernel(x_ref, of_ref, y_ref, bonus_ref, g_ref, gt_ref, sc_ref, sh_ref, lw_ref, lb_ref, gf_ref,
                    wo_ref, wr_ref, br_ref, hs_ref, lt_ref, x1_ref, h2_ref, ti_ref, tw_ref, rk_ref, cnt_ref, run_ref):
    @pl.when(pl.program_id(1) == 0)
    def _():
        run_ref[...] = jnp.zeros_like(run_ref)

    hs = hs_ref[...]
    y = y_ref[0]
    mu = _dot_sel2(y, hs) * (1.0 / HEAD_DIM)
    yc = y - mu
    var = _dot_sel2(yc * yc, hs) * (1.0 / HEAD_DIM)
    o_rw = ((yc * lax.rsqrt(var + GN_EPS)) * lw_ref[...] + lb_ref[...] + bonus_ref[0]) * g_ref[0]
    mixed = _dot(o_rw.astype(BF16), wo_ref[WIDTH:2 * WIDTH, :])
    for hp in range(N_HEAD_PAIRS):
        mixed = mixed + _dot(of_ref[0, hp], wo_ref[hp * PAIR:(hp + 1) * PAIR, :])
    x1 = x_ref[0] + gt_ref[0] * mixed
    x1_ref[0] = x1
    ms = jnp.mean(x1 * x1, axis=-1, keepdims=True)
    h2 = (x1 * lax.rsqrt(ms + RMS_EPS) * gf_ref[...]) * (1.0 + sc_ref[0]) + sh_ref[0]
    h2_ref[0] = h2.astype(BF16)
    logits = _dot_x3(h2, wr_ref[...]) + br_ref[...]
    lane = lax.broadcasted_iota(jnp.int32, logits.shape, 1).astype(F32)
    vals, idxs = [], []
    for _ in range(TOP_K):
        m = jnp.max(logits, axis=1, keepdims=True)
        idx = jnp.min(jnp.where(logits == m, lane, float(N_EXPERTS)), axis=1, keepdims=True)
        vals.append(m)
        idxs.append(idx)
        logits = jnp.where(lane == idx, -jnp.inf, logits)
    e = [jnp.exp(vv - vals[0]) for vv in vals]
    tot = e[0] + e[1] + e[2] + e[3]
    tw_ref[0] = jnp.concatenate(e, axis=1) / tot
    ti_ref[0] = jnp.concatenate(idxs, axis=1).astype(jnp.int32)
    chosen = [lane == idx for idx in idxs]
    multi_hot = jnp.where(chosen[0] | chosen[1] | chosen[2] | chosen[3], 1.0, 0.0)
    before = _dot(lt_ref[...], multi_hot.astype(BF16)) + run_ref[...]
    ranks = [jnp.sum(jnp.where(c, before, 0.0), axis=1, keepdims=True) for c in chosen]
    rk_ref[0] = jnp.concatenate(ranks, axis=1).astype(jnp.int32)
    run_ref[...] = run_ref[...] + jnp.sum(multi_hot, axis=0, keepdims=True)
    cnt_ref[0] = run_ref[...]


def _outproj(x3, of, y, bonus, g, gt, sc, sh, lnx_w, lnx_b, g_ffn, w_out_b, w_router, b_router, tm):
    bx, tx, d = x3.shape
    per_row = gt.shape[1] != 1
    mod_spec = (pl.BlockSpec((1, tm, d), lambda b, t: (b, t, 0)) if per_row
                else pl.BlockSpec((1, 1, d), lambda b, t: (b, 0, 0)))
    row = lambda w: pl.BlockSpec((1, tm, w), lambda b, t: (b, t, 0))
    const = lambda shape: pl.BlockSpec(shape, lambda b, t: tuple(0 for _ in shape))
    lower = (lax.broadcasted_iota(jnp.int32, (tm, tm), 0) > lax.broadcasted_iota(jnp.int32, (tm, tm), 1)).astype(BF16)
    return pl.pallas_call(
        _outproj_kernel,
        grid=(bx, tx // tm),
        in_specs=[row(d), pl.BlockSpec((1, N_HEAD_PAIRS, tm, PAIR), lambda b, t: (b, 0, t, 0)),
                  row(WIDTH), row(WIDTH), row(WIDTH), mod_spec, mod_spec, mod_spec,
                  const((1, WIDTH)), const((1, WIDTH)), const((1, d)), const((2 * WIDTH, d)),
                  const((d, N_EXPERTS)), const((1, N_EXPERTS)), const((WIDTH, WIDTH)), const((tm, tm))],
        out_specs=[row(d), row(d), row(TOP_K), row(TOP_K), row(TOP_K),
                   pl.BlockSpec((1, 1, N_EXPERTS), lambda b, t: (b, 0, 0))],
        out_shape=[jax.ShapeDtypeStruct((bx, tx, d), F32), jax.ShapeDtypeStruct((bx, tx, d), BF16),
                   jax.ShapeDtypeStruct((bx, tx, TOP_K), jnp.int32), jax.ShapeDtypeStruct((bx, tx, TOP_K), F32),
                   jax.ShapeDtypeStruct((bx, tx, TOP_K), jnp.int32), jax.ShapeDtypeStruct((bx, 1, N_EXPERTS), F32)],
        scratch_shapes=[pltpu.VMEM((1, N_EXPERTS), F32)],
        compiler_params=_cparams("parallel", "arbitrary"),
    )(x3, of, y, bonus, g, gt, sc, sh, lnx_w, lnx_b, g_ffn, w_out_b, w_router, b_router, _head_ones(N_HEADS), lower)


MOE_TM = 512


def _moe_kernel(te_ref, tf_ref, x_ref, wg_ref, bg_ref, wu_ref, bu_ref, wd_ref, bd_ref, o_ref, wb_ref):
    del te_ref
    i = pl.program_id(0)

    @pl.when(tf_ref[i] == 2)
    def _():
        wb_ref[0] = wg_ref[0].astype(BF16)
        wb_ref[1] = wu_ref[0].astype(BF16)
        wb_ref[2] = wd_ref[0].astype(BF16)

    @pl.when(tf_ref[i] != 0)
    def _():
        x = x_ref[...]
        glu = jnp.minimum(_dot(x, wb_ref[0]) + bg_ref[0], SWIGLU_LIMIT)
        lin = jnp.clip(_dot(x, wb_ref[1]) + bu_ref[0], -SWIGLU_LIMIT, SWIGLU_LIMIT)
        act = glu * _sigmoid(SWIGLU_ALPHA * glu) * (lin + 1.0)
        o_ref[...] = (_dot(act.astype(BF16), wb_ref[2]) + bd_ref[0]).astype(o_ref.dtype)

    @pl.when(tf_ref[i] == 0)
    def _():
        o_ref[...] = jnp.zeros_like(o_ref)


def _moe_gemm(tile_expert, tile_flag, xg, wg, bg, wu, bu, wd, bd):
    n_slots, d = xg.shape
    n_tiles = n_slots // MOE_TM
    wspec = pl.BlockSpec((1, d, d), lambda i, te, tf: (te[i], 0, 0))
    bspec = pl.BlockSpec((1, 1, d), lambda i, te, tf: (te[i], 0, 0))
    grid_spec = pltpu.PrefetchScalarGridSpec(
        num_scalar_prefetch=2,
        grid=(n_tiles,),
        in_specs=[pl.BlockSpec((MOE_TM, d), lambda i, te, tf: (i, 0)), wspec, bspec, wspec, bspec, wspec, bspec],
        out_specs=pl.BlockSpec((MOE_TM, d), lambda i, te, tf: (i, 0)),
        scratch_shapes=[pltpu.VMEM((3, d, d), BF16)],
    )
    return pl.pallas_call(
        _moe_kernel,
        grid_spec=grid_spec,
        out_shape=jax.ShapeDtypeStruct((n_slots, d), BF16),
        compiler_params=_cparams("arbitrary"),
    )(tile_expert, tile_flag, xg, wg, bg, wu, bu, wd, bd)


def _final_kernel(x1_ref, yg_ref, tw_ref, gt_ref, o_ref):
    tw = tw_ref[0]
    moe = tw[:, 0:1] * yg_ref[0, 0].astype(F32)
    for kk in range(1, TOP_K):
        moe = moe + tw[:, kk:kk + 1] * yg_ref[kk, 0].astype(F32)
    o_ref[0] = x1_ref[0] + gt_ref[0] * moe


def _final(x1, yg, tw, gt, tm):
    bx, tx, d = x1.shape
    per_row = gt.shape[1] != 1
    mod_spec = (pl.BlockSpec((1, tm, d), lambda b, t: (b, t, 0)) if per_row
                else pl.BlockSpec((1, 1, d), lambda b, t: (b, 0, 0)))
    row = pl.BlockSpec((1, tm, d), lambda b, t: (b, t, 0))
    return pl.pallas_call(
        _final_kernel,
        grid=(bx, tx // tm),
        in_specs=[row, pl.BlockSpec((TOP_K, 1, tm, d), lambda b, t: (0, b, t, 0)),
                  pl.BlockSpec((1, tm, TOP_K), lambda b, t: (b, t, 0)), mod_spec],
        out_specs=row,
        out_shape=jax.ShapeDtypeStruct((bx, tx, d), F32),
        compiler_params=_cparams("parallel", "arbitrary"),
    )(x1, yg, tw, gt)


def _dispatch_plan(top_i, rank_in_seg, seg_counts, seg_tokens):
    n_assign = top_i.size
    n_tiles = -(-n_assign // MOE_TM) + N_EXPERTS
    n_slots = n_tiles * MOE_TM
    e_flat = top_i.reshape(-1)
    order = jnp.argsort(e_flat, stable=True).astype(jnp.int32)
    counts = jnp.sum(seg_counts, axis=0)
    cnt_start = jnp.cumsum(counts) - counts
    seg_base = jnp.cumsum(seg_counts, axis=0) - seg_counts
    padded = ((counts + MOE_TM - 1) // MOE_TM) * MOE_TM
    pad_end = jnp.cumsum(padded)
    pad_start = pad_end - padded
    experts = jnp.arange(N_EXPERTS, dtype=jnp.int32)
    tile_start = jnp.arange(n_tiles, dtype=jnp.int32) * MOE_TM
    tile_expert = jnp.minimum(jnp.sum((tile_start[:, None] >= pad_end[None, :]).astype(jnp.int32), axis=1),
                              N_EXPERTS - 1)
    tile_live = tile_start < pad_end[-1]
    is_first = jnp.concatenate([jnp.ones((1,), bool), tile_expert[1:] != tile_expert[:-1]])
    tile_flag = jnp.where(tile_live, jnp.where(is_first, 2, 1), 0).astype(jnp.int32)
    of_tile = lambda tab: jnp.sum(jnp.where(tile_expert[:, None] == experts[None, :], tab[None, :], 0), axis=1)
    rank = (tile_start - of_tile(pad_start))[:, None] + jnp.arange(MOE_TM, dtype=jnp.int32)[None, :]
    live = rank < of_tile(counts)[:, None]
    src = order.at[jnp.clip(of_tile(cnt_start)[:, None] + rank, 0, n_assign - 1)].get(mode="promise_in_bounds")
    filler = (tile_start[:, None] + jnp.arange(MOE_TM, dtype=jnp.int32)[None, :]) % (n_assign // TOP_K)
    token_of_slot = jnp.where(live, src // TOP_K, filler).reshape(n_slots)
    tok_base = jnp.repeat(seg_base + pad_start[None, :], jnp.asarray(seg_tokens), axis=0,
                          total_repeat_length=sum(seg_tokens))
    hit = top_i[:, :, None] == experts[None, None, :]
    slot_of_assign = jnp.sum(jnp.where(hit, tok_base[:, None, :], 0), axis=-1) + rank_in_seg
    return slot_of_assign, token_of_slot, tile_expert, tile_flag


def _pick_tile(n, pref):
    t = min(n, pref)
    assert n % t == 0, (n, t)
    return t


def kernel(x_prompt, x_sample, cache_k, cache_v, cache_logf, state_wkv, state_shift, page_table, c_prompt,
           c_sample, w_mod, b_mod, g_mix, g_ffn, w_in, q_norm, k_norm, b_forget, rw_mu, rw_w0, rw_w2, rw_a0,
           rw_a2, rw_g2, rw_kk, rw_ka, rw_rk, lnx_w, lnx_b, w_out, w_router, b_router, w_gate, b_gate, w_up,
           b_up, w_down, b_down):
    depth = w_mod.shape[0]
    assert depth == 1
    n_p, t_p, d = x_prompt.shape
    n_s, t_s, _ = x_sample.shape
    assert d == D_MODEL and t_s * N_HEADS == QROWS
    l = 0

    w_in_l = w_in[l]
    w_r = jnp.concatenate([w_in_l[:, :3 * WIDTH], w_in_l[:, FOX_COLS:], w_in_l[:, 3 * WIDTH:FOX_COLS],
                           jnp.zeros((d, 128 - N_HEADS), F32)], axis=1).astype(BF16)
    wfz_t = w_in_l[:, 3 * WIDTH:FOX_COLS].T.astype(BF16)
    gq = (jnp.tile(q_norm[l], N_HEADS) * ATTN_SCALE).reshape(1, WIDTH)
    gk = jnp.tile(k_norm[l], N_HEADS).reshape(1, WIDTH)
    zpad = jnp.zeros((DECAY_LORA, WIDTH), F32)
    w2p = jnp.concatenate([rw_w2[l], zpad], axis=0)
    a2p = jnp.concatenate([zpad, rw_a2[l]], axis=0)
    row = lambda a: a.reshape(1, -1)

    n_c = n_p + n_s
    pad_c = (-n_c) % 8
    c_all = jnp.concatenate([c_sample, c_prompt, jnp.zeros((pad_c, d), F32)], axis=0)
    mod = _mod(c_all, w_mod[l], b_mod[l])
    mod_s = mod[:n_s].reshape(n_s, 1, N_MOD, d)
    mod_p = mod[n_s:n_s + n_p].reshape(n_p, 1, N_MOD, d)
    mod_s = jnp.broadcast_to(mod_s, (n_s, t_s, N_MOD, d)).reshape(1, n_s * t_s, N_MOD, d)
    mp = [mod_p[:, :, i] for i in range(N_MOD)]
    msm = [mod_s[:, :, i] for i in range(N_MOD)]

    xs3 = x_sample.reshape(1, n_s * t_s, d)
    tm_p = _pick_tile(t_p, 256)
    tm_s = _pick_tile(n_s * t_s, 256)

    proj_args = (row(g_mix[l]), w_r, wfz_t, gq, gk, b_forget[l])
    proj_args_p = (row(g_mix[l]), w_r, wfz_t, gq * LOG2E, gk, b_forget[l])
    qh_p, kh_p, vh_p, k_p, v_p, lf_p, lft_p, rw_p = _inproj(x_prompt, mp[1], mp[0], *proj_args_p, tm_p)
    qh_s, _, _, k_s, v_s, lf_s, _, rw_s = _inproj(xs3, msm[1], msm[0], *proj_args, tm_s)

    ft = _cumsum_t(lft_p, _pick_tile(t_p, 512), LOG2E)
    tq = _pick_tile(t_p, 1024)
    of_p = _fox_prompt(qh_p, kh_p, vh_p, ft, tq, tq)

    q_s = jnp.transpose(qh_s[0], (1, 0, 2)).reshape(n_s, QROWS, HEAD_DIM)
    kn_s = k_s.reshape(n_s, QROWS, HEAD_DIM).astype(BF16)
    vn_s = v_s.reshape(n_s, QROWS, HEAD_DIM).astype(BF16)
    cache_kt = jnp.transpose(cache_k[l], (0, 2, 3, 1))
    cache_vt = jnp.transpose(cache_v[l], (0, 2, 3, 1))
    cache_lft = jnp.swapaxes(cache_logf[l], 1, 2)
    o_s = _fox_decode(page_table, q_s, kn_s, vn_s, lf_s.reshape(n_s, QROWS), cache_kt, cache_vt, cache_lft)
    of_s = jnp.transpose(o_s.reshape(n_s * t_s, N_HEAD_PAIRS, PAIR), (1, 0, 2))[None].astype(BF16)

    prep_w = (row(rw_mu[l]), row(rw_w0[l]), w2p, row(rw_a0[l]), a2p, rw_g2[l], row(rw_kk[l]), row(rw_ka[l]),
              row(rw_rk[l]))
    rw_s4 = rw_s.reshape(n_s, t_s, RWKV_COLS)
    prev_s = jnp.concatenate([state_shift[l][:, None, :], rw_s4[:, :-1]], axis=1).reshape(1, n_s * t_s, RWKV_COLS)
    r_p, kk_p, vv_p, w_p, a_p, b_p, g_p, bonus_p = _rwkv_prep(
        rw_p, None, jnp.zeros((n_p, 1, RWKV_COLS), F32), *prep_w, tm_p)
    r_s, kk_s, vv_s, w_s, a_s, b_s, g_s, bonus_s = _rwkv_prep(
        rw_s, prev_s, jnp.zeros((1, 1, RWKV_COLS), F32), *prep_w, tm_s)
    y_p, wkv_p = _wkv_scan(r_p, kk_p, vv_p, w_p, a_p, b_p,
                           jnp.zeros((n_p, N_HEADS, HEAD_DIM, HEAD_DIM), F32), _pick_tile(t_p, 256))
    to_seq = lambda a: a.reshape(n_s, t_s, WIDTH)
    y_s, wkv_s = _wkv_scan(to_seq(r_s), to_seq(kk_s), to_seq(vv_s), to_seq(w_s), to_seq(a_s), to_seq(b_s),
                           state_wkv[l], t_s)
    y_s = y_s.reshape(1, n_s * t_s, WIDTH)

    out_w = (row(lnx_w[l]), row(lnx_b[l]), row(g_ffn[l]), w_out[l].astype(BF16), w_router[l], row(b_router[l]))
    x1_p, h2_p, ti_p, tw_p, rk_p, cnt_p = _outproj(x_prompt, of_p, y_p, bonus_p, g_p, mp[2], mp[4], mp[3],
                                                   *out_w, tm_p)
    x1_s, h2_s, ti_s, tw_s, rk_s, cnt_s = _outproj(xs3, of_s, y_s, bonus_s, g_s, msm[2], msm[4], msm[3],
                                                   *out_w, tm_s)

    experts_w = (w_gate[l], b_gate[l].reshape(N_EXPERTS, 1, d), w_up[l], b_up[l].reshape(N_EXPERTS, 1, d),
                 w_down[l], b_down[l].reshape(N_EXPERTS, 1, d))
    take_rows = lambda a, idx: a.at[idx].get(mode="promise_in_bounds")

    def moe(h2, top_i, rank, counts, seg_tokens):
        n_tok = sum(seg_tokens)
        slot_of_assign, token_of_slot, tile_expert, tile_flag = _dispatch_plan(
            top_i.reshape(n_tok, TOP_K), rank.reshape(n_tok, TOP_K),
            counts.reshape(len(seg_tokens), N_EXPERTS).astype(jnp.int32), seg_tokens)
        ys = _moe_gemm(tile_expert, tile_flag, take_rows(h2.reshape(n_tok, d), token_of_slot), *experts_w)
        return take_rows(ys, slot_of_assign.T)

    yg_p = moe(h2_p, ti_p, rk_p, cnt_p, [t_p] * n_p)
    yg_s = moe(h2_s, ti_s, rk_s, cnt_s, [n_s * t_s])
    y_prompt = _final(x1_p, yg_p.reshape(TOP_K, n_p, t_p, d), tw_p, mp[5], tm_p)
    y_sample = _final(x1_s, yg_s.reshape(TOP_K, 1, n_s * t_s, d), tw_s, msm[5], tm_s)

    heads5 = lambda a, nb, nt: a.reshape(1, nb, nt, N_HEADS, HEAD_DIM)
    return (y_prompt, y_sample.reshape(n_s, t_s, d),
            heads5(k_p, n_p, t_p), heads5(v_p, n_p, t_p), lf_p[None], wkv_p[None], rw_p[:, -1][None],
            heads5(k_s, n_s, t_s), heads5(v_s, n_s, t_s), lf_s.reshape(1, n_s, t_s, N_HEADS), wkv_s[None],
            rw_s4[:, -1][None])
```

```python
import functools

import jax
import jax.numpy as jnp
from jax import lax
from jax.experimental import pallas as pl
from jax.experimental.pallas import tpu as pltpu

F32 = jnp.float32
BF16 = jnp.bfloat16

D_MODEL = 1024
HEAD_DIM = 64
N_HEADS = 8
WIDTH = N_HEADS * HEAD_DIM
PAIR = 2 * HEAD_DIM
N_HEAD_PAIRS = N_HEADS // 2
DECAY_LORA = 64
ICLR_LORA = 64
GATE_LORA = 128
FOX_COLS = 3 * WIDTH + N_HEADS
RWKV_COLS = 3 * WIDTH + DECAY_LORA + ICLR_LORA + GATE_LORA
IN_COLS_PADDED = 3 * WIDTH + RWKV_COLS + 128
FZ_OFF = 3 * WIDTH + RWKV_COLS
ATTN_SCALE = HEAD_DIM ** -0.5
N_EXPERTS = 32
TOP_K = 4
SWIGLU_ALPHA = 1.702
SWIGLU_LIMIT = 7.0
RMS_EPS = 1e-6
GN_EPS = 64e-5
N_MOD = 6
PAGE = 128
NEG = -1e30
LOG2E = 1.4426950408889634

VMEM_LIMIT = 56 * 1024 * 1024


def _cparams(*sem):
    return pltpu.CompilerParams(dimension_semantics=sem, vmem_limit_bytes=VMEM_LIMIT)


def _dot(a, b):
    return jnp.dot(a, b, preferred_element_type=F32)


def _dot_nt(a, b):
    return lax.dot_general(a, b, (((1,), (1,)), ((), ())), preferred_element_type=F32)


def _split2(x):
    hi = x.astype(BF16)
    lo = (x - hi.astype(F32)).astype(BF16)
    return hi, lo


def _split3(x):
    hi = x.astype(BF16)
    r = x - hi.astype(F32)
    mid = r.astype(BF16)
    lo = (r - mid.astype(F32)).astype(BF16)
    return hi, mid, lo


def _dot_x3(a, b):
    ah, al = _split2(a)
    bh, bl = _split2(b)
    return _dot(ah, bh) + (_dot(ah, bl) + _dot(al, bh))


def _dot_sel2(a, sel):
    hi, lo = _split2(a)
    return _dot(hi, sel) + _dot(lo, sel)


def _dot_sel3(a, sel):
    hi, mid, lo = _split3(a)
    return _dot(hi, sel) + (_dot(mid, sel) + _dot(lo, sel))


def _sigmoid(x):
    return 1.0 / (1.0 + jnp.exp(-x))


def _softplus(x):
    return jnp.maximum(x, 0.0) + jnp.log(1.0 + jnp.exp(-jnp.abs(x)))


def _head_ones(n_heads):
    w = n_heads * HEAD_DIM
    r = lax.broadcasted_iota(jnp.int32, (w, w), 0) // HEAD_DIM
    c = lax.broadcasted_iota(jnp.int32, (w, w), 1) // HEAD_DIM
    return (r == c).astype(BF16)


def _mod_kernel(c_ref, w_ref, b_ref, o_ref):
    c = c_ref[...]
    s = c * _sigmoid(c)
    o_ref[...] = _dot_x3(s, w_ref[...]) + b_ref[...]


def _mod(c, w_mod, b_mod):
    rows, d = c.shape
    n = w_mod.shape[1]
    tn = 768
    return pl.pallas_call(
        _mod_kernel,
        grid=(n // tn,),
        in_specs=[pl.BlockSpec((rows, d), lambda j: (0, 0)),
                  pl.BlockSpec((d, tn), lambda j: (0, j)),
                  pl.BlockSpec((1, tn), lambda j: (0, j))],
        out_specs=pl.BlockSpec((rows, tn), lambda j: (0, j)),
        out_shape=jax.ShapeDtypeStruct((rows, n), F32),
        compiler_params=_cparams("arbitrary"),
    )(c, w_mod, b_mod.reshape(1, n))


def _inproj_kernel(x_ref, sc_ref, sh_ref, g_ref, w_ref, wfz_ref, gq_ref, gk_ref, bf_ref, bfc_ref, hs_ref,
                   qh_ref, kh_ref, vh_ref, k_ref, v_ref, lf_ref, lft_ref, rw_ref):
    x = x_ref[0]
    ms = jnp.mean(x * x, axis=-1, keepdims=True)
    xn = x * lax.rsqrt(ms + RMS_EPS) * g_ref[...]
    h = (xn * (1.0 + sc_ref[0]) + sh_ref[0]).astype(BF16)
    proj = _dot(h, w_ref[...])
    hs = hs_ref[...]

    def head_norm(t, gain):
        m = _dot_sel2(t * t, hs) * (1.0 / HEAD_DIM)
        return t * lax.rsqrt(m + RMS_EPS) * gain

    q = head_norm(proj[:, 0:WIDTH], gq_ref[...])
    k = head_norm(proj[:, WIDTH:2 * WIDTH], gk_ref[...])
    v = proj[:, 2 * WIDTH:3 * WIDTH]
    k_ref[0] = k
    v_ref[0] = v
    qb = q.astype(BF16)
    kb = k.astype(BF16)
    vb = v.astype(BF16)
    for hp in range(N_HEAD_PAIRS):
        sl = slice(hp * PAIR, (hp + 1) * PAIR)
        qh_ref[0, hp] = qb[:, sl]
        kh_ref[0, hp] = kb[:, sl]
        vh_ref[0, hp] = vb[:, sl]
    rw_ref[0] = proj[:, 3 * WIDTH:3 * WIDTH + RWKV_COLS]
    fz = proj[:, FZ_OFF:FZ_OFF + N_HEADS] + bf_ref[...]
    lf_ref[0] = -_softplus(-fz)
    fzt = _dot_nt(wfz_ref[...], h) + bfc_ref[...]
    lft_ref[0] = -_softplus(-fzt)


def _inproj(x3, sc, sh, g_mix, w_r, wfz_t, gq, gk, b_forget, tm):
    bx, tx, d = x3.shape
    per_row = sc.shape[1] != 1
    mod_spec = (pl.BlockSpec((1, tm, d), lambda b, t: (b, t, 0)) if per_row
                else pl.BlockSpec((1, 1, d), lambda b, t: (b, 0, 0)))
    const = lambda shape: pl.BlockSpec(shape, lambda b, t: tuple(0 for _ in shape))
    head_spec = pl.BlockSpec((1, N_HEAD_PAIRS, tm, PAIR), lambda b, t: (b, 0, t, 0))
    row_spec = lambda w: pl.BlockSpec((1, tm, w), lambda b, t: (b, t, 0))
    head_shape = jax.ShapeDtypeStruct((bx, N_HEAD_PAIRS, tx, PAIR), BF16)
    return pl.pallas_call(
        _inproj_kernel,
        grid=(bx, tx // tm),
        in_specs=[row_spec(d), mod_spec, mod_spec, const((1, d)), const((d, IN_COLS_PADDED)),
                  const((N_HEADS, d)), const((1, WIDTH)), const((1, WIDTH)), const((1, N_HEADS)),
                  const((N_HEADS, 1)), const((WIDTH, WIDTH))],
        out_specs=[head_spec, head_spec, head_spec, row_spec(WIDTH), row_spec(WIDTH), row_spec(N_HEADS),
                   pl.BlockSpec((1, N_HEADS, tm), lambda b, t: (b, 0, t)), row_spec(RWKV_COLS)],
        out_shape=[head_shape, head_shape, head_shape,
                   jax.ShapeDtypeStruct((bx, tx, WIDTH), F32), jax.ShapeDtypeStruct((bx, tx, WIDTH), F32),
                   jax.ShapeDtypeStruct((bx, tx, N_HEADS), F32), jax.ShapeDtypeStruct((bx, N_HEADS, tx), F32),
                   jax.ShapeDtypeStruct((bx, tx, RWKV_COLS), F32)],
        compiler_params=_cparams("parallel", "arbitrary"),
    )(x3, sc, sh, g_mix, w_r, wfz_t, gq, gk, b_forget.reshape(1, N_HEADS), b_forget.reshape(N_HEADS, 1),
      _head_ones(N_HEADS))


def _cumsum_kernel(l_ref, u_ref, o_ref, c_ref, *, scale):
    tc = l_ref.shape[2]

    @pl.when(pl.program_id(1) == 0)
    def _():
        c_ref[...] = jnp.zeros_like(c_ref)

    f = _dot_sel3(l_ref[0], u_ref[...]) + c_ref[:, 0:1]
    o_ref[0] = f * scale
    c_ref[...] = jnp.broadcast_to(f[:, tc - 1:tc], c_ref.shape)


def _cumsum_t(lft, tc, scale):
    b, h, t = lft.shape
    r = lax.broadcasted_iota(jnp.int32, (tc, tc), 0)
    c = lax.broadcasted_iota(jnp.int32, (tc, tc), 1)
    upper = (r <= c).astype(BF16)
    return pl.pallas_call(
        functools.partial(_cumsum_kernel, scale=scale),
        grid=(b, t // tc),
        in_specs=[pl.BlockSpec((1, h, tc), lambda i, j: (i, 0, j)), pl.BlockSpec((tc, tc), lambda i, j: (0, 0))],
        out_specs=pl.BlockSpec((1, h, tc), lambda i, j: (i, 0, j)),
        out_shape=jax.ShapeDtypeStruct((b, h, t), F32),
        scratch_shapes=[pltpu.VMEM((h, 128), F32)],
        compiler_params=_cparams("parallel", "arbitrary"),
    )(lft, upper)


def _attn_kernel(q_ref, k_ref, v_ref, f_ref, o_ref, m_ref, acc_ref, *, tq, tk):
    i = pl.program_id(2)
    q2 = q_ref[0, 0]
    lane = lax.broadcasted_iota(jnp.int32, q2.shape, 1)
    zero = jnp.zeros_like(q2)
    qs = (jnp.where(lane < HEAD_DIM, q2, zero), jnp.where(lane >= HEAD_DIM, q2, zero))
    q0 = pl.multiple_of(i * tq, tq)
    f_base = [f_ref[0, 0, hh:hh + 1, pl.ds(q0, tq)][:, 0:1] for hh in range(2)]
    m_ref[...] = jnp.full_like(m_ref, NEG)
    acc_ref[...] = jnp.zeros_like(acc_ref)

    def tile(j, masked):
        k0 = pl.multiple_of(j * tk, tk)
        k2 = k_ref[0, 0, pl.ds(k0, tk), :]
        v2 = v_ref[0, 0, pl.ds(k0, tk), :]
        vlane = lax.broadcasted_iota(jnp.int32, v2.shape, 1)
        one = jnp.ones((), v2.dtype)
        vs = (jnp.where(vlane < HEAD_DIM, v2, one), jnp.where(vlane >= HEAD_DIM, v2, one))
        for hh in range(2):
            s = _dot_nt(qs[hh], k2) + (f_base[hh] - f_ref[0, 0, hh:hh + 1, pl.ds(k0, tk)])
            if masked:
                r = lax.broadcasted_iota(jnp.int32, s.shape, 0)
                c = lax.broadcasted_iota(jnp.int32, s.shape, 1)
                s = jnp.where(c + k0 <= r + q0, s, NEG)
            m_old = m_ref[hh]
            m_new = jnp.maximum(m_old, jnp.max(s, axis=1, keepdims=True))
            p = jnp.exp2(s - jnp.concatenate([m_new] * (tk // 128), axis=1))
            alpha = jnp.exp2(m_old - m_new)
            acc_ref[hh] = alpha * acc_ref[hh] + _dot(p.astype(BF16), vs[hh])
            m_ref[hh] = m_new

    def body(j, carry):
        tile(j, False)
        return carry

    n_sub = tq // tk
    lax.fori_loop(0, i * n_sub, body, 0)
    for dd in range(n_sub):
        tile(i * n_sub + dd, True)
    acc_a = acc_ref[0]
    acc_b = acc_ref[1]
    o_a = acc_a / pltpu.roll(acc_a, HEAD_DIM, 1)
    o_b = acc_b / pltpu.roll(acc_b, HEAD_DIM, 1)
    o_ref[0, 0] = jnp.where(lane < HEAD_DIM, o_a, o_b).astype(o_ref.dtype)


def _fox_prompt(qh, kh, vh, ft, tq, tk):
    b, hp, t, w = qh.shape
    assert tq % tk == 0 and tk % 128 == 0
    ft4 = ft.reshape(b, hp, 2, t)
    seq_spec = pl.BlockSpec((1, 1, t, w), lambda bi, hi, i: (bi, hi, 0, 0))
    tile_spec = pl.BlockSpec((1, 1, tq, w), lambda bi, hi, i: (bi, hi, i, 0))
    return pl.pallas_call(
        functools.partial(_attn_kernel, tq=tq, tk=tk),
        grid=(b, hp, t // tq),
        in_specs=[tile_spec, seq_spec, seq_spec, pl.BlockSpec((1, 1, 2, t), lambda bi, hi, i: (bi, hi, 0, 0))],
        out_specs=tile_spec,
        out_shape=jax.ShapeDtypeStruct((b, hp, t, w), BF16),
        scratch_shapes=[pltpu.VMEM((2, tq, 128), F32), pltpu.VMEM((2, tq, w), F32)],
        compiler_params=_cparams("parallel", "parallel", "arbitrary"),
    )(qh, kh, vh, ft4)


PAGES_PER_STEP = 16
QROWS = 4 * N_HEADS


def _decode_kernel(pt_ref, q_ref, kn_ref, vn_ref, lfn_ref, cnew_ref, ustrict_ref, fold_ref, *refs, n_steps):
    del pt_ref
    npg = PAGES_PER_STEP
    k_refs = refs[0:npg]
    v_refs = refs[npg:2 * npg]
    lf_refs = refs[2 * npg:3 * npg]
    o_ref, m_ref, l_ref, acc_ref, accn_ref, carry_ref = refs[3 * npg:]
    j = pl.program_id(1)
    q = q_ref[0]
    r = lax.broadcasted_iota(jnp.int32, (QROWS, WIDTH), 0)
    c = lax.broadcasted_iota(jnp.int32, (QROWS, WIDTH), 1)
    own_head = (r % N_HEADS) == (c // HEAD_DIM)
    q_bd = jnp.where(own_head, jnp.concatenate([q] * N_HEADS, axis=1), jnp.zeros((), q.dtype))

    def softmax_step(s):
        m_old = m_ref[...]
        m_new = jnp.maximum(m_old, jnp.max(s, axis=1, keepdims=True))
        p = jnp.exp(s - m_new)
        alpha = jnp.exp(m_old - m_new)
        l_ref[...] = alpha * l_ref[...] + jnp.sum(p, axis=1, keepdims=True)
        m_ref[...] = m_new
        return p.astype(BF16), alpha

    @pl.when(j == 0)
    def _():
        m_ref[...] = jnp.full_like(m_ref, NEG)
        l_ref[...] = jnp.zeros_like(l_ref)
        acc_ref[...] = jnp.zeros_like(acc_ref)
        carry_ref[...] = jnp.zeros_like(carry_ref)
        s = _dot_nt(q, kn_ref[0])
        bias = _dot_sel3(lfn_ref[0], cnew_ref[...])[0:1, :]
        rr = lax.broadcasted_iota(jnp.int32, s.shape, 0)
        cc = lax.broadcasted_iota(jnp.int32, s.shape, 1)
        ok = ((rr % N_HEADS) == (cc % N_HEADS)) & ((cc // N_HEADS) <= (rr // N_HEADS))
        p, _ = softmax_step(jnp.where(ok, s + bias, NEG))
        accn_ref[...] = _dot(p, vn_ref[0])

    ustrict = ustrict_ref[...]
    carry = carry_ref[...]
    scores = []
    for u in range(npg):
        kt = k_refs[u][...].reshape(WIDTH, PAGE).astype(BF16)
        lft = lf_refs[u][...]
        a = jnp.broadcast_to(lft[None], (4, N_HEADS, PAGE)).reshape(QROWS, PAGE)
        hi, mid, lo = _split3(a)
        bx = _dot(jnp.concatenate([hi, mid, lo], axis=0), ustrict)
        bias = bx[0:QROWS] + (bx[QROWS:2 * QROWS] + bx[2 * QROWS:3 * QROWS]) + carry
        carry = carry + jnp.sum(a, axis=1, keepdims=True)
        scores.append(_dot(q_bd, kt) + bias)
    carry_ref[...] = carry
    p, alpha = softmax_step(jnp.concatenate(scores, axis=1))
    acc = alpha * acc_ref[...]
    for u in range(npg):
        vt = v_refs[u][...].reshape(WIDTH, PAGE).astype(BF16)
        acc = acc + _dot_nt(p[:, u * PAGE:(u + 1) * PAGE], vt)
    acc_ref[...] = acc
    accn_ref[...] = alpha * accn_ref[...]

    @pl.when(j == n_steps - 1)
    def _():
        own = jnp.where(own_head, acc_ref[...], 0.0)
        o_ref[0] = (_dot_sel3(own, fold_ref[...]) + accn_ref[...]) / l_ref[...]


def _fox_decode(page_table, q, k_new, v_new, lf_new, cache_kt, cache_vt, cache_lft):
    nb, n_pages = page_table.shape
    npg = PAGES_PER_STEP
    n_steps = n_pages // npg
    ustrict = (lax.broadcasted_iota(jnp.int32, (PAGE, PAGE), 0)
               > lax.broadcasted_iota(jnp.int32, (PAGE, PAGE), 1)).astype(BF16)
    ri = lax.broadcasted_iota(jnp.int32, (QROWS, QROWS), 0)
    ci = lax.broadcasted_iota(jnp.int32, (QROWS, QROWS), 1)
    cnew = -(((ri % N_HEADS) == (ci % N_HEADS)) & ((ri // N_HEADS) <= (ci // N_HEADS))).astype(BF16)
    fold = (lax.broadcasted_iota(jnp.int32, (WIDTH, HEAD_DIM), 0) % HEAD_DIM
            == lax.broadcasted_iota(jnp.int32, (WIDTH, HEAD_DIM), 1)).astype(BF16)

    def page_idx(u):
        return lambda b, j, pt: (pt[b, n_pages - 1 - (j * npg + u)], 0, 0, 0)

    def page_idx3(u):
        return lambda b, j, pt: (pt[b, n_pages - 1 - (j * npg + u)], 0, 0)

    per_b = lambda shape: pl.BlockSpec((1,) + shape, lambda b, j, pt: (b, 0, 0))
    const = lambda shape: pl.BlockSpec(shape, lambda b, j, pt: (0, 0))
    in_specs = ([per_b((QROWS, HEAD_DIM)), per_b((QROWS, HEAD_DIM)), per_b((QROWS, HEAD_DIM)),
                 per_b((8, QROWS)), const((QROWS, QROWS)), const((PAGE, PAGE)), const((WIDTH, HEAD_DIM))]
                + [pl.BlockSpec((None, N_HEADS, HEAD_DIM, PAGE), page_idx(u)) for u in range(npg)]
                + [pl.BlockSpec((None, N_HEADS, HEAD_DIM, PAGE), page_idx(u)) for u in range(npg)]
                + [pl.BlockSpec((None, N_HEADS, PAGE), page_idx3(u)) for u in range(npg)])
    grid_spec = pltpu.PrefetchScalarGridSpec(
        num_scalar_prefetch=1,
        grid=(nb, n_steps),
        in_specs=in_specs,
        out_specs=pl.BlockSpec((1, QROWS, HEAD_DIM), lambda b, j, pt: (b, 0, 0)),
        scratch_shapes=[pltpu.VMEM((QROWS, 1), F32), pltpu.VMEM((QROWS, 1), F32),
                        pltpu.VMEM((QROWS, WIDTH), F32), pltpu.VMEM((QROWS, HEAD_DIM), F32),
                        pltpu.VMEM((QROWS, 1), F32)],
    )
    lfn = jnp.broadcast_to(lf_new.reshape(nb, 1, QROWS), (nb, 8, QROWS))
    return pl.pallas_call(
        functools.partial(_decode_kernel, n_steps=n_steps),
        grid_spec=grid_spec,
        out_shape=jax.ShapeDtypeStruct((nb, QROWS, HEAD_DIM), F32),
        compiler_params=_cparams("parallel", "arbitrary"),
    )(page_table, q, k_new, v_new, lfn, cnew, ustrict, fold,
      *([cache_kt] * npg), *([cache_vt] * npg), *([cache_lft] * npg))


def _rwkv_prep_kernel(p_ref, prev_ref, first_ref, mu_ref, w0_ref, w2_ref, a0_ref, a2_ref, g2_ref, kk_ref, ka_ref,
                      rk_ref, hs_ref, r_ref, k_ref, v_ref, w_ref, a_ref, b_ref, g_ref, bonus_ref, *, shift_rows):
    p = p_ref[0]
    if shift_rows:
        carry = jnp.where(pl.program_id(1) == 0, first_ref[0], prev_ref[0, 7:8, :])
        row0 = lax.broadcasted_iota(jnp.int32, p.shape, 0) == 0
        prev = jnp.where(row0, carry, pltpu.roll(p, 1, 0))
    else:
        prev = prev_ref[0]
    z = p + (prev - p) * mu_ref[...]
    r = z[:, 0:WIDTH]
    k = z[:, WIDTH:2 * WIDTH]
    v = z[:, 2 * WIDTH:3 * WIDTH]
    lora = z[:, 3 * WIDTH:3 * WIDTH + DECAY_LORA + ICLR_LORA]
    gd = z[:, 3 * WIDTH + DECAY_LORA + ICLR_LORA:RWKV_COLS]
    hs = hs_ref[...]
    w = -_softplus(-(w0_ref[...] + _dot_x3(jnp.tanh(lora), w2_ref[...]))) - 0.5
    decay = jnp.exp(-jnp.exp(w))
    a = _sigmoid(a0_ref[...] + _dot_x3(lora, a2_ref[...]))
    g = _dot_x3(_sigmoid(gd), g2_ref[...])
    kk = k * kk_ref[...]
    norm = jnp.sqrt(_dot_sel3(kk * kk, hs))
    kk = kk / jnp.maximum(norm, 1e-12)
    k = k * (1.0 + (a - 1.0) * ka_ref[...])
    bonus = _dot_sel3(r * k * rk_ref[...], hs) * v
    r_ref[0] = r
    k_ref[0] = k
    v_ref[0] = v
    w_ref[0] = decay
    a_ref[0] = -kk
    b_ref[0] = kk * a
    g_ref[0] = g
    bonus_ref[0] = bonus


def _rwkv_prep(p_rw, prev, first, mu, w0, w2p, a0, a2p, g2, kk, ka, rk, tm):
    bx, tx, _ = p_rw.shape
    row = lambda w: pl.BlockSpec((1, tm, w), lambda b, t: (b, t, 0))
    const = lambda shape: pl.BlockSpec(shape, lambda b, t: tuple(0 for _ in shape))
    out = jax.ShapeDtypeStruct((bx, tx, WIDTH), F32)
    shift_rows = prev is None
    if shift_rows:
        prev = p_rw
        prev_spec = pl.BlockSpec((1, 8, RWKV_COLS), lambda b, t: (b, jnp.maximum(t * (tm // 8) - 1, 0), 0))
    else:
        prev_spec = row(RWKV_COLS)
    return pl.pallas_call(
        functools.partial(_rwkv_prep_kernel, shift_rows=shift_rows),
        grid=(bx, tx // tm),
        in_specs=[row(RWKV_COLS), prev_spec, pl.BlockSpec((1, 1, RWKV_COLS), lambda b, t: (b, 0, 0)),
                  const((1, RWKV_COLS)), const((1, WIDTH)),
                  const((DECAY_LORA + ICLR_LORA, WIDTH)), const((1, WIDTH)), const((DECAY_LORA + ICLR_LORA, WIDTH)),
                  const((GATE_LORA, WIDTH)), const((1, WIDTH)), const((1, WIDTH)), const((1, WIDTH)),
                  const((WIDTH, WIDTH))],
        out_specs=[row(WIDTH)] * 8,
        out_shape=[out] * 8,
        compiler_params=_cparams("parallel", "arbitrary"),
    )(p_rw, prev, first, mu, w0, w2p, a0, a2p, g2, kk, ka, rk, _head_ones(N_HEADS))


WKV_NB = 2
WKV_HG = 4
WKV_LANES = WKV_HG * HEAD_DIM
WKV_GROUPS = N_HEADS // WKV_HG
WKV_ROWS = WKV_NB * WKV_GROUPS * HEAD_DIM


def _wkv_kernel(r_ref, k_ref, v_ref, w_ref, a_ref, b_ref, s0_ref, ones_ref, y_ref, sT_ref, st_ref, *, tc, n_t):
    ti = pl.program_id(1)
    blocks = [(bi, gg) for bi in range(WKV_NB) for gg in range(WKV_GROUPS)]

    @pl.when(ti == 0)
    def _():
        for n, (bi, gg) in enumerate(blocks):
            st_ref[n * HEAD_DIM:(n + 1) * HEAD_DIM, :] = jnp.concatenate(
                [s0_ref[bi, gg * WKV_HG + q] for q in range(WKV_HG)], axis=1)

    rows = lax.broadcasted_iota(jnp.int32, (WKV_ROWS, WKV_LANES), 0)
    lanes = lax.broadcasted_iota(jnp.int32, (WKV_ROWS, WKV_LANES), 1)
    eye = (rows % HEAD_DIM) == (lanes % HEAD_DIM)
    eye_f = eye.astype(F32)
    ones = ones_ref[...]
    grp = min(tc, 8)

    def bcast(xs, u):
        return jnp.concatenate(
            [jnp.broadcast_to(xs[bi][u:u + 1, gg * WKV_LANES:(gg + 1) * WKV_LANES], (HEAD_DIM, WKV_LANES))
             for bi, gg in blocks], axis=0)

    def group(gi, carry):
        t0 = pl.multiple_of(gi * grp, grp)
        load = lambda ref: [ref[bi, pl.ds(t0, grp), :] for bi in range(WKV_NB)]
        r8, k8, v8, w8, a8, b8 = (load(ref) for ref in (r_ref, k_ref, v_ref, w_ref, a_ref, b_ref))
        vdiag = lambda u: jnp.where(eye, bcast(v8, u), 0.0).astype(BF16)
        s = st_ref[...]
        ys = []

        def y_rows(yb):
            yb = yb * eye_f
            return [jnp.sum(yb[n * HEAD_DIM:(n + 1) * HEAD_DIM], axis=0, keepdims=True) for n in range(len(blocks))]

        vcol = _dot(vdiag(0), ones)
        ybf = None
        for u in range(grp):
            sa = _dot((s * bcast(a8, u)).astype(BF16), ones)
            side = [vdiag(u + 1)] if u + 1 < grp else []
            if ybf is not None:
                side.append(ybf)
            misc = _dot(jnp.concatenate(side, axis=0), ones) if side else None
            s = s * bcast(w8, u) + sa * bcast(b8, u) + vcol * bcast(k8, u)
            if misc is not None:
                off = 0
                if u + 1 < grp:
                    vcol = misc[0:WKV_ROWS]
                    off = WKV_ROWS
                if ybf is not None:
                    ys.append(y_rows(misc[off:off + WKV_ROWS]))
            ybf = (s * bcast(r8, u)).astype(BF16)
        ys.append(y_rows(_dot(ybf, ones)))
        st_ref[...] = s
        for bi in range(WKV_NB):
            y_ref[bi, pl.ds(t0, grp), :] = jnp.concatenate(
                [jnp.concatenate([ys[u][bi * WKV_GROUPS + gg] for gg in range(WKV_GROUPS)], axis=1)
                 for u in range(grp)], axis=0)
        return carry

    lax.fori_loop(0, tc // grp, group, 0)

    @pl.when(ti == n_t - 1)
    def _():
        for n, (bi, gg) in enumerate(blocks):
            for q in range(WKV_HG):
                sT_ref[bi, gg * WKV_HG + q] = st_ref[n * HEAD_DIM:(n + 1) * HEAD_DIM, q * HEAD_DIM:(q + 1) * HEAD_DIM]


def _wkv_scan(r, k, v, w, a, b, s0, tc):
    bx, tx, _ = r.shape
    assert bx % WKV_NB == 0
    n_t = tx // tc
    seq = pl.BlockSpec((WKV_NB, tc, WIDTH), lambda i, t: (i, t, 0))
    st = pl.BlockSpec((WKV_NB, N_HEADS, HEAD_DIM, HEAD_DIM), lambda i, t: (i, 0, 0, 0))
    ri = lax.broadcasted_iota(jnp.int32, (WKV_LANES, WKV_LANES), 0) // HEAD_DIM
    ci = lax.broadcasted_iota(jnp.int32, (WKV_LANES, WKV_LANES), 1) // HEAD_DIM
    ones = (ri == ci).astype(BF16)
    return pl.pallas_call(
        functools.partial(_wkv_kernel, tc=tc, n_t=n_t),
        grid=(bx // WKV_NB, n_t),
        in_specs=[seq] * 6 + [st, pl.BlockSpec((WKV_LANES, WKV_LANES), lambda i, t: (0, 0))],
        out_specs=[seq, st],
        out_shape=[jax.ShapeDtypeStruct((bx, tx, WIDTH), F32),
                   jax.ShapeDtypeStruct((bx, N_HEADS, HEAD_DIM, HEAD_DIM), F32)],
        scratch_shapes=[pltpu.VMEM((WKV_ROWS, WKV_LANES), F32)],
        compiler_params=_cparams("parallel", "arbitrary"),
    )(r, k, v, w, a, b, s0, ones)


def _outproj_kernel(x_ref, of_ref, y_ref, bonus_ref, g_ref, gt_ref, sc_ref, sh_ref, lw_ref, lb_ref, gf_ref,
                    wo_ref, wr_ref, br_ref, hs_ref, lt_ref, x1_ref, h2_ref, ti_ref, tw_ref, rk_ref, cnt_ref, run_ref):
    @pl.when(pl.program_id(1) == 0)
    def _():
        run_ref[...] = jnp.zeros_like(run_ref)

    hs = hs_ref[...]
    y = y_ref[0]
    mu = _dot_sel2(y, hs) * (1.0 / HEAD_DIM)
    yc = y - mu
    var = _dot_sel2(yc * yc, hs) * (1.0 / HEAD_DIM)
    o_rw = ((yc * lax.rsqrt(var + GN_EPS)) * lw_ref[...] + lb_ref[...] + bonus_ref[0]) * g_ref[0]
    mixed = _dot(o_rw.astype(BF16), wo_ref[WIDTH:2 * WIDTH, :])
    for hp in range(N_HEAD_PAIRS):
        mixed = mixed + _dot(of_ref[0, hp], wo_ref[hp * PAIR:(hp + 1) * PAIR, :])
    x1 = x_ref[0] + gt_ref[0] * mixed
    x1_ref[0] = x1
    ms = jnp.mean(x1 * x1, axis=-1, keepdims=True)
    h2 = (x1 * lax.rsqrt(ms + RMS_EPS) * gf_ref[...]) * (1.0 + sc_ref[0]) + sh_ref[0]
    h2_ref[0] = h2.astype(BF16)
    logits = _dot_x3(h2, wr_ref[...]) + br_ref[...]
    lane = lax.broadcasted_iota(jnp.int32, logits.shape, 1).astype(F32)
    vals, idxs = [], []
    for _ in range(TOP_K):
        m = jnp.max(logits, axis=1, keepdims=True)
        idx = jnp.min(jnp.where(logits == m, lane, float(N_EXPERTS)), axis=1, keepdims=True)
        vals.append(m)
        idxs.append(idx)
        logits = jnp.where(lane == idx, -jnp.inf, logits)
    e = [jnp.exp(vv - vals[0]) for vv in vals]
    tot = e[0] + e[1] + e[2] + e[3]
    tw_ref[0] = jnp.concatenate(e, axis=1) / tot
    ti_ref[0] = jnp.concatenate(idxs, axis=1).astype(jnp.int32)
    chosen = [lane == idx for idx in idxs]
    multi_hot = jnp.where(chosen[0] | chosen[1] | chosen[2] | chosen[3], 1.0, 0.0)
    before = _dot(lt_ref[...], multi_hot.astype(BF16)) + run_ref[...]
    ranks = [jnp.sum(jnp.where(c, before, 0.0), axis=1, keepdims=True) for c in chosen]
    rk_ref[0] = jnp.concatenate(ranks, axis=1).astype(jnp.int32)
    run_ref[...] = run_ref[...] + jnp.sum(multi_hot, axis=0, keepdims=True)
    cnt_ref[0] = run_ref[...]


def _outproj(x3, of, y, bonus, g, gt, sc, sh, lnx_w, lnx_b, g_ffn, w_out_b, w_router, b_router, tm):
    bx, tx, d = x3.shape
    per_row = gt.shape[1] != 1
    mod_spec = (pl.BlockSpec((1, tm, d), lambda b, t: (b, t, 0)) if per_row
                else pl.BlockSpec((1, 1, d), lambda b, t: (b, 0, 0)))
    row = lambda w: pl.BlockSpec((1, tm, w), lambda b, t: (b, t, 0))
    const = lambda shape: pl.BlockSpec(shape, lambda b, t: tuple(0 for _ in shape))
    lower = (lax.broadcasted_iota(jnp.int32, (tm, tm), 0) > lax.broadcasted_iota(jnp.int32, (tm, tm), 1)).astype(BF16)
    return pl.pallas_call(
        _outproj_kernel,
        grid=(bx, tx // tm),
        in_specs=[row(d), pl.BlockSpec((1, N_HEAD_PAIRS, tm, PAIR), lambda b, t: (b, 0, t, 0)),
                  row(WIDTH), row(WIDTH), row(WIDTH), mod_spec, mod_spec, mod_spec,
                  const((1, WIDTH)), const((1, WIDTH)), const((1, d)), const((2 * WIDTH, d)),
                  const((d, N_EXPERTS)), const((1, N_EXPERTS)), const((WIDTH, WIDTH)), const((tm, tm))],
        out_specs=[row(d), row(d), row(TOP_K), row(TOP_K), row(TOP_K),
                   pl.BlockSpec((1, 1, N_EXPERTS), lambda b, t: (b, 0, 0))],
        out_shape=[jax.ShapeDtypeStruct((bx, tx, d), F32), jax.ShapeDtypeStruct((bx, tx, d), BF16),
                   jax.ShapeDtypeStruct((bx, tx, TOP_K), jnp.int32), jax.ShapeDtypeStruct((bx, tx, TOP_K), F32),
                   jax.ShapeDtypeStruct((bx, tx, TOP_K), jnp.int32), jax.ShapeDtypeStruct((bx, 1, N_EXPERTS), F32)],
        scratch_shapes=[pltpu.VMEM((1, N_EXPERTS), F32)],
        compiler_params=_cparams("parallel", "arbitrary"),
    )(x3, of, y, bonus, g, gt, sc, sh, lnx_w, lnx_b, g_ffn, w_out_b, w_router, b_router, _head_ones(N_HEADS), lower)


MOE_TM = 512


def _moe_kernel(te_ref, tf_ref, x_ref, wg_ref, bg_ref, wu_ref, bu_ref, wd_ref, bd_ref, o_ref, wb_ref):
    del te_ref
    i = pl.program_id(0)

    @pl.when(tf_ref[i] == 2)
    def _():
        wb_ref[0] = wg_ref[0].astype(BF16)
        wb_ref[1] = wu_ref[0].astype(BF16)
        wb_ref[2] = wd_ref[0].astype(BF16)

    @pl.when(tf_ref[i] != 0)
    def _():
        x = x_ref[...]
        glu = jnp.minimum(_dot(x, wb_ref[0]) + bg_ref[0], SWIGLU_LIMIT)
        lin = jnp.clip(_dot(x, wb_ref[1]) + bu_ref[0], -SWIGLU_LIMIT, SWIGLU_LIMIT)
        act = glu * _sigmoid(SWIGLU_ALPHA * glu) * (lin + 1.0)
        o_ref[...] = (_dot(act.astype(BF16), wb_ref[2]) + bd_ref[0]).astype(o_ref.dtype)

    @pl.when(tf_ref[i] == 0)
    def _():
        o_ref[...] = jnp.zeros_like(o_ref)


def _moe_gemm(tile_expert, tile_flag, xg, wg, bg, wu, bu, wd, bd):
    n_slots, d = xg.shape
    n_tiles = n_slots // MOE_TM
    wspec = pl.BlockSpec((1, d, d), lambda i, te, tf: (te[i], 0, 0))
    bspec = pl.BlockSpec((1, 1, d), lambda i, te, tf: (te[i], 0, 0))
    grid_spec = pltpu.PrefetchScalarGridSpec(
        num_scalar_prefetch=2,
        grid=(n_tiles,),
        in_specs=[pl.BlockSpec((MOE_TM, d), lambda i, te, tf: (i, 0)), wspec, bspec, wspec, bspec, wspec, bspec],
        out_specs=pl.BlockSpec((MOE_TM, d), lambda i, te, tf: (i, 0)),
        scratch_shapes=[pltpu.VMEM((3, d, d), BF16)],
    )
    return pl.pallas_call(
        _moe_kernel,
        grid_spec=grid_spec,
        out_shape=jax.ShapeDtypeStruct((n_slots, d), BF16),
        compiler_params=_cparams("arbitrary"),
    )(tile_expert, tile_flag, xg, wg, bg, wu, bu, wd, bd)


def _final_kernel(x1_ref, yg_ref, tw_ref, gt_ref, o_ref):
    tw = tw_ref[0]
    moe = tw[:, 0:1] * yg_ref[0, 0].astype(F32)
    for kk in range(1, TOP_K):
        moe = moe + tw[:, kk:kk + 1] * yg_ref[kk, 0].astype(F32)
    o_ref[0] = x1_ref[0] + gt_ref[0] * moe


def _final(x1, yg, tw, gt, tm):
    bx, tx, d = x1.shape
    per_row = gt.shape[1] != 1
    mod_spec = (pl.BlockSpec((1, tm, d), lambda b, t: (b, t, 0)) if per_row
                else pl.BlockSpec((1, 1, d), lambda b, t: (b, 0, 0)))
    row = pl.BlockSpec((1, tm, d), lambda b, t: (b, t, 0))
    return pl.pallas_call(
        _final_kernel,
        grid=(bx, tx // tm),
        in_specs=[row, pl.BlockSpec((TOP_K, 1, tm, d), lambda b, t: (0, b, t, 0)),
                  pl.BlockSpec((1, tm, TOP_K), lambda b, t: (b, t, 0)), mod_spec],
        out_specs=row,
        out_shape=jax.ShapeDtypeStruct((bx, tx, d), F32),
        compiler_params=_cparams("parallel", "arbitrary"),
    )(x1, yg, tw, gt)


def _dispatch_plan(top_i, rank_in_seg, seg_counts, seg_tokens):
    n_assign = top_i.size
    n_tiles = -(-n_assign // MOE_TM) + N_EXPERTS
    n_slots = n_tiles * MOE_TM
    e_flat = top_i.reshape(-1)
    order = jnp.argsort(e_flat, stable=True).astype(jnp.int32)
    counts = jnp.sum(seg_counts, axis=0)
    cnt_start = jnp.cumsum(counts) - counts
    seg_base = jnp.cumsum(seg_counts, axis=0) - seg_counts
    padded = ((counts + MOE_TM - 1) // MOE_TM) * MOE_TM
    pad_end = jnp.cumsum(padded)
    pad_start = pad_end - padded
    experts = jnp.arange(N_EXPERTS, dtype=jnp.int32)
    tile_start = jnp.arange(n_tiles, dtype=jnp.int32) * MOE_TM
    tile_expert = jnp.minimum(jnp.sum((tile_start[:, None] >= pad_end[None, :]).astype(jnp.int32), axis=1),
                              N_EXPERTS - 1)
    tile_live = tile_start < pad_end[-1]
    is_first = jnp.concatenate([jnp.ones((1,), bool), tile_expert[1:] != tile_expert[:-1]])
    tile_flag = jnp.where(tile_live, jnp.where(is_first, 2, 1), 0).astype(jnp.int32)
    of_tile = lambda tab: jnp.sum(jnp.where(tile_expert[:, None] == experts[None, :], tab[None, :], 0), axis=1)
    rank = (tile_start - of_tile(pad_start))[:, None] + jnp.arange(MOE_TM, dtype=jnp.int32)[None, :]
    live = rank < of_tile(counts)[:, None]
    src = order.at[jnp.clip(of_tile(cnt_start)[:, None] + rank, 0, n_assign - 1)].get(mode="promise_in_bounds")
    filler = (tile_start[:, None] + jnp.arange(MOE_TM, dtype=jnp.int32)[None, :]) % (n_assign // TOP_K)
    token_of_slot = jnp.where(live, src // TOP_K, filler).reshape(n_slots)
    tok_base = jnp.repeat(seg_base + pad_start[None, :], jnp.asarray(seg_tokens), axis=0,
                          total_repeat_length=sum(seg_tokens))
    hit = top_i[:, :, None] == experts[None, None, :]
    slot_of_assign = jnp.sum(jnp.where(hit, tok_base[:, None, :], 0), axis=-1) + rank_in_seg
    return slot_of_assign, token_of_slot, tile_expert, tile_flag


def _pick_tile(n, pref):
    t = min(n, pref)
    assert n % t == 0, (n, t)
    return t


def kernel(x_prompt, x_sample, cache_k, cache_v, cache_logf, state_wkv, state_shift, page_table, c_prompt,
           c_sample, w_mod, b_mod, g_mix, g_ffn, w_in, q_norm, k_norm, b_forget, rw_mu, rw_w0, rw_w2, rw_a0,
           rw_a2, rw_g2, rw_kk, rw_ka, rw_rk, lnx_w, lnx_b, w_out, w_router, b_router, w_gate, b_gate, w_up,
           b_up, w_down, b_down):
    depth = w_mod.shape[0]
    assert depth == 1
    n_p, t_p, d = x_prompt.shape
    n_s, t_s, _ = x_sample.shape
    assert d == D_MODEL and t_s * N_HEADS == QROWS
    l = 0

    w_in_l = w_in[l]
    w_r = jnp.concatenate([w_in_l[:, :3 * WIDTH], w_in_l[:, FOX_COLS:], w_in_l[:, 3 * WIDTH:FOX_COLS],
                           jnp.zeros((d, 128 - N_HEADS), F32)], axis=1).astype(BF16)
    wfz_t = w_in_l[:, 3 * WIDTH:FOX_COLS].T.astype(BF16)
    gq = (jnp.tile(q_norm[l], N_HEADS) * ATTN_SCALE).reshape(1, WIDTH)
    gk = jnp.tile(k_norm[l], N_HEADS).reshape(1, WIDTH)
    zpad = jnp.zeros((DECAY_LORA, WIDTH), F32)
    w2p = jnp.concatenate([rw_w2[l], zpad], axis=0)
    a2p = jnp.concatenate([zpad, rw_a2[l]], axis=0)
    row = lambda a: a.reshape(1, -1)

    n_c = n_p + n_s
    pad_c = (-n_c) % 8
    c_all = jnp.concatenate([c_sample, c_prompt, jnp.zeros((pad_c, d), F32)], axis=0)
    mod = _mod(c_all, w_mod[l], b_mod[l])
    mod_s = mod[:n_s].reshape(n_s, 1, N_MOD, d)
    mod_p = mod[n_s:n_s + n_p].reshape(n_p, 1, N_MOD, d)
    mod_s = jnp.broadcast_to(mod_s, (n_s, t_s, N_MOD, d)).reshape(1, n_s * t_s, N_MOD, d)
    mp = [mod_p[:, :, i] for i in range(N_MOD)]
    msm = [mod_s[:, :, i] for i in range(N_MOD)]

    xs3 = x_sample.reshape(1, n_s * t_s, d)
    tm_p = _pick_tile(t_p, 256)
    tm_s = _pick_tile(n_s * t_s, 256)

    proj_args = (row(g_mix[l]), w_r, wfz_t, gq, gk, b_forget[l])
    proj_args_p = (row(g_mix[l]), w_r, wfz_t, gq * LOG2E, gk, b_forget[l])
    qh_p, kh_p, vh_p, k_p, v_p, lf_p, lft_p, rw_p = _inproj(x_prompt, mp[1], mp[0], *proj_args_p, tm_p)
    qh_s, _, _, k_s, v_s, lf_s, _, rw_s = _inproj(xs3, msm[1], msm[0], *proj_args, tm_s)

    ft = _cumsum_t(lft_p, _pick_tile(t_p, 512), LOG2E)
    tq = _pick_tile(t_p, 1024)
    of_p = _fox_prompt(qh_p, kh_p, vh_p, ft, tq, tq)

    q_s = jnp.transpose(qh_s[0], (1, 0, 2)).reshape(n_s, QROWS, HEAD_DIM)
    kn_s = k_s.reshape(n_s, QROWS, HEAD_DIM).astype(BF16)
    vn_s = v_s.reshape(n_s, QROWS, HEAD_DIM).astype(BF16)
    cache_kt = jnp.transpose(cache_k[l], (0, 2, 3, 1))
    cache_vt = jnp.transpose(cache_v[l], (0, 2, 3, 1))
    cache_lft = jnp.swapaxes(cache_logf[l], 1, 2)
    o_s = _fox_decode(page_table, q_s, kn_s, vn_s, lf_s.reshape(n_s, QROWS), cache_kt, cache_vt, cache_lft)
    of_s = jnp.transpose(o_s.reshape(n_s * t_s, N_HEAD_PAIRS, PAIR), (1, 0, 2))[None].astype(BF16)

    prep_w = (row(rw_mu[l]), row(rw_w0[l]), w2p, row(rw_a0[l]), a2p, rw_g2[l], row(rw_kk[l]), row(rw_ka[l]),
              row(rw_rk[l]))
    rw_s4 = rw_s.reshape(n_s, t_s, RWKV_COLS)
    prev_s = jnp.concatenate([state_shift[l][:, None, :], rw_s4[:, :-1]], axis=1).reshape(1, n_s * t_s, RWKV_COLS)
    r_p, kk_p, vv_p, w_p, a_p, b_p, g_p, bonus_p = _rwkv_prep(
        rw_p, None, jnp.zeros((n_p, 1, RWKV_COLS), F32), *prep_w, tm_p)
    r_s, kk_s, vv_s, w_s, a_s, b_s, g_s, bonus_s = _rwkv_prep(
        rw_s, prev_s, jnp.zeros((1, 1, RWKV_COLS), F32), *prep_w, tm_s)
    y_p, wkv_p = _wkv_scan(r_p, kk_p, vv_p, w_p, a_p, b_p,
                           jnp.zeros((n_p, N_HEADS, HEAD_DIM, HEAD_DIM), F32), _pick_tile(t_p, 256))
    to_seq = lambda a: a.reshape(n_s, t_s, WIDTH)
    y_s, wkv_s = _wkv_scan(to_seq(r_s), to_seq(kk_s), to_seq(vv_s), to_seq(w_s), to_seq(a_s), to_seq(b_s),
                           state_wkv[l], t_s)
    y_s = y_s.reshape(1, n_s * t_s, WIDTH)

    out_w = (row(lnx_w[l]), row(lnx_b[l]), row(g_ffn[l]), w_out[l].astype(BF16), w_router[l], row(b_router[l]))
    x1_p, h2_p, ti_p, tw_p, rk_p, cnt_p = _outproj(x_prompt, of_p, y_p, bonus_p, g_p, mp[2], mp[4], mp[3],
                                                   *out_w, tm_p)
    x1_s, h2_s, ti_s, tw_s, rk_s, cnt_s = _outproj(xs3, of_s, y_s, bonus_s, g_s, msm[2], msm[4], msm[3],
                                                   *out_w, tm_s)

    experts_w = (w_gate[l], b_gate[l].reshape(N_EXPERTS, 1, d), w_up[l], b_up[l].reshape(N_EXPERTS, 1, d),
                 w_down[l], b_down[l].reshape(N_EXPERTS, 1, d))
    take_rows = lambda a, idx: a.at[idx].get(mode="promise_in_bounds")

    def moe(h2, top_i, rank, counts, seg_tokens):
        n_tok = sum(seg_tokens)
        slot_of_assign, token_of_slot, tile_expert, tile_flag = _dispatch_plan(
            top_i.reshape(n_tok, TOP_K), rank.reshape(n_tok, TOP_K),
            counts.reshape(len(seg_tokens), N_EXPERTS).astype(jnp.int32), seg_tokens)
        ys = _moe_gemm(tile_expert, tile_flag, take_rows(h2.reshape(n_tok, d), token_of_slot), *experts_w)
        return take_rows(ys, slot_of_assign.T)

    yg_p = moe(h2_p, ti_p, rk_p, cnt_p, [t_p] * n_p)
    yg_s = moe(h2_s, ti_s, rk_s, cnt_s, [n_s * t_s])
    y_prompt = _final(x1_p, yg_p.reshape(TOP_K, n_p, t_p, d), tw_p, mp[5], tm_p)
    y_sample = _final(x1_s, yg_s.reshape(TOP_K, 1, n_s * t_s, d), tw_s, msm[5], tm_s)

    heads5 = lambda a, nb, nt: a.reshape(1, nb, nt, N_HEADS, HEAD_DIM)
    return (y_prompt, y_sample.reshape(n_s, t_s, d),
            heads5(k_p, n_p, t_p), heads5(v_p, n_p, t_p), lf_p[None], wkv_p[None], rw_p[:, -1][None],
            heads5(k_s, n_s, t_s), heads5(v_s, n_s, t_s), lf_s.reshape(1, n_s, t_s, N_HEADS), wkv_s[None],
            rw_s4[:, -1][None])
```
